```python
import math
import jax, jax.numpy as jnp
from jax import lax
import numpy as np

D_MODEL = 1024
BATCH = 32
SEQ = 2048
DEPTH = 2
DEC_BATCH = 8
DEC_SEQ = 2048
PAST_LEN = 128

GRID_W = 64
Q_BLOCK = 128
HEAD_DIM = 64
HY_WIDTH = D_MODEL // 4
HY_ORDER = 2
SHORT_CONV = 3
N_BANDS = 16
FILTER_EMB = 1 + 2 * N_BANDS
FILTER_HID = 64
DECAY_TARGET = 1e-2
FAST_DECAY_PCT = 0.3
SLOW_DECAY_PCT = 1.5
MIN_DECAY = math.log(DECAY_TARGET) / SLOW_DECAY_PCT
MAX_DECAY = math.log(DECAY_TARGET) / FAST_DECAY_PCT
DECAY_SHIFT = 0.05
DA_HEADS = 4
DA_HALF = HEAD_DIM // 2
DA_WIDTH = DA_HEADS * HEAD_DIM
GQ_HEADS = 8
GQ_KV = 2
GQ_GROUP = GQ_HEADS // GQ_KV
GQ_WIDTH = GQ_HEADS * HEAD_DIM
ROPE_AXIS = HEAD_DIM // 2
ROPE_THETA = 10000.0
HY_COLS = (HY_ORDER + 1) * HY_WIDTH
DA_COLS = 3 * DA_WIDTH
GQ_COLS = GQ_WIDTH + 2 * GQ_KV * HEAD_DIM
N_BRANCH = 3
GATE_COLS = N_BRANCH * D_MODEL
IN_COLS = HY_COLS + DA_COLS + GQ_COLS + GATE_COLS
D_FF = 256 * ((8 * D_MODEL // 3 + 255) // 256)
N_EXPERTS = 8
TOP_K = 2
MOE_FF = D_FF // 2
N_DENSE = (DEPTH + 1) // 2
N_MOE = DEPTH // 2
DEEPNORM_ALPHA = (2.0 * DEPTH) ** 0.25
DEEPNORM_BETA = (8.0 * DEPTH) ** -0.25
EPS = 1e-5

kernel_name = 'hybrid_hyena_diffattn_gqa_encoder'


def _layer_norm(x, g=None, b=None):
    xf = x.astype(jnp.float32)
    mu = jnp.mean(xf, axis=-1, keepdims=True)
    var = jnp.mean(jnp.square(xf - mu), axis=-1, keepdims=True)
    y = (xf - mu) * lax.rsqrt(var + EPS)
    if g is not None:
        y = y * g.astype(jnp.float32) + b.astype(jnp.float32)
    return y.astype(x.dtype)


def _rms_norm(x, g):
    xf = x.astype(jnp.float32)
    y = xf * lax.rsqrt(jnp.mean(jnp.square(xf), axis=-1, keepdims=True) + EPS)
    return (y * g.astype(jnp.float32)).astype(x.dtype)


def _swiglu(x, w_up, w_down):
    g, u = jnp.split(x @ w_up, 2, axis=-1)
    return (jax.nn.silu(g) * u) @ w_down


def _short_conv(z, w, b):
    L = z.shape[1]
    pad = SHORT_CONV // 2
    zp = jnp.pad(z, ((0, 0), (pad, SHORT_CONV - 1 - pad), (0, 0)))
    out = b
    for j in range(SHORT_CONV):
        out = out + zp[:, j:j + L] * w[j]
    return out


def _hyena_filters(L, w1, b1, w2, b2, w3, b3):
    f32 = jnp.float32
    t_idx = jnp.arange(L, dtype=f32)
    t_norm = t_idx / max(L - 1, 1)
    bands = jnp.linspace(1e-4, N_BANDS - 1, N_BANDS, dtype=f32)
    ang = (2.0 * math.pi / L) * t_idx[:, None] * bands[None, :]
    feats = jnp.concatenate([t_norm[:, None], jnp.cos(ang), -jnp.sin(ang)], axis=-1)
    h = jnp.sin(feats @ w1.astype(f32) + b1.astype(f32))
    h = jnp.sin(h @ w2.astype(f32) + b2.astype(f32))
    h = (h @ w3.astype(f32) + b3.astype(f32)).reshape(L, 2, HY_ORDER, HY_WIDTH)
    deltas = jnp.abs(jnp.linspace(MIN_DECAY, MAX_DECAY, HY_WIDTH, dtype=f32))
    window = jnp.exp(-t_norm[:, None] * deltas[None, :]) + DECAY_SHIFT
    h = h * window[:, None, None, :]
    fwd, bwd = h[:, 0], h[:, 1]
    two_sided = jnp.concatenate([fwd, jnp.zeros_like(fwd[:1]), bwd[1:][::-1]], axis=0)
    two_sided = two_sided / (jnp.sum(jnp.abs(two_sided), axis=0, keepdims=True) + EPS)
    return jnp.fft.rfft(two_sided, axis=0)


def _long_conv(u, kf, bias):
    L = u.shape[1]
    uf32 = u.astype(jnp.float32)
    uf = jnp.fft.rfft(uf32, n=2 * L, axis=1)
    y = jnp.fft.irfft(uf * kf[None], n=2 * L, axis=1)[:, :L]
    return (y + uf32 * bias.astype(jnp.float32)).astype(u.dtype)


def _hyena(z, conv_w, conv_b, w1, b1, w2, b2, w3, b3, bias):
    L = z.shape[1]
    z = _short_conv(z, conv_w, conv_b)
    x1, x2, v = jnp.split(z, 3, axis=-1)
    kf = _hyena_filters(L, w1, b1, w2, b2, w3, b3)
    u = x1 * _long_conv(v, kf[:, 0], bias[0])
    return x2 * _long_conv(u, kf[:, 1], bias[1])


def _diff_attention(zq, zk, zv, lam, lam_init, subln_g):
    B, L, _ = zq.shape
    f32 = jnp.float32
    q = zq.reshape(B, L, DA_HEADS, 2, DA_HALF)
    k = zk.reshape(B, L, DA_HEADS, 2, DA_HALF)
    v = zv.reshape(B, L, DA_HEADS, HEAD_DIM)
    slopes = jnp.exp2(-8.0 * jnp.arange(1, DA_HEADS + 1, dtype=f32) / DA_HEADS)
    nb = L // Q_BLOCK
    qb = q.reshape(B, nb, Q_BLOCK, DA_HEADS, 2, DA_HALF).transpose(1, 0, 2, 3, 4, 5)
    starts = jnp.arange(nb, dtype=jnp.int32) * Q_BLOCK
    kpos = jnp.arange(L, dtype=jnp.int32)
    scale = DA_HALF ** -0.5

    def block(args):
        qi, s0 = args
        s = jnp.einsum('bqhcd,bkhcd->bhcqk', qi, k).astype(f32) * scale
        dist = jnp.abs((s0 + jnp.arange(Q_BLOCK, dtype=jnp.int32))[:, None] - kpos[None, :]).astype(f32)
        s = s - slopes[:, None, None, None] * dist
        p = jax.nn.softmax(s, axis=-1)
        a = p[:, :, 0] - lam * p[:, :, 1]
        return jnp.einsum('bhqk,bkhd->bqhd', a.astype(v.dtype), v)

    o = lax.map(block, (qb, starts))
    o = o.transpose(1, 0, 2, 3, 4).reshape(B, L, DA_HEADS, HEAD_DIM)
    o = _rms_norm(o, subln_g) * (1.0 - lam_init)
    return o.reshape(B, L, DA_WIDTH)


def _axial_rope(x, ang_r, ang_c):
    def rot(xh, ang):
        xf = xh.astype(jnp.float32)
        x1, x2 = jnp.split(xf, 2, axis=-1)
        cos = jnp.cos(ang)[:, None, :]
        sin = jnp.sin(ang)[:, None, :]
        return jnp.concatenate([x1 * cos - x2 * sin, x2 * cos + x1 * sin], axis=-1)
    out = jnp.concatenate([rot(x[..., :ROPE_AXIS], ang_r), rot(x[..., ROPE_AXIS:], ang_c)], axis=-1)
    return out.astype(x.dtype)


def _gqa_attention(zq, zk, zv, qn_g, kn_g):
    B, L, _ = zq.shape
    f32 = jnp.float32
    q = _rms_norm(zq.reshape(B, L, GQ_HEADS, HEAD_DIM), qn_g)
    k = _rms_norm(zk.reshape(B, L, GQ_KV, HEAD_DIM), kn_g)
    v = zv.reshape(B, L, GQ_KV, HEAD_DIM)
    rows = L // GRID_W
    row = jnp.repeat(jnp.arange(rows, dtype=f32), GRID_W)
    col = jnp.tile(jnp.arange(GRID_W, dtype=f32), rows)
    inv = ROPE_THETA ** (-jnp.arange(0, ROPE_AXIS, 2, dtype=f32) / ROPE_AXIS)
    ang_r = row[:, None] * inv[None, :]
    ang_c = col[:, None] * inv[None, :]
    q = _axial_rope(q, ang_r, ang_c)
    k = _axial_rope(k, ang_r, ang_c)
    nb = L // Q_BLOCK
    qb = q.reshape(B, nb, Q_BLOCK, GQ_KV, GQ_GROUP, HEAD_DIM).transpose(1, 0, 2, 3, 4, 5)
    scale = HEAD_DIM ** -0.5

    def block(qi):
        s = jnp.einsum('bqkgd,bskd->bkgqs', qi, k).astype(f32) * scale
        p = jax.nn.softmax(s, axis=-1).astype(v.dtype)
        return jnp.einsum('bkgqs,bskd->bqkgd', p, v)

    o = lax.map(block, qb)
    return o.transpose(1, 0, 2, 3, 4, 5).reshape(B, L, GQ_WIDTH)


def _mixer(h, p, l):
    B, L, _ = h.shape
    z = h @ p['w_in'][l] + p['b_in'][l]
    za, zb, zc, zg = jnp.split(z, [HY_COLS, HY_COLS + DA_COLS, HY_COLS + DA_COLS + GQ_COLS], axis=-1)
    ya = _hyena(za, p['hy_conv_w'][l], p['hy_conv_b'][l], p['hy_f_w1'][l], p['hy_f_b1'][l],
                p['hy_f_w2'][l], p['hy_f_b2'][l], p['hy_f_w3'][l], p['hy_f_b3'][l], p['hy_bias'][l])
    lam_init = 0.8 - 0.6 * math.exp(-0.3 * l)
    f32 = jnp.float32
    lam = (jnp.exp(jnp.sum(p['da_lam_q1'][l].astype(f32) * p['da_lam_k1'][l].astype(f32)))
           - jnp.exp(jnp.sum(p['da_lam_q2'][l].astype(f32) * p['da_lam_k2'][l].astype(f32))) + lam_init)
    bq, bk, bv = jnp.split(zb, 3, axis=-1)
    yb = _diff_attention(bq, bk, bv, lam, lam_init, p['da_subln_g'][l])
    cq, ck, cv = jnp.split(zc, [GQ_WIDTH, GQ_WIDTH + GQ_KV * HEAD_DIM], axis=-1)
    yc = _gqa_attention(cq, ck, cv, p['gq_qnorm_g'][l], p['gq_knorm_g'][l])
    gates = jax.nn.sigmoid(zg.astype(f32)).astype(h.dtype).reshape(B, L, N_BRANCH, D_MODEL)
    merged = (gates[:, :, 0] * (ya @ p['w_br_a'][l])
              + gates[:, :, 1] * (yb @ p['w_br_b'][l])
              + gates[:, :, 2] * (yc @ p['w_br_c'][l]))
    return merged @ p['w_out'][l]


def _moe_swiglu(h, w_router, b_router, w_up, w_down):
    B, L, D = h.shape
    t = h.reshape(B * L, D)
    logits = (t @ w_router).astype(jnp.float32) + b_router.astype(jnp.float32)
    top_v, top_i = lax.top_k(logits, TOP_K)
    wts = jax.nn.softmax(top_v, axis=-1)
    comb = jnp.einsum('tk,tke->te', wts, jax.nn.one_hot(top_i, N_EXPERTS, dtype=jnp.float32)).astype(t.dtype)
    y = jnp.zeros_like(t)
    for e in range(N_EXPERTS):
        y = y + comb[:, e:e + 1] * _swiglu(t, w_up[e], w_down[e])
    return y.reshape(B, L, D)


def _trunk(x, c, p):
    for l in range(DEPTH):
        mod = jax.nn.silu(c) @ p['w_ada'][l] + p['b_ada'][l]
        sh1, sc1, g1, sh2, sc2, g2 = jnp.split(mod[:, None, :], 6, axis=-1)
        h = _layer_norm(x) * (1.0 + sc1) + sh1
        y = _mixer(h, p, l)
        x = _layer_norm(DEEPNORM_ALPHA * x + (1.0 + g1) * y, p['ln1_g'][l], p['ln1_b'][l])
        h = _layer_norm(x) * (1.0 + sc2) + sh2
        if l % 2 == 0:
            f = _swiglu(h, p['ffn_w_up'][l // 2], p['ffn_w_down'][l // 2])
        else:
            f = _moe_swiglu(h, p['moe_w_router'][l // 2], p['moe_b_router'][l // 2],
                            p['moe_w_up'][l // 2], p['moe_w_down'][l // 2])
        x = _layer_norm(DEEPNORM_ALPHA * x + (1.0 + g2) * f, p['ln2_g'][l], p['ln2_b'][l])
    return x


def setup_inputs(seed: int = 0) -> dict:
    key = jax.random.key(seed)
    ks = iter(jax.random.split(key, 48))
    f32 = jnp.float32

    def nrm(shape, scale):
        return jax.random.normal(next(ks), shape, f32) * scale

    D = D_MODEL
    return {
        'x_prompt': nrm((BATCH, SEQ, D), 1.0),
        'x_sample': nrm((DEC_BATCH, DEC_SEQ, D), 1.0),
        'c_prompt': nrm((BATCH, D), 1.0),
        'c_sample': nrm((DEC_BATCH, D), 1.0),
        'w_ada': nrm((DEPTH, D, 6 * D), 0.5 * D ** -0.5),
        'b_ada': nrm((DEPTH, 6 * D), 0.01),
        'w_in': nrm((DEPTH, D, IN_COLS), D ** -0.5),
        'b_in': nrm((DEPTH, IN_COLS), 0.01),
        'hy_conv_w': nrm((DEPTH, SHORT_CONV, HY_COLS), SHORT_CONV ** -0.5),
        'hy_conv_b': nrm((DEPTH, HY_COLS), 0.01),
        'hy_f_w1': nrm((DEPTH, FILTER_EMB, FILTER_HID), FILTER_EMB ** -0.5),
        'hy_f_b1': nrm((DEPTH, FILTER_HID), 0.01),
        'hy_f_w2': nrm((DEPTH, FILTER_HID, FILTER_HID), FILTER_HID ** -0.5),
        'hy_f_b2': nrm((DEPTH, FILTER_HID), 0.01),
        'hy_f_w3': nrm((DEPTH, FILTER_HID, 2 * HY_ORDER * HY_WIDTH), FILTER_HID ** -0.5),
        'hy_f_b3': nrm((DEPTH, 2 * HY_ORDER * HY_WIDTH), 0.01),
        'hy_bias': nrm((DEPTH, HY_ORDER, HY_WIDTH), 0.5),
        'da_lam_q1': nrm((DEPTH, DA_HALF), 0.1),
        'da_lam_k1': nrm((DEPTH, DA_HALF), 0.1),
        'da_lam_q2': nrm((DEPTH, DA_HALF), 0.1),
        'da_lam_k2': nrm((DEPTH, DA_HALF), 0.1),
        'da_subln_g': 1.0 + nrm((DEPTH, HEAD_DIM), 0.01),
        'gq_qnorm_g': 1.0 + nrm((DEPTH, HEAD_DIM), 0.01),
        'gq_knorm_g': 1.0 + nrm((DEPTH, HEAD_DIM), 0.01),
        'w_br_a': nrm((DEPTH, HY_WIDTH, D), HY_WIDTH ** -0.5),
        'w_br_b': nrm((DEPTH, DA_WIDTH, D), DA_WIDTH ** -0.5),
        'w_br_c': nrm((DEPTH, GQ_WIDTH, D), GQ_WIDTH ** -0.5),
        'w_out': nrm((DEPTH, D, D), DEEPNORM_BETA * D ** -0.5),
        'ln1_g': 1.0 + nrm((DEPTH, D), 0.01),
        'ln1_b': nrm((DEPTH, D), 0.01),
        'ffn_w_up': nrm((N_DENSE, D, 2 * D_FF), D ** -0.5),
        'ffn_w_down': nrm((N_DENSE, D_FF, D), DEEPNORM_BETA * D_FF ** -0.5),
        'moe_w_router': nrm((N_MOE, D, N_EXPERTS), D ** -0.5),
        'moe_b_router': nrm((N_MOE, N_EXPERTS), 0.01),
        'moe_w_up': nrm((N_MOE, N_EXPERTS, D, 2 * MOE_FF), D ** -0.5),
        'moe_w_down': nrm((N_MOE, N_EXPERTS, MOE_FF, D), DEEPNORM_BETA * MOE_FF ** -0.5),
        'ln2_g': 1.0 + nrm((DEPTH, D), 0.01),
        'ln2_b': nrm((DEPTH, D), 0.01),
    }


def reference(x_prompt, x_sample, c_prompt, c_sample, w_ada, b_ada, w_in, b_in,
              hy_conv_w, hy_conv_b, hy_f_w1, hy_f_b1, hy_f_w2, hy_f_b2, hy_f_w3, hy_f_b3, hy_bias,
              da_lam_q1, da_lam_k1, da_lam_q2, da_lam_k2, da_subln_g, gq_qnorm_g, gq_knorm_g,
              w_br_a, w_br_b, w_br_c, w_out, ln1_g, ln1_b, ffn_w_up, ffn_w_down,
              moe_w_router, moe_b_router, moe_w_up, moe_w_down, ln2_g, ln2_b):
    p = dict(w_ada=w_ada, b_ada=b_ada, w_in=w_in, b_in=b_in,
             hy_conv_w=hy_conv_w, hy_conv_b=hy_conv_b, hy_f_w1=hy_f_w1, hy_f_b1=hy_f_b1,
             hy_f_w2=hy_f_w2, hy_f_b2=hy_f_b2, hy_f_w3=hy_f_w3, hy_f_b3=hy_f_b3, hy_bias=hy_bias,
             da_lam_q1=da_lam_q1, da_lam_k1=da_lam_k1, da_lam_q2=da_lam_q2, da_lam_k2=da_lam_k2,
             da_subln_g=da_subln_g, gq_qnorm_g=gq_qnorm_g, gq_knorm_g=gq_knorm_g,
             w_br_a=w_br_a, w_br_b=w_br_b, w_br_c=w_br_c, w_out=w_out, ln1_g=ln1_g, ln1_b=ln1_b,
             ffn_w_up=ffn_w_up, ffn_w_down=ffn_w_down, moe_w_router=moe_w_router,
             moe_b_router=moe_b_router, moe_w_up=moe_w_up, moe_w_down=moe_w_down,
             ln2_g=ln2_g, ln2_b=ln2_b)
    y_prompt = _trunk(x_prompt, c_prompt, p)
    y_sample = _trunk(x_sample, c_sample, p)
    return (y_prompt, y_sample)
```

```python
import functools
import math

import jax
import jax.numpy as jnp
from jax import lax
from jax.experimental import pallas as pl
from jax.experimental.pallas import tpu as pltpu

F32 = jnp.float32
BF16 = jnp.bfloat16

D_MODEL = 1024
DEPTH = 2
GRID_W = 64
HEAD_DIM = 64
HY_WIDTH = D_MODEL // 4
HY_ORDER = 2
SHORT_CONV = 3
N_BANDS = 16
FILTER_HID = 64
DECAY_TARGET = 1e-2
MIN_DECAY = math.log(DECAY_TARGET) / 1.5
MAX_DECAY = math.log(DECAY_TARGET) / 0.3
DECAY_SHIFT = 0.05
DA_HEADS = 4
DA_HALF = HEAD_DIM // 2
DA_WIDTH = DA_HEADS * HEAD_DIM
GQ_HEADS = 8
GQ_KV = 2
GQ_WIDTH = GQ_HEADS * HEAD_DIM
ROPE_AXIS = HEAD_DIM // 2
ROPE_THETA = 10000.0
HY_COLS = (HY_ORDER + 1) * HY_WIDTH
DA_COLS = 3 * DA_WIDTH
GQ_COLS = GQ_WIDTH + 2 * GQ_KV * HEAD_DIM
GATE_COLS = 3 * D_MODEL
IN_COLS = HY_COLS + DA_COLS + GQ_COLS + GATE_COLS
D_FF = 256 * ((8 * D_MODEL // 3 + 255) // 256)
N_EXPERTS = 8
MOE_FF = D_FF // 2
DEEPNORM_ALPHA = (2.0 * DEPTH) ** 0.25
EPS = 1e-5

LANES = 128
VMEM_LIMIT = 56 * 1024 * 1024

COL_GATE = 0
COL_HY = GATE_COLS
COL_DA = COL_HY + HY_COLS
COL_GQ = COL_DA + DA_COLS


def _params(sem):
    return pltpu.CompilerParams(dimension_semantics=sem, vmem_limit_bytes=VMEM_LIMIT)


def _dot(a, b):
    return jnp.dot(a, b, preferred_element_type=F32)


def _dot_nt(a, b):
    return lax.dot_general(a, b, (((1,), (1,)), ((), ())), preferred_element_type=F32)


def _split(x):
    hi = x.astype(BF16)
    lo = (x - hi.astype(F32)).astype(BF16)
    return hi, lo


def _dot3(a, b):
    ah, al = _split(a)
    bh, bl = _split(b)
    return _dot(ah, bh) + (_dot(ah, bl) + _dot(al, bh))


def _sigmoid(x):
    return 1.0 / (1.0 + jnp.exp(-x))


def _ln(x):
    mu = jnp.mean(x, axis=-1, keepdims=True)
    xc = x - mu
    var = jnp.mean(xc * xc, axis=-1, keepdims=True)
    return xc * lax.rsqrt(var + EPS)


def _head_ssq(x):
    w = x.shape[1]
    r = lax.broadcasted_iota(jnp.int32, (w, w), 0) // HEAD_DIM
    c = lax.broadcasted_iota(jnp.int32, (w, w), 1) // HEAD_DIM
    ones = jnp.where(r == c, 1.0, 0.0).astype(BF16)
    hi, lo = _split(x * x)
    return _dot(hi, ones) + _dot(lo, ones)


def _ada_kernel(c_ref, w_ref, b_ref, o_ref):
    c = c_ref[...]
    o_ref[...] = _dot3(c * _sigmoid(c), w_ref[...]) + b_ref[...]


def _ada(c, w_ada, b_ada):
    nb = c.shape[0]
    tn = 1536
    return pl.pallas_call(
        _ada_kernel,
        grid=(DEPTH, 6 * D_MODEL // tn),
        in_specs=[
            pl.BlockSpec((nb, D_MODEL), lambda l, j: (0, 0)),
            pl.BlockSpec((None, D_MODEL, tn), lambda l, j: (l, 0, j)),
            pl.BlockSpec((None, 1, tn), lambda l, j: (l, 0, j)),
        ],
        out_specs=pl.BlockSpec((None, nb, tn), lambda l, j: (l, 0, j)),
        out_shape=jax.ShapeDtypeStruct((DEPTH, nb, 6 * D_MODEL), F32),
        compiler_params=_params(("arbitrary", "arbitrary")),
        name="ada_mod",
    )(c, w_ada, b_ada.reshape(DEPTH, 1, 6 * D_MODEL))


LN_ROWS = 256


def _inproj_kernel(x_ref, mod_ref, w_ref, b_ref, o_ref, h_ref):
    rows = x_ref.shape[1]

    @pl.when(pl.program_id(1) == 0)
    def _():
        shift = mod_ref[0, 0:1, :]
        scale = 1.0 + mod_ref[0, 1:2, :]

        def body(i, carry):
            r = pl.ds(pl.multiple_of(i * LN_ROWS, LN_ROWS), LN_ROWS)
            h_ref[r, :] = (_ln(x_ref[0, r, :]) * scale + shift).astype(BF16)
            return carry

        lax.fori_loop(0, rows // LN_ROWS, body, 0)

    o_ref[0] = (_dot(h_ref[...], w_ref[...]) + b_ref[...]).astype(BF16)


def _inproj(x, mod, w, b):
    nb, seq, _ = x.shape
    tn = 768
    return pl.pallas_call(
        _inproj_kernel,
        grid=(nb, IN_COLS // tn),
        in_specs=[
            pl.BlockSpec((1, seq, D_MODEL), lambda i, j: (i, 0, 0)),
            pl.BlockSpec((1, 6, D_MODEL), lambda i, j: (i, 0, 0)),
            pl.BlockSpec((D_MODEL, tn), lambda i, j: (0, j)),
            pl.BlockSpec((1, tn), lambda i, j: (0, j)),
        ],
        out_specs=pl.BlockSpec((1, seq, tn), lambda i, j: (i, 0, j)),
        out_shape=jax.ShapeDtypeStruct((nb, seq, IN_COLS), BF16),
        scratch_shapes=[pltpu.VMEM((seq, D_MODEL), BF16)],
        compiler_params=_params(("arbitrary", "arbitrary")),
        name="ln_inproj",
    )(x, mod, w, b)


def _shortconv_kernel(z_ref, w_ref, b_ref, o_ref):
    z = z_ref[0].astype(F32)
    seq = z.shape[0]
    row = lax.broadcasted_iota(jnp.int32, z.shape, 0)
    prev = jnp.where(row == 0, 0.0, pltpu.roll(z, 1, 0))
    nxt = jnp.where(row == seq - 1, 0.0, pltpu.roll(z, seq - 1, 0))
    o = b_ref[...] + prev * w_ref[0:1, :] + z * w_ref[1:2, :] + nxt * w_ref[2:3, :]
    o_ref[0] = o.astype(BF16)


def _shortconv(z, w, b):
    nb, seq, _ = z.shape
    cb = COL_HY // HY_WIDTH
    return pl.pallas_call(
        _shortconv_kernel,
        grid=(nb, HY_COLS // HY_WIDTH),
        in_specs=[
            pl.BlockSpec((1, seq, HY_WIDTH), lambda i, j: (i, 0, cb + j)),
            pl.BlockSpec((SHORT_CONV, HY_WIDTH), lambda i, j: (0, j)),
            pl.BlockSpec((1, HY_WIDTH), lambda i, j: (0, j)),
        ],
        out_specs=pl.BlockSpec((1, seq, HY_WIDTH), lambda i, j: (i, 0, j)),
        out_shape=jax.ShapeDtypeStruct((nb, seq, HY_COLS), BF16),
        compiler_params=_params(("arbitrary", "arbitrary")),
        name="hy_shortconv",
    )(z, w, b)


FILT_ROWS = 512


def _filter_kernel(bands_ref, w1_ref, b1_ref, w2_ref, b2_ref, w3_ref, b3_ref, dec_ref, o_ref, ts_ref, *, seq):
    n = 2 * seq
    rows = min(FILT_ROWS, n)
    hw = HY_ORDER * HY_WIDTH
    lane = lax.broadcasted_iota(jnp.int32, (1, LANES), 1)

    def fill(i, asum):
        j0 = pl.multiple_of(i * rows, rows)
        j = j0 + lax.broadcasted_iota(jnp.int32, (rows, 1), 0)
        t = jnp.where(j < seq, j, n - j).astype(F32)
        t_norm = t / max(seq - 1, 1)
        ang = (2.0 * math.pi / seq) * t * bands_ref[...]
        feats = jnp.where(lane == 0, t_norm,
                          jnp.where(lane <= N_BANDS, jnp.cos(ang),
                                    jnp.where(lane <= 2 * N_BANDS, -jnp.sin(ang), 0.0)))
        h = jnp.sin(_dot3(feats, w1_ref[...]) + b1_ref[...])
        h = jnp.sin(_dot3(h, w2_ref[...]) + b2_ref[...])
        h = _dot3(h, w3_ref[...]) + b3_ref[...]
        window = jnp.exp(-t_norm * dec_ref[...]) + DECAY_SHIFT
        sel = jnp.where(j < seq, h[:, :hw], h[:, hw:]) * window
        sel = jnp.where(j == seq, 0.0, sel)
        ts_ref[pl.ds(j0, rows), :] = sel
        return asum + jnp.sum(jnp.abs(sel), axis=0, keepdims=True)

    asum = lax.fori_loop(0, n // rows, fill, jnp.zeros((1, hw), F32))
    inv = 1.0 / (asum + EPS)

    def norm(i, carry):
        r = pl.ds(pl.multiple_of(i * rows, rows), rows)
        o_ref[r, :] = (ts_ref[r, :] * inv).astype(BF16)
        return carry

    lax.fori_loop(0, n // rows, norm, 0)


def _filters_time(seq, bands, w1, b1, w2, b2, w3, b3, dec):
    hw = HY_ORDER * HY_WIDTH
    return pl.pallas_call(
        functools.partial(_filter_kernel, seq=seq),
        out_shape=jax.ShapeDtypeStruct((2 * seq, hw), BF16),
        scratch_shapes=[pltpu.VMEM((2 * seq, hw), F32)],
        compiler_params=pltpu.CompilerParams(vmem_limit_bytes=VMEM_LIMIT),
        name="hy_filter_time",
    )(bands, w1, b1, w2, b2, w3, b3, dec)


def _matmul_kernel(a_ref, b_ref, o_ref):
    o_ref[...] = _dot(a_ref[...], b_ref[...])


def _filter_dft(wf, ts):
    n, hw = ts.shape
    tm = min(512, n)
    return pl.pallas_call(
        _matmul_kernel,
        grid=(n // tm,),
        in_specs=[pl.BlockSpec((tm, n), lambda i: (i, 0)), pl.BlockSpec((n, hw), lambda i: (0, 0))],
        out_specs=pl.BlockSpec((tm, hw), lambda i: (i, 0)),
        out_shape=jax.ShapeDtypeStruct((n, hw), F32),
        compiler_params=_params(("arbitrary",)),
        name="hy_filter_dft",
    )(wf, ts)


def _longconv_kernel(v_ref, g_ref, w_ref, wi_ref, kf_ref, bias_ref, o_ref, acc_ref):
    kt = pl.program_id(1)
    nb = v_ref.shape[0]
    tf = w_ref.shape[1]

    @pl.when(kt == 0)
    def _():
        acc_ref[...] = jnp.zeros_like(acc_ref)

    kr = kf_ref[0]
    ki = kf_ref[1]
    first = (lax.broadcasted_iota(jnp.int32, (tf, 1), 0) == 0) & (kt == 0)
    for n in range(nb):
        v = v_ref[n]
        ur = _dot(w_ref[0], v)
        ui = _dot(w_ref[1], v)
        gr = ur * kr - jnp.where(first, 0.0, ui * ki)
        gi = jnp.where(first, ui * ki, ur * ki + ui * kr)
        acc_ref[n] += _dot(wi_ref[0], gr.astype(BF16)) + _dot(wi_ref[1], gi.astype(BF16))

    @pl.when(kt == pl.num_programs(1) - 1)
    def _():
        for n in range(nb):
            vf = v_ref[n].astype(F32)
            y = (acc_ref[n] + vf * bias_ref[...]).astype(BF16)
            o_ref[n] = (g_ref[n] * y).astype(BF16)


def _longconv(v_arr, v_col, g_arr, g_col, w, wi, kf, order, bias):
    nbatch, seq, _ = v_arr.shape
    nb = math.gcd(nbatch, 4)
    tf = min(256, seq)
    c = HY_WIDTH
    return pl.pallas_call(
        _longconv_kernel,
        grid=(nbatch // nb, seq // tf),
        in_specs=[
            pl.BlockSpec((nb, seq, c), lambda i, k: (i, 0, v_col)),
            pl.BlockSpec((nb, seq, c), lambda i, k: (i, 0, g_col)),
            pl.BlockSpec((2, tf, seq), lambda i, k: (0, k, 0)),
            pl.BlockSpec((2, seq, tf), lambda i, k: (0, 0, k)),
            pl.BlockSpec((2, tf, c), lambda i, k: (0, k, order)),
            pl.BlockSpec((1, c), lambda i, k: (0, 0)),
        ],
        out_specs=pl.BlockSpec((nb, seq, c), lambda i, k: (i, 0, 0)),
        out_shape=jax.ShapeDtypeStruct((nbatch, seq, c), BF16),
        scratch_shapes=[pltpu.VMEM((nb, seq, c), F32)],
        compiler_params=_params(("arbitrary", "arbitrary")),
        name="hy_longconv",
    )(v_arr, g_arr, w, wi, kf, bias)


def _lane_mask(lo, width, n=LANES):
    lane = lax.broadcasted_iota(jnp.int32, (1, n), 1)
    return (lane >= lo) & (lane < lo + width)


def _da_kernel(q_ref, k_ref, v_ref, lam_ref, g_ref, o_ref, *, lam_init):
    tq = q_ref.shape[1]
    seq = k_ref.shape[1]
    qi = pl.program_id(1)
    lv = lam_ref[...]
    lam = (jnp.exp(jnp.sum(lv[0:1] * lv[1:2], axis=1, keepdims=True))
           - jnp.exp(jnp.sum(lv[2:3] * lv[3:4], axis=1, keepdims=True)) + lam_init)
    qpos = qi * tq + lax.broadcasted_iota(jnp.int32, (tq, seq), 0)
    kpos = lax.broadcasted_iota(jnp.int32, (tq, seq), 1)
    dist = jnp.abs(qpos - kpos).astype(F32)
    slabs = []
    for s in range(DA_WIDTH // LANES):
        cols = slice(s * LANES, (s + 1) * LANES)
        q_slab = q_ref[0, :, cols]
        k_slab = k_ref[0, :, cols]
        v_slab = v_ref[0, :, cols]
        o_slab = jnp.zeros((tq, LANES), F32)
        for hh in range(LANES // HEAD_DIM):
            head = s * (LANES // HEAD_DIM) + hh
            slope = 2.0 ** (-8.0 * (head + 1) / DA_HEADS)
            probs = []
            for c in range(2):
                keep = jnp.where(_lane_mask(hh * HEAD_DIM + c * DA_HALF, DA_HALF), 1.0, 0.0).astype(BF16)
                sc = _dot_nt(q_slab * keep, k_slab) - slope * dist
                p = jnp.exp(sc - jnp.max(sc, axis=1, keepdims=True))
                probs.append((p, jnp.sum(p, axis=1, keepdims=True)))
            a = probs[0][0] * (1.0 / probs[0][1]) - probs[1][0] * (lam / probs[1][1])
            o2 = _dot(a.astype(BF16), v_slab)
            o_slab = jnp.where(_lane_mask(hh * HEAD_DIM, HEAD_DIM), o2, o_slab)
        slabs.append(o_slab)
    o = jnp.concatenate(slabs, axis=1)
    y = o * lax.rsqrt(_head_ssq(o) * (1.0 / HEAD_DIM) + EPS) * g_ref[...]
    o_ref[0] = (y * (1.0 - lam_init)).astype(BF16)


def _diff_attn(z, lam_vecs, g_tiled, lam_init):
    nb, seq, _ = z.shape
    tq = min(256, seq)
    cb = COL_DA // DA_WIDTH
    return pl.pallas_call(
        functools.partial(_da_kernel, lam_init=lam_init),
        grid=(nb, seq // tq),
        in_specs=[
            pl.BlockSpec((1, tq, DA_WIDTH), lambda b, i: (b, i, cb)),
            pl.BlockSpec((1, seq, DA_WIDTH), lambda b, i: (b, 0, cb + 1)),
            pl.BlockSpec((1, seq, DA_WIDTH), lambda b, i: (b, 0, cb + 2)),
            pl.BlockSpec((4, DA_HALF), lambda b, i: (0, 0)),
            pl.BlockSpec((1, DA_WIDTH), lambda b, i: (0, 0)),
        ],
        out_specs=pl.BlockSpec((1, tq, DA_WIDTH), lambda b, i: (b, i, 0)),
        out_shape=jax.ShapeDtypeStruct((nb, seq, DA_WIDTH), BF16),
        compiler_params=_params(("arbitrary", "arbitrary")),
        name="diff_attn",
    )(z, z, z, lam_vecs, g_tiled)


def _rope(x, cos, sin):
    w = x.shape[1]
    lane = lax.broadcasted_iota(jnp.int32, (1, w), 1)
    half = ROPE_AXIS // 2
    partner = jnp.where(lane % ROPE_AXIS < half, pltpu.roll(x, w - half, 1), pltpu.roll(x, half, 1))
    return x * cos + partner * sin


def _dup_heads(x):
    first = _lane_mask(0, HEAD_DIM)
    a = jnp.where(first, x, 0.0)
    b = jnp.where(first, 0.0, x)
    return jnp.concatenate([a + pltpu.roll(a, HEAD_DIM, 1), b + pltpu.roll(b, HEAD_DIM, 1)], axis=1)


def _gqa_prep_kernel(q_ref, kv_ref, cos_ref, sin_ref, gq_ref, gk_ref, qo_ref, ko_ref, vo_ref):
    kvw = GQ_KV * HEAD_DIM
    q = q_ref[0].astype(F32)
    qn = q * lax.rsqrt(_head_ssq(q) * (1.0 / HEAD_DIM) + EPS) * gq_ref[...]
    qo_ref[0] = (_rope(qn, cos_ref[...], sin_ref[...]) * HEAD_DIM ** -0.5).astype(BF16)
    k = kv_ref[0, :, :kvw].astype(F32)
    kn = k * lax.rsqrt(_head_ssq(k) * (1.0 / HEAD_DIM) + EPS) * gk_ref[...]
    kr = _rope(kn, cos_ref[:, :kvw], sin_ref[:, :kvw])
    ko_ref[0] = _dup_heads(kr).astype(BF16)
    vo_ref[0] = _dup_heads(kv_ref[0, :, kvw:].astype(F32)).astype(BF16)


def _gqa_prep(z, cos, sin, gq, gk):
    nb, seq, _ = z.shape
    tl = min(512, seq)
    kvw = GQ_KV * HEAD_DIM
    return pl.pallas_call(
        _gqa_prep_kernel,
        grid=(nb, seq // tl),
        in_specs=[
            pl.BlockSpec((1, tl, GQ_WIDTH), lambda b, i: (b, i, COL_GQ // GQ_WIDTH)),
            pl.BlockSpec((1, tl, 2 * kvw), lambda b, i: (b, i, (COL_GQ + GQ_WIDTH) // (2 * kvw))),
            pl.BlockSpec((tl, GQ_WIDTH), lambda b, i: (i, 0)),
            pl.BlockSpec((tl, GQ_WIDTH), lambda b, i: (i, 0)),
            pl.BlockSpec((1, GQ_WIDTH), lambda b, i: (0, 0)),
            pl.BlockSpec((1, kvw), lambda b, i: (0, 0)),
        ],
        out_specs=[
            pl.BlockSpec((1, tl, GQ_WIDTH), lambda b, i: (b, i, 0)),
            pl.BlockSpec((1, tl, 2 * kvw), lambda b, i: (b, i, 0)),
            pl.BlockSpec((1, tl, 2 * kvw), lambda b, i: (b, i, 0)),
        ],
        out_shape=[
            jax.ShapeDtypeStruct((nb, seq, GQ_WIDTH), BF16),
            jax.ShapeDtypeStruct((nb, seq, 2 * kvw), BF16),
            jax.ShapeDtypeStruct((nb, seq, 2 * kvw), BF16),
        ],
        compiler_params=_params(("arbitrary", "arbitrary")),
        name="gqa_prep",
    )(z, z, cos, sin, gq, gk)


def _gqa_kernel(q_ref, k_ref, v_ref, o_ref):
    tq = q_ref.shape[1]
    per_slab = LANES // HEAD_DIM
    group = GQ_HEADS // GQ_KV
    slabs = []
    for s in range(GQ_WIDTH // LANES):
        q_slab = q_ref[0, :, s * LANES:(s + 1) * LANES]
        kv = (s * per_slab) // group
        k_dup = k_ref[0, :, kv * LANES:(kv + 1) * LANES]
        v_dup = v_ref[0, :, kv * LANES:(kv + 1) * LANES]
        o_slab = jnp.zeros((tq, LANES), F32)
        for hh in range(per_slab):
            keep = jnp.where(_lane_mask(hh * HEAD_DIM, HEAD_DIM), 1.0, 0.0).astype(BF16)
            sc = _dot_nt(q_slab * keep, k_dup)
            p = jnp.exp(sc - jnp.max(sc, axis=1, keepdims=True))
            inv = 1.0 / jnp.sum(p, axis=1, keepdims=True)
            o2 = _dot(p.astype(BF16), v_dup) * inv
            o_slab = jnp.where(_lane_mask(hh * HEAD_DIM, HEAD_DIM), o2, o_slab)
        slabs.append(o_slab)
    o_ref[0] = jnp.concatenate(slabs, axis=1).astype(BF16)


def _gqa_attn(q, k, v):
    nb, seq, _ = q.shape
    tq = min(256, seq)
    kvw = 2 * GQ_KV * HEAD_DIM
    return pl.pallas_call(
        _gqa_kernel,
        grid=(nb, seq // tq),
        in_specs=[
            pl.BlockSpec((1, tq, GQ_WIDTH), lambda b, i: (b, i, 0)),
            pl.BlockSpec((1, seq, kvw), lambda b, i: (b, 0, 0)),
            pl.BlockSpec((1, seq, kvw), lambda b, i: (b, 0, 0)),
        ],
        out_specs=pl.BlockSpec((1, tq, GQ_WIDTH), lambda b, i: (b, i, 0)),
        out_shape=jax.ShapeDtypeStruct((nb, seq, GQ_WIDTH), BF16),
        compiler_params=_params(("arbitrary", "arbitrary")),
        name="gqa_attn",
    )(q, k, v)


def _merge_kernel(ya_ref, yb_ref, yc_ref, zg_ref, x_ref, mod_ref, wa_ref, wb_ref, wc_ref, wo_ref,
                  lg_ref, lb_ref, wr_ref, br_ref, xo_ref, h_ref, comb_ref):
    d = D_MODEL
    m = _sigmoid(zg_ref[0, :, 0:d].astype(F32)) * _dot(ya_ref[0], wa_ref[...])
    m = m + _sigmoid(zg_ref[0, :, d:2 * d].astype(F32)) * _dot(yb_ref[0], wb_ref[...])
    m = m + _sigmoid(zg_ref[0, :, 2 * d:3 * d].astype(F32)) * _dot(yc_ref[0], wc_ref[...])
    y = _dot(m.astype(BF16), wo_ref[...])
    gate1 = 1.0 + mod_ref[0, 2:3, :]
    xn = _ln(DEEPNORM_ALPHA * x_ref[0] + gate1 * y) * lg_ref[...] + lb_ref[...]
    xo_ref[0] = xn
    h = _ln(xn) * (1.0 + mod_ref[0, 4:5, :]) + mod_ref[0, 3:4, :]
    h_ref[0] = h.astype(BF16)
    lane = lax.broadcasted_iota(jnp.int32, (1, LANES), 1).astype(F32)
    neg = -jnp.inf
    logits = jnp.where(lane < N_EXPERTS, _dot3(h, wr_ref[...]) + br_ref[...], neg)
    v1 = jnp.max(logits, axis=1, keepdims=True)
    i1 = jnp.min(jnp.where(logits == v1, lane, float(LANES)), axis=1, keepdims=True)
    rest = jnp.where(lane == i1, neg, logits)
    v2 = jnp.max(rest, axis=1, keepdims=True)
    i2 = jnp.min(jnp.where(rest == v2, lane, float(LANES)), axis=1, keepdims=True)
    e = jnp.exp(v2 - v1)
    w1 = 1.0 / (1.0 + e)
    comb_ref[0] = jnp.where(lane == i1, w1, 0.0) + jnp.where(lane == i2, e * w1, 0.0)


def _merge(ya, yb, yc, z, x, mod, wa, wb, wc, wo, lg, lb, wr, br):
    nb, seq, d = x.shape
    tm = min(512, seq)
    full = lambda shape: pl.BlockSpec(shape, lambda b, i: (0,) * len(shape))
    row = lambda w, col=0: pl.BlockSpec((1, tm, w), lambda b, i: (b, i, col))
    return pl.pallas_call(
        _merge_kernel,
        grid=(nb, seq // tm),
        in_specs=[
            row(HY_WIDTH), row(DA_WIDTH), row(GQ_WIDTH), row(GATE_COLS, COL_GATE // GATE_COLS), row(d),
            pl.BlockSpec((1, 6, d), lambda b, i: (b, 0, 0)),
            full((HY_WIDTH, d)), full((DA_WIDTH, d)), full((GQ_WIDTH, d)), full((d, d)),
            full((1, d)), full((1, d)), full((d, LANES)), full((1, LANES)),
        ],
        out_specs=[row(d), row(d), row(LANES)],
        out_shape=[
            jax.ShapeDtypeStruct((nb, seq, d), F32),
            jax.ShapeDtypeStruct((nb, seq, d), BF16),
            jax.ShapeDtypeStruct((nb, seq, LANES), F32),
        ],
        compiler_params=_params(("arbitrary", "arbitrary")),
        name="merge_outproj",
    )(ya, yb, yc, z, x, mod, wa, wb, wc, wo, lg, lb, wr, br)


def _ffn_kernel(h_ref, comb_ref, wg_ref, wu_ref, wd_ref, x_ref, mod_ref, lg_ref, lb_ref, o_ref, acc_ref, *, routed):
    e = pl.program_id(2)
    j = pl.program_id(3)

    @pl.when((e == 0) & (j == 0))
    def _():
        acc_ref[...] = jnp.zeros_like(acc_ref)

    h = h_ref[0]
    g = _dot(h, wg_ref[...])
    u = _dot(h, wu_ref[...])
    a = g * _sigmoid(g) * u
    if routed:
        lane = lax.broadcasted_iota(jnp.int32, (1, LANES), 1)
        a = a * jnp.sum(jnp.where(lane == e, comb_ref[0], 0.0), axis=1, keepdims=True)
    acc_ref[...] += _dot(a.astype(BF16), wd_ref[...])

    @pl.when((e == pl.num_programs(2) - 1) & (j == pl.num_programs(3) - 1))
    def _():
        gate2 = 1.0 + mod_ref[0, 5:6, :]
        o_ref[0] = _ln(DEEPNORM_ALPHA * x_ref[0] + gate2 * acc_ref[...]) * lg_ref[...] + lb_ref[...]


def _ffn(h, comb, w_up, w_down, x, mod, lg, lb, *, routed, tm, tf):
    nb, seq, d = x.shape
    ne, _, ff2 = w_up.shape
    ff = ff2 // 2
    tm = min(tm, seq)
    nf = ff // tf
    row = lambda w: pl.BlockSpec((1, tm, w), lambda b, i, e, j: (b, i, 0))
    vec = pl.BlockSpec((1, d), lambda b, i, e, j: (0, 0))
    return pl.pallas_call(
        functools.partial(_ffn_kernel, routed=routed),
        grid=(nb, seq // tm, ne, nf),
        in_specs=[
            row(d), row(LANES),
            pl.BlockSpec((None, d, tf), lambda b, i, e, j: (e, 0, j)),
            pl.BlockSpec((None, d, tf), lambda b, i, e, j: (e, 0, nf + j)),
            pl.BlockSpec((None, tf, d), lambda b, i, e, j: (e, j, 0)),
            row(d),
            pl.BlockSpec((1, 6, d), lambda b, i, e, j: (b, 0, 0)),
            vec, vec,
        ],
        out_specs=row(d),
        out_shape=jax.ShapeDtypeStruct((nb, seq, d), F32),
        scratch_shapes=[pltpu.VMEM((tm, d), F32)],
        compiler_params=_params(("arbitrary",) * 4),
        name="swiglu_routed" if routed else "swiglu_dense",
    )(h, comb, w_up, w_up, w_down, x, mod, lg, lb)


def _dft_tables(seq):
    n = 2 * seq
    k = jnp.arange(seq, dtype=jnp.int32)[:, None]
    s = jnp.arange(n, dtype=jnp.int32)[None, :]
    ang = ((k * s) % n).astype(F32) * (2.0 * math.pi / n)
    cos = jnp.cos(ang)
    msin = -jnp.sin(ang)
    alt = jnp.where(s % 2 == 0, 1.0, -1.0).astype(F32)
    msin = jnp.where(k == 0, alt, msin)
    fwd = jnp.stack([cos, msin])
    scale = jnp.where(k == 0, 1.0 / n, 2.0 / n).astype(F32)
    inv = jnp.stack([cos[:, :seq] * scale, msin[:, :seq] * scale]).transpose(0, 2, 1)
    return fwd.astype(BF16), inv.astype(BF16)


def _rope_tables(seq):
    t = jnp.arange(seq, dtype=jnp.int32)
    inv = ROPE_THETA ** (-jnp.arange(0, ROPE_AXIS, 2, dtype=F32) / ROPE_AXIS)
    ang_r = (t // GRID_W).astype(F32)[:, None] * inv[None, :]
    ang_c = (t % GRID_W).astype(F32)[:, None] * inv[None, :]
    cos = jnp.concatenate([jnp.cos(ang_r)] * 2 + [jnp.cos(ang_c)] * 2, axis=1)
    sin = jnp.concatenate([-jnp.sin(ang_r), jnp.sin(ang_r), -jnp.sin(ang_c), jnp.sin(ang_c)], axis=1)
    return jnp.tile(cos, (1, GQ_HEADS)), jnp.tile(sin, (1, GQ_HEADS))


def _pad_to(a, rows, cols):
    return jnp.pad(a, ((0, rows - a.shape[0]), (0, cols - a.shape[1])))


def _prepare(seq, p):
    fwd, inv = _dft_tables(seq)
    cos, sin = _rope_tables(seq)
    bands = jnp.linspace(1e-4, N_BANDS - 1, N_BANDS, dtype=F32)
    bands_row = jnp.zeros((1, LANES), F32).at[0, 1:1 + N_BANDS].set(bands).at[0, 1 + N_BANDS:1 + 2 * N_BANDS].set(bands)
    deltas = jnp.abs(jnp.linspace(MIN_DECAY, MAX_DECAY, HY_WIDTH, dtype=F32))
    dec = jnp.tile(deltas, HY_ORDER)[None, :]
    da_scale = DA_HALF ** -0.5
    layers = []
    for l in range(DEPTH):
        w_in, b_in = p['w_in'][l], p['b_in'][l]
        a0, b0, c0, g0 = 0, HY_COLS, HY_COLS + DA_COLS, HY_COLS + DA_COLS + GQ_COLS
        w_cols = [w_in[:, g0:], w_in[:, a0:b0], w_in[:, b0:b0 + DA_WIDTH] * da_scale, w_in[:, b0 + DA_WIDTH:c0], w_in[:, c0:g0]]
        b_cols = [b_in[g0:], b_in[a0:b0], b_in[b0:b0 + DA_WIDTH] * da_scale, b_in[b0 + DA_WIDTH:c0], b_in[c0:g0]]
        ts = _filters_time(
            seq, bands_row,
            _pad_to(p['hy_f_w1'][l], LANES, LANES), _pad_to(p['hy_f_b1'][l][None, :], 1, LANES),
            _pad_to(p['hy_f_w2'][l], LANES, LANES), _pad_to(p['hy_f_b2'][l][None, :], 1, LANES),
            _pad_to(p['hy_f_w3'][l], LANES, 2 * HY_ORDER * HY_WIDTH), p['hy_f_b3'][l][None, :], dec)
        kf = _filter_dft(fwd.reshape(2 * seq, 2 * seq), ts).reshape(2, seq, HY_ORDER * HY_WIDTH)
        lay = dict(
            w_in=jnp.concatenate(w_cols, axis=1).astype(BF16),
            b_in=jnp.concatenate(b_cols)[None, :],
            conv_w=p['hy_conv_w'][l], conv_b=p['hy_conv_b'][l][None, :],
            kf=kf, hy_bias=p['hy_bias'][l],
            lam=jnp.stack([p['da_lam_q1'][l], p['da_lam_k1'][l], p['da_lam_q2'][l], p['da_lam_k2'][l]]),
            lam_init=0.8 - 0.6 * math.exp(-0.3 * l),
            subln=jnp.tile(p['da_subln_g'][l], DA_HEADS)[None, :],
            gq=jnp.tile(p['gq_qnorm_g'][l], GQ_HEADS)[None, :],
            gk=jnp.tile(p['gq_knorm_g'][l], GQ_KV)[None, :],
            wa=p['w_br_a'][l].astype(BF16), wb=p['w_br_b'][l].astype(BF16), wc=p['w_br_c'][l].astype(BF16),
            wo=p['w_out'][l].astype(BF16),
            ln1_g=p['ln1_g'][l][None, :], ln1_b=p['ln1_b'][l][None, :],
            ln2_g=p['ln2_g'][l][None, :], ln2_b=p['ln2_b'][l][None, :],
        )
        if l % 2 == 0:
            lay.update(w_up=p['ffn_w_up'][l // 2][None].astype(BF16), w_down=p['ffn_w_down'][l // 2][None].astype(BF16),
                       wr=jnp.zeros((D_MODEL, LANES), F32), br=jnp.zeros((1, LANES), F32))
        else:
            lay.update(w_up=p['moe_w_up'][l // 2].astype(BF16), w_down=p['moe_w_down'][l // 2].astype(BF16),
                       wr=_pad_to(p['moe_w_router'][l // 2], D_MODEL, LANES),
                       br=_pad_to(p['moe_b_router'][l // 2][None, :], 1, LANES))
        layers.append(lay)
    return dict(fwd=fwd[:, :, :seq], inv=inv, cos=cos, sin=sin, layers=layers)


def _trunk(x, c, p, prep):
    mods = _ada(c, p['w_ada'], p['b_ada'])
    nb = x.shape[0]
    for l, lay in enumerate(prep['layers']):
        mod = mods[l].reshape(nb, 6, D_MODEL)
        z = _inproj(x, mod, lay['w_in'], lay['b_in'])
        zc = _shortconv(z, lay['conv_w'], lay['conv_b'])
        u = _longconv(zc, 2, zc, 0, prep['fwd'], prep['inv'], lay['kf'], 0, lay['hy_bias'][0:1])
        ya = _longconv(u, 0, zc, 1, prep['fwd'], prep['inv'], lay['kf'], 1, lay['hy_bias'][1:2])
        yb = _diff_attn(z, lay['lam'], lay['subln'], lay['lam_init'])
        q, kd, vd = _gqa_prep(z, prep['cos'], prep['sin'], lay['gq'], lay['gk'])
        yc = _gqa_attn(q, kd, vd)
        x, h, comb = _merge(ya, yb, yc, z, x, mod, lay['wa'], lay['wb'], lay['wc'], lay['wo'],
                            lay['ln1_g'], lay['ln1_b'], lay['wr'], lay['br'])
        if l % 2 == 0:
            x = _ffn(h, comb, lay['w_up'], lay['w_down'], x, mod, lay['ln2_g'], lay['ln2_b'],
                     routed=False, tm=1024, tf=256)
        else:
            x = _ffn(h, comb, lay['w_up'], lay['w_down'], x, mod, lay['ln2_g'], lay['ln2_b'],
                     routed=True, tm=512, tf=MOE_FF)
    return x


def kernel(x_prompt, x_sample, c_prompt, c_sample, w_ada, b_ada, w_in, b_in, hy_conv_w, hy_conv_b, hy_f_w1, hy_f_b1, hy_f_w2, hy_f_b2, hy_f_w3, hy_f_b3, hy_bias, da_lam_q1, da_lam_k1, da_lam_q2, da_lam_k2, da_subln_g, gq_qnorm_g, gq_knorm_g, w_br_a, w_br_b, w_br_c, w_out, ln1_g, ln1_b, ffn_w_up, ffn_w_down, moe_w_router, moe_b_router, moe_w_up, moe_w_down, ln2_g, ln2_b):
    p = dict(w_ada=w_ada, b_ada=b_ada, w_in=w_in, b_in=b_in,
             hy_conv_w=hy_conv_w, hy_conv_b=hy_conv_b, hy_f_w1=hy_f_w1, hy_f_b1=hy_f_b1,
             hy_f_w2=hy_f_w2, hy_f_b2=hy_f_b2, hy_f_w3=hy_f_w3, hy_f_b3=hy_f_b3, hy_bias=hy_bias,
             da_lam_q1=da_lam_q1, da_lam_k1=da_lam_k1, da_lam_q2=da_lam_q2, da_lam_k2=da_lam_k2,
             da_subln_g=da_subln_g, gq_qnorm_g=gq_qnorm_g, gq_knorm_g=gq_knorm_g,
             w_br_a=w_br_a, w_br_b=w_br_b, w_br_c=w_br_c, w_out=w_out, ln1_g=ln1_g, ln1_b=ln1_b,
             ffn_w_up=ffn_w_up, ffn_w_down=ffn_w_down, moe_w_router=moe_w_router,
             moe_b_router=moe_b_router, moe_w_up=moe_w_up, moe_w_down=moe_w_down,
             ln2_g=ln2_g, ln2_b=ln2_b)
    assert x_prompt.shape[1] == x_sample.shape[1]
    prep = _prepare(x_prompt.shape[1], p)
    return (_trunk(x_prompt, c_prompt, p, prep), _trunk(x_sample, c_sample, p, prep))
```

```python
import functools
import math

import jax
import jax.numpy as jnp
from jax import lax
from jax.experimental import pallas as pl
from jax.experimental.pallas import tpu as pltpu

F32 = jnp.float32
BF16 = jnp.bfloat16

D_MODEL = 1024
DEPTH = 2
GRID_W = 64
HEAD_DIM = 64
HY_WIDTH = D_MODEL // 4
HY_ORDER = 2
SHORT_CONV = 3
N_BANDS = 16
FILTER_HID = 64
DECAY_TARGET = 1e-2
MIN_DECAY = math.log(DECAY_TARGET) / 1.5
MAX_DECAY = math.log(DECAY_TARGET) / 0.3
DECAY_SHIFT = 0.05
DA_HEADS = 4
DA_HALF = HEAD_DIM // 2
DA_WIDTH = DA_HEADS * HEAD_DIM
GQ_HEADS = 8
GQ_KV = 2
GQ_WIDTH = GQ_HEADS * HEAD_DIM
ROPE_AXIS = HEAD_DIM // 2
ROPE_THETA = 10000.0
HY_COLS = (HY_ORDER + 1) * HY_WIDTH
DA_COLS = 3 * DA_WIDTH
GQ_COLS = GQ_WIDTH + 2 * GQ_KV * HEAD_DIM
GATE_COLS = 3 * D_MODEL
IN_COLS = HY_COLS + DA_COLS + GQ_COLS + GATE_COLS
D_FF = 256 * ((8 * D_MODEL // 3 + 255) // 256)
N_EXPERTS = 8
MOE_FF = D_FF // 2
DEEPNORM_ALPHA = (2.0 * DEPTH) ** 0.25
EPS = 1e-5

LANES = 128
VMEM_LIMIT = 56 * 1024 * 1024

COL_GATE = 0
COL_HY = GATE_COLS
COL_DA = COL_HY + HY_COLS
COL_GQ = COL_DA + DA_COLS


def _params(sem):
    return pltpu.CompilerParams(dimension_semantics=sem, vmem_limit_bytes=VMEM_LIMIT)


def _dot(a, b):
    return jnp.dot(a, b, preferred_element_type=F32)


def _dot_nt(a, b):
    return lax.dot_general(a, b, (((1,), (1,)), ((), ())), preferred_element_type=F32)


def _split(x):
    hi = x.astype(BF16)
    lo = (x - hi.astype(F32)).astype(BF16)
    return hi, lo


def _dot3(a, b):
    ah, al = _split(a)
    bh, bl = _split(b)
    return _dot(ah, bh) + (_dot(ah, bl) + _dot(al, bh))


def _sigmoid(x):
    return 1.0 / (1.0 + jnp.exp(-x))


def _ln(x):
    mu = jnp.mean(x, axis=-1, keepdims=True)
    xc = x - mu
    var = jnp.mean(xc * xc, axis=-1, keepdims=True)
    return xc * lax.rsqrt(var + EPS)


def _head_ssq(x):
    w = x.shape[1]
    r = lax.broadcasted_iota(jnp.int32, (w, w), 0) // HEAD_DIM
    c = lax.broadcasted_iota(jnp.int32, (w, w), 1) // HEAD_DIM
    ones = jnp.where(r == c, 1.0, 0.0).astype(BF16)
    hi, lo = _split(x * x)
    return _dot(hi, ones) + _dot(lo, ones)


def _ada_kernel(c_ref, w_ref, b_ref, o_ref):
    c = c_ref[...]
    o_ref[...] = _dot3(c * _sigmoid(c), w_ref[...]) + b_ref[...]


def _ada(c, w_ada, b_ada):
    nb = c.shape[0]
    tn = 1536
    return pl.pallas_call(
        _ada_kernel,
        grid=(DEPTH, 6 * D_MODEL // tn),
        in_specs=[
            pl.BlockSpec((nb, D_MODEL), lambda l, j: (0, 0)),
            pl.BlockSpec((None, D_MODEL, tn), lambda l, j: (l, 0, j)),
            pl.BlockSpec((None, 1, tn), lambda l, j: (l, 0, j)),
        ],
        out_specs=pl.BlockSpec((None, nb, tn), lambda l, j: (l, 0, j)),
        out_shape=jax.ShapeDtypeStruct((DEPTH, nb, 6 * D_MODEL), F32),
        compiler_params=_params(("arbitrary", "arbitrary")),
        name="ada_mod",
    )(c, w_ada, b_ada.reshape(DEPTH, 1, 6 * D_MODEL))


LN_ROWS = 256


def _inproj_kernel(x_ref, mod_ref, w_ref, b_ref, o_ref, h_ref):
    rows = x_ref.shape[1]

    @pl.when(pl.program_id(1) == 0)
    def _():
        shift = mod_ref[0, 0:1, :]
        scale = 1.0 + mod_ref[0, 1:2, :]

        def body(i, carry):
            r = pl.ds(pl.multiple_of(i * LN_ROWS, LN_ROWS), LN_ROWS)
            h_ref[r, :] = (_ln(x_ref[0, r, :]) * scale + shift).astype(BF16)
            return carry

        lax.fori_loop(0, rows // LN_ROWS, body, 0)

    o_ref[0] = (_dot(h_ref[...], w_ref[...]) + b_ref[...]).astype(BF16)


def _inproj(x, mod, w, b):
    nb, seq, _ = x.shape
    tn = 768
    return pl.pallas_call(
        _inproj_kernel,
        grid=(nb, IN_COLS // tn),
        in_specs=[
            pl.BlockSpec((1, seq, D_MODEL), lambda i, j: (i, 0, 0)),
            pl.BlockSpec((1, 6, D_MODEL), lambda i, j: (i, 0, 0)),
            pl.BlockSpec((D_MODEL, tn), lambda i, j: (0, j)),
            pl.BlockSpec((1, tn), lambda i, j: (0, j)),
        ],
        out_specs=pl.BlockSpec((1, seq, tn), lambda i, j: (i, 0, j)),
        out_shape=jax.ShapeDtypeStruct((nb, seq, IN_COLS), BF16),
        scratch_shapes=[pltpu.VMEM((seq, D_MODEL), BF16)],
        compiler_params=_params(("arbitrary", "arbitrary")),
        name="ln_inproj",
    )(x, mod, w, b)


def _shortconv_kernel(z_ref, w_ref, b_ref, o_ref):
    z = z_ref[0].astype(F32)
    seq = z.shape[0]
    row = lax.broadcasted_iota(jnp.int32, z.shape, 0)
    prev = jnp.where(row == 0, 0.0, pltpu.roll(z, 1, 0))
    nxt = jnp.where(row == seq - 1, 0.0, pltpu.roll(z, seq - 1, 0))
    o = b_ref[...] + prev * w_ref[0:1, :] + z * w_ref[1:2, :] + nxt * w_ref[2:3, :]
    o_ref[0] = o.astype(BF16)


def _shortconv(z, w, b):
    nb, seq, _ = z.shape
    cb = COL_HY // HY_WIDTH
    return pl.pallas_call(
        _shortconv_kernel,
        grid=(nb, HY_COLS // HY_WIDTH),
        in_specs=[
            pl.BlockSpec((1, seq, HY_WIDTH), lambda i, j: (i, 0, cb + j)),
            pl.BlockSpec((SHORT_CONV, HY_WIDTH), lambda i, j: (0, j)),
            pl.BlockSpec((1, HY_WIDTH), lambda i, j: (0, j)),
        ],
        out_specs=pl.BlockSpec((1, seq, HY_WIDTH), lambda i, j: (i, 0, j)),
        out_shape=jax.ShapeDtypeStruct((nb, seq, HY_COLS), BF16),
        compiler_params=_params(("arbitrary", "arbitrary")),
        name="hy_shortconv",
    )(z, w, b)


FILT_ROWS = 512


def _filter_kernel(bands_ref, w1_ref, b1_ref, w2_ref, b2_ref, w3_ref, b3_ref, dec_ref, o_ref, ts_ref, *, seq):
    n = 2 * seq
    rows = min(FILT_ROWS, n)
    hw = HY_ORDER * HY_WIDTH
    lane = lax.broadcasted_iota(jnp.int32, (1, LANES), 1)

    def fill(i, asum):
        j0 = pl.multiple_of(i * rows, rows)
        j = j0 + lax.broadcasted_iota(jnp.int32, (rows, 1), 0)
        t = jnp.where(j < seq, j, n - j).astype(F32)
        t_norm = t / max(seq - 1, 1)
        ang = (2.0 * math.pi / seq) * t * bands_ref[...]
        feats = jnp.where(lane == 0, t_norm,
                          jnp.where(lane <= N_BANDS, jnp.cos(ang),
                                    jnp.where(lane <= 2 * N_BANDS, -jnp.sin(ang), 0.0)))
        h = jnp.sin(_dot3(feats, w1_ref[...]) + b1_ref[...])
        h = jnp.sin(_dot3(h, w2_ref[...]) + b2_ref[...])
        h = _dot3(h, w3_ref[...]) + b3_ref[...]
        window = jnp.exp(-t_norm * dec_ref[...]) + DECAY_SHIFT
        sel = jnp.where(j < seq, h[:, :hw], h[:, hw:]) * window
        sel = jnp.where(j == seq, 0.0, sel)
        ts_ref[pl.ds(j0, rows), :] = sel
        return asum + jnp.sum(jnp.abs(sel), axis=0, keepdims=True)

    asum = lax.fori_loop(0, n // rows, fill, jnp.zeros((1, hw), F32))
    inv = 1.0 / (asum + EPS)

    def norm(i, carry):
        r = pl.ds(pl.multiple_of(i * rows, rows), rows)
        o_ref[r, :] = (ts_ref[r, :] * inv).astype(BF16)
        return carry

    lax.fori_loop(0, n // rows, norm, 0)


def _filters_time(seq, bands, w1, b1, w2, b2, w3, b3, dec):
    hw = HY_ORDER * HY_WIDTH
    return pl.pallas_call(
        functools.partial(_filter_kernel, seq=seq),
        out_shape=jax.ShapeDtypeStruct((2 * seq, hw), BF16),
        scratch_shapes=[pltpu.VMEM((2 * seq, hw), F32)],
        compiler_params=pltpu.CompilerParams(vmem_limit_bytes=VMEM_LIMIT),
        name="hy_filter_time",
    )(bands, w1, b1, w2, b2, w3, b3, dec)


def _matmul_kernel(a_ref, b_ref, o_ref):
    o_ref[...] = _dot(a_ref[...], b_ref[...])


def _filter_dft(wf, ts):
    n, hw = ts.shape
    tm = min(512, n)
    return pl.pallas_call(
        _matmul_kernel,
        grid=(n // tm,),
        in_specs=[pl.BlockSpec((tm, n), lambda i: (i, 0)), pl.BlockSpec((n, hw), lambda i: (0, 0))],
        out_specs=pl.BlockSpec((tm, hw), lambda i: (i, 0)),
        out_shape=jax.ShapeDtypeStruct((n, hw), F32),
        compiler_params=_params(("arbitrary",)),
        name="hy_filter_dft",
    )(wf, ts)


def _longconv_kernel(v_ref, g_ref, w_ref, wi_ref, kf_ref, bias_ref, o_ref, acc_ref):
    kt = pl.program_id(1)
    nb = v_ref.shape[0]
    tf = w_ref.shape[1]

    @pl.when(kt == 0)
    def _():
        acc_ref[...] = jnp.zeros_like(acc_ref)

    kr = kf_ref[0]
    ki = kf_ref[1]
    first = (lax.broadcasted_iota(jnp.int32, (tf, 1), 0) == 0) & (kt == 0)
    for n in range(nb):
        v = v_ref[n]
        ur = _dot(w_ref[0], v)
        ui = _dot(w_ref[1], v)
        gr = ur * kr - jnp.where(first, 0.0, ui * ki)
        gi = jnp.where(first, ui * ki, ur * ki + ui * kr)
        acc_ref[n] += _dot(wi_ref[0], gr.astype(BF16)) + _dot(wi_ref[1], gi.astype(BF16))

    @pl.when(kt == pl.num_programs(1) - 1)
    def _():
        for n in range(nb):
            vf = v_ref[n].astype(F32)
            y = (acc_ref[n] + vf * bias_ref[...]).astype(BF16)
            o_ref[n] = (g_ref[n] * y).astype(BF16)


def _longconv(v_arr, v_col, g_arr, g_col, w, wi, kf, order, bias):
    nbatch, seq, _ = v_arr.shape
    nb = math.gcd(nbatch, 4)
    tf = min(256, seq)
    c = HY_WIDTH
    return pl.pallas_call(
        _longconv_kernel,
        grid=(nbatch // nb, seq // tf),
        in_specs=[
            pl.BlockSpec((nb, seq, c), lambda i, k: (i, 0, v_col)),
            pl.BlockSpec((nb, seq, c), lambda i, k: (i, 0, g_col)),
            pl.BlockSpec((2, tf, seq), lambda i, k: (0, k, 0)),
            pl.BlockSpec((2, seq, tf), lambda i, k: (0, 0, k)),
            pl.BlockSpec((2, tf, c), lambda i, k: (0, k, order)),
            pl.BlockSpec((1, c), lambda i, k: (0, 0)),
        ],
        out_specs=pl.BlockSpec((nb, seq, c), lambda i, k: (i, 0, 0)),
        out_shape=jax.ShapeDtypeStruct((nbatch, seq, c), BF16),
        scratch_shapes=[pltpu.VMEM((nb, seq, c), F32)],
        compiler_params=_params(("arbitrary", "arbitrary")),
        name="hy_longconv",
    )(v_arr, g_arr, w, wi, kf, bias)


def _lane_mask(lo, width, n=LANES):
    lane = lax.broadcasted_iota(jnp.int32, (1, n), 1)
    return (lane >= lo) & (lane < lo + width)


def _da_kernel(q_ref, k_ref, v_ref, lam_ref, g_ref, o_ref, *, lam_init):
    tq = q_ref.shape[1]
    seq = k_ref.shape[1]
    qi = pl.program_id(1)
    lv = lam_ref[...]
    lam = (jnp.exp(jnp.sum(lv[0:1] * lv[1:2], axis=1, keepdims=True))
           - jnp.exp(jnp.sum(lv[2:3] * lv[3:4], axis=1, keepdims=True)) + lam_init)
    qpos = qi * tq + lax.broadcasted_iota(jnp.int32, (tq, seq), 0)
    kpos = lax.broadcasted_iota(jnp.int32, (tq, seq), 1)
    dist = jnp.abs(qpos - kpos).astype(F32)
    slabs = []
    for s in range(DA_WIDTH // LANES):
        cols = slice(s * LANES, (s + 1) * LANES)
        q_slab = q_ref[0, :, cols]
        k_slab = k_ref[0, :, cols]
        v_slab = v_ref[0, :, cols]
        o_slab = jnp.zeros((tq, LANES), F32)
        for hh in range(LANES // HEAD_DIM):
            head = s * (LANES // HEAD_DIM) + hh
            slope = 2.0 ** (-8.0 * (head + 1) / DA_HEADS)
            probs = []
            for c in range(2):
                keep = jnp.where(_lane_mask(hh * HEAD_DIM + c * DA_HALF, DA_HALF), 1.0, 0.0).astype(BF16)
                sc = _dot_nt(q_slab * keep, k_slab) - slope * dist
                p = jnp.exp(sc - jnp.max(sc, axis=1, keepdims=True))
                probs.append((p, jnp.sum(p, axis=1, keepdims=True)))
            a = probs[0][0] * (1.0 / probs[0][1]) - probs[1][0] * (lam / probs[1][1])
            o2 = _dot(a.astype(BF16), v_slab)
            o_slab = jnp.where(_lane_mask(hh * HEAD_DIM, HEAD_DIM), o2, o_slab)
        slabs.append(o_slab)
    o = jnp.concatenate(slabs, axis=1)
    y = o * lax.rsqrt(_head_ssq(o) * (1.0 / HEAD_DIM) + EPS) * g_ref[...]
    o_ref[0] = (y * (1.0 - lam_init)).astype(BF16)


def _diff_attn(z, lam_vecs, g_tiled, lam_init):
    nb, seq, _ = z.shape
    tq = min(256, seq)
    cb = COL_DA // DA_WIDTH
    return pl.pallas_call(
        functools.partial(_da_kernel, lam_init=lam_init),
        grid=(nb, seq // tq),
        in_specs=[
            pl.BlockSpec((1, tq, DA_WIDTH), lambda b, i: (b, i, cb)),
            pl.BlockSpec((1, seq, DA_WIDTH), lambda b, i: (b, 0, cb + 1)),
            pl.BlockSpec((1, seq, DA_WIDTH), lambda b, i: (b, 0, cb + 2)),
            pl.BlockSpec((4, DA_HALF), lambda b, i: (0, 0)),
            pl.BlockSpec((1, DA_WIDTH), lambda b, i: (0, 0)),
        ],
        out_specs=pl.BlockSpec((1, tq, DA_WIDTH), lambda b, i: (b, i, 0)),
        out_shape=jax.ShapeDtypeStruct((nb, seq, DA_WIDTH), BF16),
        compiler_params=_params(("arbitrary", "arbitrary")),
        name="diff_attn",
    )(z, z, z, lam_vecs, g_tiled)


def _rope(x, cos, sin):
    w = x.shape[1]
    lane = lax.broadcasted_iota(jnp.int32, (1, w), 1)
    half = ROPE_AXIS // 2
    partner = jnp.where(lane % ROPE_AXIS < half, pltpu.roll(x, w - half, 1), pltpu.roll(x, half, 1))
    return x * cos + partner * sin


def _dup_heads(x):
    first = _lane_mask(0, HEAD_DIM)
    a = jnp.where(first, x, 0.0)
    b = jnp.where(first, 0.0, x)
    return jnp.concatenate([a + pltpu.roll(a, HEAD_DIM, 1), b + pltpu.roll(b, HEAD_DIM, 1)], axis=1)


def _gqa_prep_kernel(q_ref, kv_ref, cos_ref, sin_ref, gq_ref, gk_ref, qo_ref, ko_ref, vo_ref):
    kvw = GQ_KV * HEAD_DIM
    q = q_ref[0].astype(F32)
    qn = q * lax.rsqrt(_head_ssq(q) * (1.0 / HEAD_DIM) + EPS) * gq_ref[...]
    qo_ref[0] = (_rope(qn, cos_ref[...], sin_ref[...]) * HEAD_DIM ** -0.5).astype(BF16)
    k = kv_ref[0, :, :kvw].astype(F32)
    kn = k * lax.rsqrt(_head_ssq(k) * (1.0 / HEAD_DIM) + EPS) * gk_ref[...]
    kr = _rope(kn, cos_ref[:, :kvw], sin_ref[:, :kvw])
    ko_ref[0] = _dup_heads(kr).astype(BF16)
    vo_ref[0] = _dup_heads(kv_ref[0, :, kvw:].astype(F32)).astype(BF16)


def _gqa_prep(z, cos, sin, gq, gk):
    nb, seq, _ = z.shape
    tl = min(512, seq)
    kvw = GQ_KV * HEAD_DIM
    return pl.pallas_call(
        _gqa_prep_kernel,
        grid=(nb, seq // tl),
        in_specs=[
            pl.BlockSpec((1, tl, GQ_WIDTH), lambda b, i: (b, i, COL_GQ // GQ_WIDTH)),
            pl.BlockSpec((1, tl, 2 * kvw), lambda b, i: (b, i, (COL_GQ + GQ_WIDTH) // (2 * kvw))),
            pl.BlockSpec((tl, GQ_WIDTH), lambda b, i: (i, 0)),
            pl.BlockSpec((tl, GQ_WIDTH), lambda b, i: (i, 0)),
            pl.BlockSpec((1, GQ_WIDTH), lambda b, i: (0, 0)),
            pl.BlockSpec((1, kvw), lambda b, i: (0, 0)),
        ],
        out_specs=[
            pl.BlockSpec((1, tl, GQ_WIDTH), lambda b, i: (b, i, 0)),
            pl.BlockSpec((1, tl, 2 * kvw), lambda b, i: (b, i, 0)),
            pl.BlockSpec((1, tl, 2 * kvw), lambda b, i: (b, i, 0)),
        ],
        out_shape=[
            jax.ShapeDtypeStruct((nb, seq, GQ_WIDTH), BF16),
            jax.ShapeDtypeStruct((nb, seq, 2 * kvw), BF16),
            jax.ShapeDtypeStruct((nb, seq, 2 * kvw), BF16),
        ],
        compiler_params=_params(("arbitrary", "arbitrary")),
        name="gqa_prep",
    )(z, z, cos, sin, gq, gk)


def _gqa_kernel(q_ref, k_ref, v_ref, o_ref):
    tq = q_ref.shape[1]
    per_slab = LANES // HEAD_DIM
    group = GQ_HEADS // GQ_KV
    slabs = []
    for s in range(GQ_WIDTH // LANES):
        q_slab = q_ref[0, :, s * LANES:(s + 1) * LANES]
        kv = (s * per_slab) // group
        k_dup = k_ref[0, :, kv * LANES:(kv + 1) * LANES]
        v_dup = v_ref[0, :, kv * LANES:(kv + 1) * LANES]
        o_slab = jnp.zeros((tq, LANES), F32)
        for hh in range(per_slab):
            keep = jnp.where(_lane_mask(hh * HEAD_DIM, HEAD_DIM), 1.0, 0.0).astype(BF16)
            sc = _dot_nt(q_slab * keep, k_dup)
            p = jnp.exp(sc - jnp.max(sc, axis=1, keepdims=True))
            inv = 1.0 / jnp.sum(p, axis=1, keepdims=True)
            o2 = _dot(p.astype(BF16), v_dup) * inv
            o_slab = jnp.where(_lane_mask(hh * HEAD_DIM, HEAD_DIM), o2, o_slab)
        slabs.append(o_slab)
    o_ref[0] = jnp.concatenate(slabs, axis=1).astype(BF16)


def _gqa_attn(q, k, v):
    nb, seq, _ = q.shape
    tq = min(256, seq)
    kvw = 2 * GQ_KV * HEAD_DIM
    return pl.pallas_call(
        _gqa_kernel,
        grid=(nb, seq // tq),
        in_specs=[
            pl.BlockSpec((1, tq, GQ_WIDTH), lambda b, i: (b, i, 0)),
            pl.BlockSpec((1, seq, kvw), lambda b, i: (b, 0, 0)),
            pl.BlockSpec((1, seq, kvw), lambda b, i: (b, 0, 0)),
        ],
        out_specs=pl.BlockSpec((1, tq, GQ_WIDTH), lambda b, i: (b, i, 0)),
        out_shape=jax.ShapeDtypeStruct((nb, seq, GQ_WIDTH), BF16),
        compiler_params=_params(("arbitrary", "arbitrary")),
        name="gqa_attn",
    )(q, k, v)


def _merge_kernel(ya_ref, yb_ref, yc_ref, zg_ref, x_ref, mod_ref, wa_ref, wb_ref, wc_ref, wo_ref,
                  lg_ref, lb_ref, wr_ref, br_ref, xo_ref, h_ref, comb_ref):
    d = D_MODEL
    m = _sigmoid(zg_ref[0, :, 0:d].astype(F32)) * _dot(ya_ref[0], wa_ref[...])
    m = m + _sigmoid(zg_ref[0, :, d:2 * d].astype(F32)) * _dot(yb_ref[0], wb_ref[...])
    m = m + _sigmoid(zg_ref[0, :, 2 * d:3 * d].astype(F32)) * _dot(yc_ref[0], wc_ref[...])
    y = _dot(m.astype(BF16), wo_ref[...])
    gate1 = 1.0 + mod_ref[0, 2:3, :]
    xn = _ln(DEEPNORM_ALPHA * x_ref[0] + gate1 * y) * lg_ref[...] + lb_ref[...]
    xo_ref[0] = xn
    h = _ln(xn) * (1.0 + mod_ref[0, 4:5, :]) + mod_ref[0, 3:4, :]
    h_ref[0] = h.astype(BF16)
    lane = lax.broadcasted_iota(jnp.int32, (1, LANES), 1).astype(F32)
    neg = -jnp.inf
    logits = jnp.where(lane < N_EXPERTS, _dot3(h, wr_ref[...]) + br_ref[...], neg)
    v1 = jnp.max(logits, axis=1, keepdims=True)
    i1 = jnp.min(jnp.where(logits == v1, lane, float(LANES)), axis=1, keepdims=True)
    rest = jnp.where(lane == i1, neg, logits)
    v2 = jnp.max(rest, axis=1, keepdims=True)
    i2 = jnp.min(jnp.where(rest == v2, lane, float(LANES)), axis=1, keepdims=True)
    e = jnp.exp(v2 - v1)
    w1 = 1.0 / (1.0 + e)
    comb_ref[0] = jnp.where(lane == i1, w1, 0.0) + jnp.where(lane == i2, e * w1, 0.0)


def _merge(ya, yb, yc, z, x, mod, wa, wb, wc, wo, lg, lb, wr, br):
    nb, seq, d = x.shape
    tm = min(512, seq)
    full = lambda shape: pl.BlockSpec(shape, lambda b, i: (0,) * len(shape))
    row = lambda w, col=0: pl.BlockSpec((1, tm, w), lambda b, i: (b, i, col))
    return pl.pallas_call(
        _merge_kernel,
        grid=(nb, seq // tm),
        in_specs=[
            row(HY_WIDTH), row(DA_WIDTH), row(GQ_WIDTH), row(GATE_COLS, COL_GATE // GATE_COLS), row(d),
            pl.BlockSpec((1, 6, d), lambda b, i: (b, 0, 0)),
            full((HY_WIDTH, d)), full((DA_WIDTH, d)), full((GQ_WIDTH, d)), full((d, d)),
            full((1, d)), full((1, d)), full((d, LANES)), full((1, LANES)),
        ],
        out_specs=[row(d), row(d), row(LANES)],
        out_shape=[
            jax.ShapeDtypeStruct((nb, seq, d), F32),
            jax.ShapeDtypeStruct((nb, seq, d), BF16),
            jax.ShapeDtypeStruct((nb, seq, LANES), F32),
        ],
        compiler_params=_params(("arbitrary", "arbitrary")),
        name="merge_outproj",
    )(ya, yb, yc, z, x, mod, wa, wb, wc, wo, lg, lb, wr, br)


def _ffn_kernel(h_ref, wg_ref, wu_ref, wd_ref, x_ref, mod_ref, lg_ref, lb_ref, o_ref, acc_ref):
    j = pl.program_id(2)

    @pl.when(j == 0)
    def _():
        acc_ref[...] = jnp.zeros_like(acc_ref)

    h = h_ref[0]
    g = _dot(h, wg_ref[...])
    u = _dot(h, wu_ref[...])
    a = g * _sigmoid(g) * u
    acc_ref[...] += _dot(a.astype(BF16), wd_ref[...])

    @pl.when(j == pl.num_programs(2) - 1)
    def _():
        gate2 = 1.0 + mod_ref[0, 5:6, :]
        o_ref[0] = _ln(DEEPNORM_ALPHA * x_ref[0] + gate2 * acc_ref[...]) * lg_ref[...] + lb_ref[...]


def _ffn(h, w_up, w_down, x, mod, lg, lb):
    nb, seq, d = x.shape
    ff = w_down.shape[0]
    tm = min(1024, seq)
    tf = 256
    nf = ff // tf
    row = lambda w: pl.BlockSpec((1, tm, w), lambda b, i, j: (b, i, 0))
    vec = pl.BlockSpec((1, d), lambda b, i, j: (0, 0))
    return pl.pallas_call(
        _ffn_kernel,
        grid=(nb, seq // tm, nf),
        in_specs=[
            row(d),
            pl.BlockSpec((d, tf), lambda b, i, j: (0, j)),
            pl.BlockSpec((d, tf), lambda b, i, j: (0, nf + j)),
            pl.BlockSpec((tf, d), lambda b, i, j: (j, 0)),
            row(d),
            pl.BlockSpec((1, 6, d), lambda b, i, j: (b, 0, 0)),
            vec, vec,
        ],
        out_specs=row(d),
        out_shape=jax.ShapeDtypeStruct((nb, seq, d), F32),
        scratch_shapes=[pltpu.VMEM((tm, d), F32)],
        compiler_params=_params(("arbitrary",) * 3),
        name="swiglu_dense",
    )(h, w_up, w_up, w_down, x, mod, lg, lb)


MOE_TOKENS = 1024
MOE_CHUNK = 320


def _moe_kernel(h_ref, comb_ref, wg_ref, wu_ref, wd_ref, x_ref, mod_ref, lg_ref, lb_ref, o_ref,
                posc_ref, posr_ref, cnt_ref):
    e = pl.program_id(2)
    tm = h_ref.shape[1]
    lane = lax.broadcasted_iota(jnp.int32, (1, LANES), 1)

    @pl.when(e == 0)
    def _():
        comb = comb_ref[0]
        sel = jnp.where(comb > 0.0, 1.0, 0.0)
        r = lax.broadcasted_iota(jnp.int32, (tm, tm), 0)
        c = lax.broadcasted_iota(jnp.int32, (tm, tm), 1)
        before = jnp.where(c < r, 1.0, 0.0).astype(BF16)
        rank = _dot(before, sel.astype(BF16))
        posc = jnp.where(comb > 0.0, rank, -1.0)
        posc_ref[...] = posc
        posr_ref[...] = posc.T
        cnt_ref[...] = jnp.sum(sel, axis=0, keepdims=True)
        o_ref[0] = jnp.zeros(o_ref.shape[1:], F32)

    mine = lane == e
    pos_col = jnp.sum(jnp.where(mine, posc_ref[...], 0.0), axis=1, keepdims=True)
    w_col = jnp.sum(jnp.where(mine, comb_ref[0], 0.0), axis=1, keepdims=True)
    pos_row = posr_ref[pl.ds(e, 1), :]
    count = jnp.sum(jnp.where(mine, cnt_ref[...], 0.0))

    for c in range(-(-tm // MOE_CHUNK)):
        @pl.when(count > c * MOE_CHUNK)
        def _():
            slot_r = (c * MOE_CHUNK + lax.broadcasted_iota(jnp.int32, (MOE_CHUNK, 1), 0)).astype(F32)
            slot_c = (c * MOE_CHUNK + lax.broadcasted_iota(jnp.int32, (1, MOE_CHUNK), 1)).astype(F32)
            gather = jnp.where(pos_row == slot_r, 1.0, 0.0).astype(BF16)
            xs = _dot(gather, h_ref[0]).astype(BF16)
            g = _dot(xs, wg_ref[...])
            u = _dot(xs, wu_ref[...])
            a = (g * _sigmoid(g) * u).astype(BF16)
            y = _dot(a, wd_ref[...]).astype(BF16)
            scatter = jnp.where(pos_col == slot_c, 1.0, 0.0).astype(BF16)
            o_ref[0] += w_col * _dot(scatter, y)

    @pl.when(e == pl.num_programs(2) - 1)
    def _():
        gate2 = 1.0 + mod_ref[0, 5:6, :]
        o_ref[0] = _ln(DEEPNORM_ALPHA * x_ref[0] + gate2 * o_ref[0]) * lg_ref[...] + lb_ref[...]


def _moe(h, comb, w_up, w_down, x, mod, lg, lb):
    nb, seq, d = x.shape
    ne, ff, _ = w_down.shape
    tm = min(MOE_TOKENS, seq)
    row = lambda w: pl.BlockSpec((1, tm, w), lambda b, i, e: (b, i, 0))
    vec = pl.BlockSpec((1, d), lambda b, i, e: (0, 0))
    return pl.pallas_call(
        _moe_kernel,
        grid=(nb, seq // tm, ne),
        in_specs=[
            row(d), row(LANES),
            pl.BlockSpec((None, d, ff), lambda b, i, e: (e, 0, 0)),
            pl.BlockSpec((None, d, ff), lambda b, i, e: (e, 0, 1)),
            pl.BlockSpec((None, ff, d), lambda b, i, e: (e, 0, 0)),
            row(d),
            pl.BlockSpec((1, 6, d), lambda b, i, e: (b, 0, 0)),
            vec, vec,
        ],
        out_specs=row(d),
        out_shape=jax.ShapeDtypeStruct((nb, seq, d), F32),
        scratch_shapes=[pltpu.VMEM((tm, LANES), F32), pltpu.VMEM((LANES, tm), F32), pltpu.VMEM((1, LANES), F32)],
        compiler_params=_params(("arbitrary",) * 3),
        name="swiglu_routed",
    )(h, comb, w_up, w_up, w_down, x, mod, lg, lb)


def _dft_tables(seq):
    n = 2 * seq
    k = jnp.arange(seq, dtype=jnp.int32)[:, None]
    s = jnp.arange(n, dtype=jnp.int32)[None, :]
    ang = ((k * s) % n).astype(F32) * (2.0 * math.pi / n)
    cos = jnp.cos(ang)
    msin = -jnp.sin(ang)
    alt = jnp.where(s % 2 == 0, 1.0, -1.0).astype(F32)
    msin = jnp.where(k == 0, alt, msin)
    fwd = jnp.stack([cos, msin])
    scale = jnp.where(k == 0, 1.0 / n, 2.0 / n).astype(F32)
    inv = jnp.stack([cos[:, :seq] * scale, msin[:, :seq] * scale]).transpose(0, 2, 1)
    return fwd.astype(BF16), inv.astype(BF16)


def _rope_tables(seq):
    t = jnp.arange(seq, dtype=jnp.int32)
    inv = ROPE_THETA ** (-jnp.arange(0, ROPE_AXIS, 2, dtype=F32) / ROPE_AXIS)
    ang_r = (t // GRID_W).astype(F32)[:, None] * inv[None, :]
    ang_c = (t % GRID_W).astype(F32)[:, None] * inv[None, :]
    cos = jnp.concatenate([jnp.cos(ang_r)] * 2 + [jnp.cos(ang_c)] * 2, axis=1)
    sin = jnp.concatenate([-jnp.sin(ang_r), jnp.sin(ang_r), -jnp.sin(ang_c), jnp.sin(ang_c)], axis=1)
    return jnp.tile(cos, (1, GQ_HEADS)), jnp.tile(sin, (1, GQ_HEADS))


def _pad_to(a, rows, cols):
    return jnp.pad(a, ((0, rows - a.shape[0]), (0, cols - a.shape[1])))


def _prepare(seq, p):
    fwd, inv = _dft_tables(seq)
    cos, sin = _rope_tables(seq)
    bands = jnp.linspace(1e-4, N_BANDS - 1, N_BANDS, dtype=F32)
    bands_row = jnp.zeros((1, LANES), F32).at[0, 1:1 + N_BANDS].set(bands).at[0, 1 + N_BANDS:1 + 2 * N_BANDS].set(bands)
    deltas = jnp.abs(jnp.linspace(MIN_DECAY, MAX_DECAY, HY_WIDTH, dtype=F32))
    dec = jnp.tile(deltas, HY_ORDER)[None, :]
    da_scale = DA_HALF ** -0.5
    layers = []
    for l in range(DEPTH):
        w_in, b_in = p['w_in'][l], p['b_in'][l]
        a0, b0, c0, g0 = 0, HY_COLS, HY_COLS + DA_COLS, HY_COLS + DA_COLS + GQ_COLS
        w_cols = [w_in[:, g0:], w_in[:, a0:b0], w_in[:, b0:b0 + DA_WIDTH] * da_scale, w_in[:, b0 + DA_WIDTH:c0], w_in[:, c0:g0]]
        b_cols = [b_in[g0:], b_in[a0:b0], b_in[b0:b0 + DA_WIDTH] * da_scale, b_in[b0 + DA_WIDTH:c0], b_in[c0:g0]]
        ts = _filters_time(
            seq, bands_row,
            _pad_to(p['hy_f_w1'][l], LANES, LANES), _pad_to(p['hy_f_b1'][l][None, :], 1, LANES),
            _pad_to(p['hy_f_w2'][l], LANES, LANES), _pad_to(p['hy_f_b2'][l][None, :], 1, LANES),
            _pad_to(p['hy_f_w3'][l], LANES, 2 * HY_ORDER * HY_WIDTH), p['hy_f_b3'][l][None, :], dec)
        kf = _filter_dft(fwd.reshape(2 * seq, 2 * seq), ts).reshape(2, seq, HY_ORDER * HY_WIDTH)
        lay = dict(
            w_in=jnp.concatenate(w_cols, axis=1).astype(BF16),
            b_in=jnp.concatenate(b_cols)[None, :],
            conv_w=p['hy_conv_w'][l], conv_b=p['hy_conv_b'][l][None, :],
            kf=kf, hy_bias=p['hy_bias'][l],
            lam=jnp.stack([p['da_lam_q1'][l], p['da_lam_k1'][l], p['da_lam_q2'][l], p['da_lam_k2'][l]]),
            lam_init=0.8 - 0.6 * math.exp(-0.3 * l),
            subln=jnp.tile(p['da_subln_g'][l], DA_HEADS)[None, :],
            gq=jnp.tile(p['gq_qnorm_g'][l], GQ_HEADS)[None, :],
            gk=jnp.tile(p['gq_knorm_g'][l], GQ_KV)[None, :],
            wa=p['w_br_a'][l].astype(BF16), wb=p['w_br_b'][l].astype(BF16), wc=p['w_br_c'][l].astype(BF16),
            wo=p['w_out'][l].astype(BF16),
            ln1_g=p['ln1_g'][l][None, :], ln1_b=p['ln1_b'][l][None, :],
            ln2_g=p['ln2_g'][l][None, :], ln2_b=p['ln2_b'][l][None, :],
        )
        if l % 2 == 0:
            lay.update(w_up=p['ffn_w_up'][l // 2].astype(BF16), w_down=p['ffn_w_down'][l // 2].astype(BF16),
                       wr=jnp.zeros((D_MODEL, LANES), F32), br=jnp.zeros((1, LANES), F32))
        else:
            lay.update(w_up=p['moe_w_up'][l // 2].astype(BF16), w_down=p['moe_w_down'][l // 2].astype(BF16),
                       wr=_pad_to(p['moe_w_router'][l // 2], D_MODEL, LANES),
                       br=_pad_to(p['moe_b_router'][l // 2][None, :], 1, LANES))
        layers.append(lay)
    return dict(fwd=fwd[:, :, :seq], inv=inv, cos=cos, sin=sin, layers=layers)


def _trunk(x, c, p, prep):
    mods = _ada(c, p['w_ada'], p['b_ada'])
    nb = x.shape[0]
    for l, lay in enumerate(prep['layers']):
        mod = mods[l].reshape(nb, 6, D_MODEL)
        z = _inproj(x, mod, lay['w_in'], lay['b_in'])
        zc = _shortconv(z, lay['conv_w'], lay['conv_b'])
        u = _longconv(zc, 2, zc, 0, prep['fwd'], prep['inv'], lay['kf'], 0, lay['hy_bias'][0:1])
        ya = _longconv(u, 0, zc, 1, prep['fwd'], prep['inv'], lay['kf'], 1, lay['hy_bias'][1:2])
        yb = _diff_attn(z, lay['lam'], lay['subln'], lay['lam_init'])
        q, kd, vd = _gqa_prep(z, prep['cos'], prep['sin'], lay['gq'], lay['gk'])
        yc = _gqa_attn(q, kd, vd)
        x, h, comb = _merge(ya, yb, yc, z, x, mod, lay['wa'], lay['wb'], lay['wc'], lay['wo'],
                            lay['ln1_g'], lay['ln1_b'], lay['wr'], lay['br'])
        if l % 2 == 0:
            x = _ffn(h, lay['w_up'], lay['w_down'], x, mod, lay['ln2_g'], lay['ln2_b'])
        else:
            x = _moe(h, comb, lay['w_up'], lay['w_down'], x, mod, lay['ln2_g'], lay['ln2_b'])
    return x


def kernel(x_prompt, x_sample, c_prompt, c_sample, w_ada, b_ada, w_in, b_in, hy_conv_w, hy_conv_b, hy_f_w1, hy_f_b1, hy_f_w2, hy_f_b2, hy_f_w3, hy_f_b3, hy_bias, da_lam_q1, da_lam_k1, da_lam_q2, da_lam_k2, da_subln_g, gq_qnorm_g, gq_knorm_g, w_br_a, w_br_b, w_br_c, w_out, ln1_g, ln1_b, ffn_w_up, ffn_w_down, moe_w_router, moe_b_router, moe_w_up, moe_w_down, ln2_g, ln2_b):
    p = dict(w_ada=w_ada, b_ada=b_ada, w_in=w_in, b_in=b_in,
             hy_conv_w=hy_conv_w, hy_conv_b=hy_conv_b, hy_f_w1=hy_f_w1, hy_f_b1=hy_f_b1,
             hy_f_w2=hy_f_w2, hy_f_b2=hy_f_b2, hy_f_w3=hy_f_w3, hy_f_b3=hy_f_b3, hy_bias=hy_bias,
             da_lam_q1=da_lam_q1, da_lam_k1=da_lam_k1, da_lam_q2=da_lam_q2, da_lam_k2=da_lam_k2,
             da_subln_g=da_subln_g, gq_qnorm_g=gq_qnorm_g, gq_knorm_g=gq_knorm_g,
             w_br_a=w_br_a, w_br_b=w_br_b, w_br_c=w_br_c, w_out=w_out, ln1_g=ln1_g, ln1_b=ln1_b,
             ffn_w_up=ffn_w_up, ffn_w_down=ffn_w_down, moe_w_router=moe_w_router,
             moe_b_router=moe_b_router, moe_w_up=moe_w_up, moe_w_down=moe_w_down,
             ln2_g=ln2_g, ln2_b=ln2_b)
    assert x_prompt.shape[1] == x_sample.shape[1]
    prep = _prepare(x_prompt.shape[1], p)
    return (_trunk(x_prompt, c_prompt, p, prep), _trunk(x_sample, c_sample, p, prep))
```

```python
import functools
import math

import jax
import jax.numpy as jnp
from jax import lax
from jax.experimental import pallas as pl
from jax.experimental.pallas import tpu as pltpu

F32 = jnp.float32
BF16 = jnp.bfloat16

D_MODEL = 1024
DEPTH = 2
GRID_W = 64
HEAD_DIM = 64
HY_WIDTH = D_MODEL // 4
HY_ORDER = 2
SHORT_CONV = 3
N_BANDS = 16
FILTER_HID = 64
DECAY_TARGET = 1e-2
MIN_DECAY = math.log(DECAY_TARGET) / 1.5
MAX_DECAY = math.log(DECAY_TARGET) / 0.3
DECAY_SHIFT = 0.05
DA_HEADS = 4
DA_HALF = HEAD_DIM // 2
DA_WIDTH = DA_HEADS * HEAD_DIM
GQ_HEADS = 8
GQ_KV = 2
GQ_WIDTH = GQ_HEADS * HEAD_DIM
ROPE_AXIS = HEAD_DIM // 2
ROPE_THETA = 10000.0
HY_COLS = (HY_ORDER + 1) * HY_WIDTH
DA_COLS = 3 * DA_WIDTH
GQ_COLS = GQ_WIDTH + 2 * GQ_KV * HEAD_DIM
GATE_COLS = 3 * D_MODEL
IN_COLS = HY_COLS + DA_COLS + GQ_COLS + GATE_COLS
D_FF = 256 * ((8 * D_MODEL // 3 + 255) // 256)
N_EXPERTS = 8
MOE_FF = D_FF // 2
DEEPNORM_ALPHA = (2.0 * DEPTH) ** 0.25
EPS = 1e-5

LANES = 128
VMEM_LIMIT = 56 * 1024 * 1024

COL_GATE = 0
COL_HY = GATE_COLS
COL_DA = COL_HY + HY_COLS
COL_GQ = COL_DA + DA_COLS


def _params(sem):
    return pltpu.CompilerParams(dimension_semantics=sem, vmem_limit_bytes=VMEM_LIMIT)


def _dot(a, b):
    return jnp.dot(a, b, preferred_element_type=F32)


def _dot_nt(a, b):
    return lax.dot_general(a, b, (((1,), (1,)), ((), ())), preferred_element_type=F32)


def _split(x):
    hi = x.astype(BF16)
    lo = (x - hi.astype(F32)).astype(BF16)
    return hi, lo


def _dot3(a, b):
    ah, al = _split(a)
    bh, bl = _split(b)
    return _dot(ah, bh) + (_dot(ah, bl) + _dot(al, bh))


def _sigmoid(x):
    return 0.5 * jnp.tanh(0.5 * x) + 0.5


def _ln(x):
    mu = jnp.mean(x, axis=-1, keepdims=True)
    xc = x - mu
    var = jnp.mean(xc * xc, axis=-1, keepdims=True)
    return xc * lax.rsqrt(var + EPS)


def _head_ssq(x):
    w = x.shape[1]
    r = lax.broadcasted_iota(jnp.int32, (w, w), 0) // HEAD_DIM
    c = lax.broadcasted_iota(jnp.int32, (w, w), 1) // HEAD_DIM
    ones = jnp.where(r == c, 1.0, 0.0).astype(BF16)
    hi, lo = _split(x * x)
    return _dot(hi, ones) + _dot(lo, ones)


def _ada_kernel(c_ref, w_ref, b_ref, o_ref):
    c = c_ref[...]
    o_ref[...] = _dot3(c * _sigmoid(c), w_ref[...]) + b_ref[...]


def _ada(c, w_ada, b_ada):
    nb = c.shape[0]
    tn = 1536
    return pl.pallas_call(
        _ada_kernel,
        grid=(DEPTH, 6 * D_MODEL // tn),
        in_specs=[
            pl.BlockSpec((nb, D_MODEL), lambda l, j: (0, 0)),
            pl.BlockSpec((None, D_MODEL, tn), lambda l, j: (l, 0, j)),
            pl.BlockSpec((None, 1, tn), lambda l, j: (l, 0, j)),
        ],
        out_specs=pl.BlockSpec((None, nb, tn), lambda l, j: (l, 0, j)),
        out_shape=jax.ShapeDtypeStruct((DEPTH, nb, 6 * D_MODEL), F32),
        compiler_params=_params(("arbitrary", "arbitrary")),
        name="ada_mod",
    )(c, w_ada, b_ada.reshape(DEPTH, 1, 6 * D_MODEL))


LN_ROWS = 256


def _inproj_kernel(x_ref, mod_ref, w_ref, b_ref, o_ref, h_ref):
    rows = x_ref.shape[1]

    @pl.when(pl.program_id(1) == 0)
    def _():
        shift = mod_ref[0, 0:1, :]
        scale = 1.0 + mod_ref[0, 1:2, :]

        def body(i, carry):
            r = pl.ds(pl.multiple_of(i * LN_ROWS, LN_ROWS), LN_ROWS)
            h_ref[r, :] = (_ln(x_ref[0, r, :]) * scale + shift).astype(BF16)
            return carry

        lax.fori_loop(0, rows // LN_ROWS, body, 0)

    o_ref[0] = (_dot(h_ref[...], w_ref[...]) + b_ref[...]).astype(BF16)


def _inproj(x, mod, w, b):
    nb, seq, _ = x.shape
    tn = 768
    return pl.pallas_call(
        _inproj_kernel,
        grid=(nb, IN_COLS // tn),
        in_specs=[
            pl.BlockSpec((1, seq, D_MODEL), lambda i, j: (i, 0, 0)),
            pl.BlockSpec((1, 6, D_MODEL), lambda i, j: (i, 0, 0)),
            pl.BlockSpec((D_MODEL, tn), lambda i, j: (0, j)),
            pl.BlockSpec((1, tn), lambda i, j: (0, j)),
        ],
        out_specs=pl.BlockSpec((1, seq, tn), lambda i, j: (i, 0, j)),
        out_shape=jax.ShapeDtypeStruct((nb, seq, IN_COLS), BF16),
        scratch_shapes=[pltpu.VMEM((seq, D_MODEL), BF16)],
        compiler_params=_params(("arbitrary", "arbitrary")),
        name="ln_inproj",
    )(x, mod, w, b)


def _shortconv_kernel(z_ref, w_ref, b_ref, o_ref):
    z = z_ref[0].astype(F32)
    seq = z.shape[0]
    row = lax.broadcasted_iota(jnp.int32, z.shape, 0)
    prev = jnp.where(row == 0, 0.0, pltpu.roll(z, 1, 0))
    nxt = jnp.where(row == seq - 1, 0.0, pltpu.roll(z, seq - 1, 0))
    o = b_ref[...] + prev * w_ref[0:1, :] + z * w_ref[1:2, :] + nxt * w_ref[2:3, :]
    o_ref[0] = o.astype(BF16)


def _shortconv(z, w, b):
    nb, seq, _ = z.shape
    cb = COL_HY // HY_WIDTH
    return pl.pallas_call(
        _shortconv_kernel,
        grid=(nb, HY_COLS // HY_WIDTH),
        in_specs=[
            pl.BlockSpec((1, seq, HY_WIDTH), lambda i, j: (i, 0, cb + j)),
            pl.BlockSpec((SHORT_CONV, HY_WIDTH), lambda i, j: (0, j)),
            pl.BlockSpec((1, HY_WIDTH), lambda i, j: (0, j)),
        ],
        out_specs=pl.BlockSpec((1, seq, HY_WIDTH), lambda i, j: (i, 0, j)),
        out_shape=jax.ShapeDtypeStruct((nb, seq, HY_COLS), BF16),
        compiler_params=_params(("arbitrary", "arbitrary")),
        name="hy_shortconv",
    )(z, w, b)


FILT_ROWS = 512


def _filter_kernel(bands_ref, w1_ref, b1_ref, w2_ref, b2_ref, w3_ref, b3_ref, dec_ref, o_ref, ts_ref, *, seq):
    n = 2 * seq
    rows = min(FILT_ROWS, n)
    hw = HY_ORDER * HY_WIDTH
    lane = lax.broadcasted_iota(jnp.int32, (1, LANES), 1)

    def fill(i, asum):
        j0 = pl.multiple_of(i * rows, rows)
        j = j0 + lax.broadcasted_iota(jnp.int32, (rows, 1), 0)
        t = jnp.where(j < seq, j, n - j).astype(F32)
        t_norm = t / max(seq - 1, 1)
        ang = (2.0 * math.pi / seq) * t * bands_ref[...]
        feats = jnp.where(lane == 0, t_norm,
                          jnp.where(lane <= N_BANDS, jnp.cos(ang),
                                    jnp.where(lane <= 2 * N_BANDS, -jnp.sin(ang), 0.0)))
        h = jnp.sin(_dot3(feats, w1_ref[...]) + b1_ref[...])
        h = jnp.sin(_dot3(h, w2_ref[...]) + b2_ref[...])
        h = _dot3(h, w3_ref[...]) + b3_ref[...]
        window = jnp.exp(-t_norm * dec_ref[...]) + DECAY_SHIFT
        sel = jnp.where(j < seq, h[:, :hw], h[:, hw:]) * window
        sel = jnp.where(j == seq, 0.0, sel)
        ts_ref[pl.ds(j0, rows), :] = sel
        return asum + jnp.sum(jnp.abs(sel), axis=0, keepdims=True)

    asum = lax.fori_loop(0, n // rows, fill, jnp.zeros((1, hw), F32))
    inv = 1.0 / (asum + EPS)

    def norm(i, carry):
        r = pl.ds(pl.multiple_of(i * rows, rows), rows)
        o_ref[r, :] = (ts_ref[r, :] * inv).astype(BF16)
        return carry

    lax.fori_loop(0, n // rows, norm, 0)


def _filters_time(seq, bands, w1, b1, w2, b2, w3, b3, dec):
    hw = HY_ORDER * HY_WIDTH
    return pl.pallas_call(
        functools.partial(_filter_kernel, seq=seq),
        out_shape=jax.ShapeDtypeStruct((2 * seq, hw), BF16),
        scratch_shapes=[pltpu.VMEM((2 * seq, hw), F32)],
        compiler_params=pltpu.CompilerParams(vmem_limit_bytes=VMEM_LIMIT),
        name="hy_filter_time",
    )(bands, w1, b1, w2, b2, w3, b3, dec)


def _matmul_kernel(a_ref, b_ref, o_ref):
    o_ref[...] = _dot(a_ref[...], b_ref[...])


def _filter_dft(wf, ts):
    n, hw = ts.shape
    tm = min(512, n)
    return pl.pallas_call(
        _matmul_kernel,
        grid=(n // tm,),
        in_specs=[pl.BlockSpec((tm, n), lambda i: (i, 0)), pl.BlockSpec((n, hw), lambda i: (0, 0))],
        out_specs=pl.BlockSpec((tm, hw), lambda i: (i, 0)),
        out_shape=jax.ShapeDtypeStruct((n, hw), F32),
        compiler_params=_params(("arbitrary",)),
        name="hy_filter_dft",
    )(wf, ts)


def _longconv_kernel(v_ref, g_ref, w_ref, wi_ref, kf_ref, bias_ref, o_ref, acc_ref):
    kt = pl.program_id(1)
    nb = v_ref.shape[0]
    tf = w_ref.shape[1]

    @pl.when(kt == 0)
    def _():
        acc_ref[...] = jnp.zeros_like(acc_ref)

    kr = kf_ref[0]
    ki = kf_ref[1]
    first = (lax.broadcasted_iota(jnp.int32, (tf, 1), 0) == 0) & (kt == 0)
    for n in range(nb):
        v = v_ref[n]
        ur = _dot(w_ref[0], v)
        ui = _dot(w_ref[1], v)
        gr = ur * kr - jnp.where(first, 0.0, ui * ki)
        gi = jnp.where(first, ui * ki, ur * ki + ui * kr)
        acc_ref[n] += _dot(wi_ref[0], gr.astype(BF16)) + _dot(wi_ref[1], gi.astype(BF16))

    @pl.when(kt == pl.num_programs(1) - 1)
    def _():
        for n in range(nb):
            vf = v_ref[n].astype(F32)
            y = (acc_ref[n] + vf * bias_ref[...]).astype(BF16)
            o_ref[n] = (g_ref[n] * y).astype(BF16)


def _longconv(v_arr, v_col, g_arr, g_col, w, wi, kf, order, bias):
    nbatch, seq, _ = v_arr.shape
    nb = math.gcd(nbatch, 4)
    tf = min(256, seq)
    c = HY_WIDTH
    return pl.pallas_call(
        _longconv_kernel,
        grid=(nbatch // nb, seq // tf),
        in_specs=[
            pl.BlockSpec((nb, seq, c), lambda i, k: (i, 0, v_col)),
            pl.BlockSpec((nb, seq, c), lambda i, k: (i, 0, g_col)),
            pl.BlockSpec((2, tf, seq), lambda i, k: (0, k, 0)),
            pl.BlockSpec((2, seq, tf), lambda i, k: (0, 0, k)),
            pl.BlockSpec((2, tf, c), lambda i, k: (0, k, order)),
            pl.BlockSpec((1, c), lambda i, k: (0, 0)),
        ],
        out_specs=pl.BlockSpec((nb, seq, c), lambda i, k: (i, 0, 0)),
        out_shape=jax.ShapeDtypeStruct((nbatch, seq, c), BF16),
        scratch_shapes=[pltpu.VMEM((nb, seq, c), F32)],
        compiler_params=_params(("arbitrary", "arbitrary")),
        name="hy_longconv",
    )(v_arr, g_arr, w, wi, kf, bias)


def _lane_mask(lo, width, n=LANES):
    lane = lax.broadcasted_iota(jnp.int32, (1, n), 1)
    return (lane >= lo) & (lane < lo + width)


ALIBI_SPLIT = 3
POS_RADIX = 256


def _da_coef(head):
    return 2.0 ** (-8.0 * (head + 1) / DA_HEADS) * math.log2(math.e)


def _alibi_tables(seq):
    pos = jnp.arange(seq, dtype=jnp.int32)
    hi = ((pos // POS_RADIX) * POS_RADIX).astype(F32)[:, None]
    lo = (pos % POS_RADIX).astype(F32)[:, None]
    ones = jnp.ones((seq, 1), F32)
    fq, fk = [], []
    for head in range(DA_HEADS):
        rest = jnp.float32(_da_coef(head))
        pieces = []
        for _ in range(ALIBI_SPLIT):
            piece = rest.astype(BF16).astype(F32)
            pieces.append(piece)
            rest = rest - piece
        cq = jnp.concatenate([ones * c for c in pieces], axis=1)
        fq.append(jnp.concatenate([hi] * ALIBI_SPLIT + [lo] * ALIBI_SPLIT + [cq, cq], axis=1))
        fk.append(jnp.concatenate([-cq, -cq] + [hi] * ALIBI_SPLIT + [lo] * ALIBI_SPLIT, axis=1))
    pad = lambda t: jnp.pad(t, ((0, 0), (0, 0), (0, LANES - 4 * ALIBI_SPLIT))).astype(BF16)
    fk = jnp.stack(fk)
    tq = min(DA_TQ, seq)
    key_tile = (pos // tq)[None, :, None]
    signed = [pad(fk * jnp.where(key_tile < qi, 1.0, jnp.where(key_tile > qi, -1.0, 0.0))) for qi in range(seq // tq)]
    return pad(jnp.stack(fq)), signed


def _da_kernel(q_ref, k_ref, v_ref, fq_ref, fk_ref, lam_ref, g_ref, o_ref, *, qi, lam_init):
    tq = q_ref.shape[1]
    seq = k_ref.shape[1]
    lv = lam_ref[...]
    lam = (jnp.exp(jnp.sum(lv[0:1] * lv[1:2], axis=1, keepdims=True))
           - jnp.exp(jnp.sum(lv[2:3] * lv[3:4], axis=1, keepdims=True)) + lam_init)
    local = jnp.abs(lax.broadcasted_iota(jnp.int32, (tq, tq), 0)
                    - lax.broadcasted_iota(jnp.int32, (tq, tq), 1)).astype(F32)
    slabs = []
    for s in range(DA_WIDTH // LANES):
        cols = slice(s * LANES, (s + 1) * LANES)
        q_slab = q_ref[0, :, cols]
        k_slab = k_ref[0, :, cols]
        v_slab = v_ref[0, :, cols]
        o_slab = jnp.zeros((tq, LANES), F32)
        for hh in range(LANES // HEAD_DIM):
            head = s * (LANES // HEAD_DIM) + hh
            k_aug = jnp.concatenate([k_slab, fk_ref[head]], axis=1)
            mine = _lane_mask(hh * HEAD_DIM, HEAD_DIM)
            v_ones = jnp.where(mine, v_slab, jnp.ones_like(v_slab))
            maps = []
            for c in range(2):
                keep = jnp.where(_lane_mask(hh * HEAD_DIM + c * DA_HALF, DA_HALF), 1.0, 0.0).astype(BF16)
                q_aug = jnp.concatenate([q_slab * keep, fq_ref[head]], axis=1)
                sc = _dot_nt(q_aug, k_aug)
                pieces = [sc[:, qi * tq:(qi + 1) * tq] - _da_coef(head) * local]
                if qi > 0:
                    pieces.insert(0, sc[:, :qi * tq])
                if (qi + 1) * tq < seq:
                    pieces.append(sc[:, (qi + 1) * tq:])
                sc = jnp.concatenate(pieces, axis=1)
                p = jnp.exp2(sc - jnp.max(sc, axis=1, keepdims=True))
                o2 = _dot(p.astype(BF16), v_ones)
                maps.append((o2, o2[:, (1 - hh) * HEAD_DIM:(1 - hh) * HEAD_DIM + 1]))
            diff = maps[0][0] * (1.0 / maps[0][1]) - maps[1][0] * (lam / maps[1][1])
            o_slab = jnp.where(mine, diff, o_slab)
        slabs.append(o_slab)
    o = jnp.concatenate(slabs, axis=1)
    y = o * lax.rsqrt(_head_ssq(o) * (1.0 / HEAD_DIM) + EPS) * g_ref[...]
    o_ref[0] = (y * (1.0 - lam_init)).astype(BF16)


DA_TQ = 512


def _diff_attn(z, fq, fk_signed, lam_vecs, g_tiled, lam_init):
    nb, seq, _ = z.shape
    tq = min(DA_TQ, seq)
    cb = COL_DA // DA_WIDTH
    tiles = []
    for qi in range(seq // tq):
        tiles.append(pl.pallas_call(
            functools.partial(_da_kernel, qi=qi, lam_init=lam_init),
            grid=(nb,),
            in_specs=[
                pl.BlockSpec((1, tq, DA_WIDTH), lambda b, qi=qi: (b, qi, cb)),
                pl.BlockSpec((1, seq, DA_WIDTH), lambda b: (b, 0, cb + 1)),
                pl.BlockSpec((1, seq, DA_WIDTH), lambda b: (b, 0, cb + 2)),
                pl.BlockSpec((DA_HEADS, tq, LANES), lambda b, qi=qi: (0, qi, 0)),
                pl.BlockSpec((DA_HEADS, seq, LANES), lambda b: (0, 0, 0)),
                pl.BlockSpec((4, DA_HALF), lambda b: (0, 0)),
                pl.BlockSpec((1, DA_WIDTH), lambda b: (0, 0)),
            ],
            out_specs=pl.BlockSpec((1, tq, DA_WIDTH), lambda b: (b, 0, 0)),
            out_shape=jax.ShapeDtypeStruct((nb, tq, DA_WIDTH), BF16),
            compiler_params=_params(("arbitrary",)),
            name="diff_attn",
        )(z, z, z, fq, fk_signed[qi], lam_vecs, g_tiled))
    return jnp.concatenate(tiles, axis=1)


def _rope(x, cos, sin):
    w = x.shape[1]
    lane = lax.broadcasted_iota(jnp.int32, (1, w), 1)
    half = ROPE_AXIS // 2
    partner = jnp.where(lane % ROPE_AXIS < half, pltpu.roll(x, w - half, 1), pltpu.roll(x, half, 1))
    return x * cos + partner * sin


def _dup_heads(x):
    first = _lane_mask(0, HEAD_DIM)
    a = jnp.where(first, x, 0.0)
    b = jnp.where(first, 0.0, x)
    return jnp.concatenate([a + pltpu.roll(a, HEAD_DIM, 1), b + pltpu.roll(b, HEAD_DIM, 1)], axis=1)


def _with_ones(v):
    first = _lane_mask(0, HEAD_DIM)
    swapped = pltpu.roll(v, HEAD_DIM, 1)
    return jnp.concatenate([jnp.where(first, v, 1.0), jnp.where(first, 1.0, swapped),
                            jnp.where(first, swapped, 1.0), jnp.where(first, 1.0, v)], axis=1)


def _gqa_prep_kernel(q_ref, kv_ref, cos_ref, sin_ref, gq_ref, gk_ref, qo_ref, ko_ref, vo_ref):
    kvw = GQ_KV * HEAD_DIM
    q = q_ref[0].astype(F32)
    qn = q * lax.rsqrt(_head_ssq(q) * (1.0 / HEAD_DIM) + EPS) * gq_ref[...]
    qo_ref[0] = (_rope(qn, cos_ref[...], sin_ref[...]) * (HEAD_DIM ** -0.5 * math.log2(math.e))).astype(BF16)
    k = kv_ref[0, :, :kvw].astype(F32)
    kn = k * lax.rsqrt(_head_ssq(k) * (1.0 / HEAD_DIM) + EPS) * gk_ref[...]
    kr = _rope(kn, cos_ref[:, :kvw], sin_ref[:, :kvw])
    ko_ref[0] = _dup_heads(kr).astype(BF16)
    vo_ref[0] = _with_ones(kv_ref[0, :, kvw:].astype(F32)).astype(BF16)


def _gqa_prep(z, cos, sin, gq, gk):
    nb, seq, _ = z.shape
    tl = min(512, seq)
    kvw = GQ_KV * HEAD_DIM
    return pl.pallas_call(
        _gqa_prep_kernel,
        grid=(nb, seq // tl),
        in_specs=[
            pl.BlockSpec((1, tl, GQ_WIDTH), lambda b, i: (b, i, COL_GQ // GQ_WIDTH)),
            pl.BlockSpec((1, tl, 2 * kvw), lambda b, i: (b, i, (COL_GQ + GQ_WIDTH) // (2 * kvw))),
            pl.BlockSpec((tl, GQ_WIDTH), lambda b, i: (i, 0)),
            pl.BlockSpec((tl, GQ_WIDTH), lambda b, i: (i, 0)),
            pl.BlockSpec((1, GQ_WIDTH), lambda b, i: (0, 0)),
            pl.BlockSpec((1, kvw), lambda b, i: (0, 0)),
        ],
        out_specs=[
            pl.BlockSpec((1, tl, GQ_WIDTH), lambda b, i: (b, i, 0)),
            pl.BlockSpec((1, tl, 2 * kvw), lambda b, i: (b, i, 0)),
            pl.BlockSpec((1, tl, 4 * kvw), lambda b, i: (b, i, 0)),
        ],
        out_shape=[
            jax.ShapeDtypeStruct((nb, seq, GQ_WIDTH), BF16),
            jax.ShapeDtypeStruct((nb, seq, 2 * kvw), BF16),
            jax.ShapeDtypeStruct((nb, seq, 4 * kvw), BF16),
        ],
        compiler_params=_params(("arbitrary", "arbitrary")),
        name="gqa_prep",
    )(z, z, cos, sin, gq, gk)


def _gqa_kernel(q_ref, k_ref, v_ref, o_ref):
    tq = q_ref.shape[1]
    per_slab = LANES // HEAD_DIM
    group = GQ_HEADS // GQ_KV
    slabs = []
    for s in range(GQ_WIDTH // LANES):
        q_slab = q_ref[0, :, s * LANES:(s + 1) * LANES]
        kv = (s * per_slab) // group
        k_dup = k_ref[0, :, kv * LANES:(kv + 1) * LANES]
        o_slab = jnp.zeros((tq, LANES), F32)
        for hh in range(per_slab):
            keep = jnp.where(_lane_mask(hh * HEAD_DIM, HEAD_DIM), 1.0, 0.0).astype(BF16)
            sc = _dot_nt(q_slab * keep, k_dup)
            p = jnp.exp2(sc - jnp.max(sc, axis=1, keepdims=True))
            v_ones = v_ref[0, :, (kv * per_slab + hh) * LANES:(kv * per_slab + hh + 1) * LANES]
            o2 = _dot(p.astype(BF16), v_ones)
            sums = o2[:, (1 - hh) * HEAD_DIM:(1 - hh) * HEAD_DIM + 1]
            o_slab = jnp.where(_lane_mask(hh * HEAD_DIM, HEAD_DIM), o2 * (1.0 / sums), o_slab)
        slabs.append(o_slab)
    o_ref[0] = jnp.concatenate(slabs, axis=1).astype(BF16)


def _gqa_attn(q, k, v):
    nb, seq, _ = q.shape
    tq = min(512, seq)
    kvw = 2 * GQ_KV * HEAD_DIM
    return pl.pallas_call(
        _gqa_kernel,
        grid=(nb, seq // tq),
        in_specs=[
            pl.BlockSpec((1, tq, GQ_WIDTH), lambda b, i: (b, i, 0)),
            pl.BlockSpec((1, seq, kvw), lambda b, i: (b, 0, 0)),
            pl.BlockSpec((1, seq, 2 * kvw), lambda b, i: (b, 0, 0)),
        ],
        out_specs=pl.BlockSpec((1, tq, GQ_WIDTH), lambda b, i: (b, i, 0)),
        out_shape=jax.ShapeDtypeStruct((nb, seq, GQ_WIDTH), BF16),
        compiler_params=_params(("arbitrary", "arbitrary")),
        name="gqa_attn",
    )(q, k, v)


def _merge_kernel(ya_ref, yb_ref, yc_ref, zg_ref, x_ref, mod_ref, wa_ref, wb_ref, wc_ref, wo_ref,
                  lg_ref, lb_ref, *rest):
    d = D_MODEL
    m = _sigmoid(zg_ref[0, :, 0:d].astype(F32)) * _dot(ya_ref[0], wa_ref[...])
    m = m + _sigmoid(zg_ref[0, :, d:2 * d].astype(F32)) * _dot(yb_ref[0], wb_ref[...])
    m = m + _sigmoid(zg_ref[0, :, 2 * d:3 * d].astype(F32)) * _dot(yc_ref[0], wc_ref[...])
    y = _dot(m.astype(BF16), wo_ref[...])
    gate1 = 1.0 + mod_ref[0, 2:3, :]
    xn = _ln(DEEPNORM_ALPHA * x_ref[0] + gate1 * y) * lg_ref[...] + lb_ref[...]
    h = _ln(xn) * (1.0 + mod_ref[0, 4:5, :]) + mod_ref[0, 3:4, :]
    if len(rest) == 2:
        xo_ref, h_ref = rest
    else:
        wr_ref, br_ref, xo_ref, h_ref, comb_ref = rest
        lane = lax.broadcasted_iota(jnp.int32, (1, LANES), 1).astype(F32)
        neg = -jnp.inf
        logits = jnp.where(lane < N_EXPERTS, _dot3(h, wr_ref[...]) + br_ref[...], neg)
        v1 = jnp.max(logits, axis=1, keepdims=True)
        i1 = jnp.min(jnp.where(logits == v1, lane, float(LANES)), axis=1, keepdims=True)
        others = jnp.where(lane == i1, neg, logits)
        v2 = jnp.max(others, axis=1, keepdims=True)
        i2 = jnp.min(jnp.where(others == v2, lane, float(LANES)), axis=1, keepdims=True)
        e = jnp.exp(v2 - v1)
        w1 = 1.0 / (1.0 + e)
        comb_ref[0] = jnp.where(lane == i1, w1, 0.0) + jnp.where(lane == i2, e * w1, 0.0)
    xo_ref[0] = xn
    h_ref[0] = h.astype(BF16)


def _merge(ya, yb, yc, z, x, mod, wa, wb, wc, wo, lg, lb, router=None):
    nb, seq, d = x.shape
    tm = min(512, seq)
    full = lambda shape: pl.BlockSpec(shape, lambda b, i: (0,) * len(shape))
    row = lambda w, col=0: pl.BlockSpec((1, tm, w), lambda b, i: (b, i, col))
    in_specs = [
        row(HY_WIDTH), row(DA_WIDTH), row(GQ_WIDTH), row(GATE_COLS, COL_GATE // GATE_COLS), row(d),
        pl.BlockSpec((1, 6, d), lambda b, i: (b, 0, 0)),
        full((HY_WIDTH, d)), full((DA_WIDTH, d)), full((GQ_WIDTH, d)), full((d, d)),
        full((1, d)), full((1, d)),
    ]
    out_specs = [row(d), row(d)]
    out_shape = [jax.ShapeDtypeStruct((nb, seq, d), F32), jax.ShapeDtypeStruct((nb, seq, d), BF16)]
    args = [ya, yb, yc, z, x, mod, wa, wb, wc, wo, lg, lb]
    if router is not None:
        in_specs += [full((d, LANES)), full((1, LANES))]
        out_specs.append(row(LANES))
        out_shape.append(jax.ShapeDtypeStruct((nb, seq, LANES), F32))
        args += list(router)
    return pl.pallas_call(
        _merge_kernel,
        grid=(nb, seq // tm),
        in_specs=in_specs,
        out_specs=out_specs,
        out_shape=out_shape,
        compiler_params=_params(("arbitrary", "arbitrary")),
        name="merge_outproj",
    )(*args)


def _ffn_kernel(h_ref, wg_ref, wu_ref, wd_ref, x_ref, mod_ref, lg_ref, lb_ref, o_ref, acc_ref):
    j = pl.program_id(2)

    @pl.when(j == 0)
    def _():
        acc_ref[...] = jnp.zeros_like(acc_ref)

    h = h_ref[0]
    g = _dot(h, wg_ref[...])
    u = _dot(h, wu_ref[...])
    a = g * _sigmoid(g) * u
    acc_ref[...] += _dot(a.astype(BF16), wd_ref[...])

    @pl.when(j == pl.num_programs(2) - 1)
    def _():
        gate2 = 1.0 + mod_ref[0, 5:6, :]
        o_ref[0] = _ln(DEEPNORM_ALPHA * x_ref[0] + gate2 * acc_ref[...]) * lg_ref[...] + lb_ref[...]


def _ffn(h, w_up, w_down, x, mod, lg, lb):
    nb, seq, d = x.shape
    ff = w_down.shape[0]
    tm = min(1024, seq)
    tf = 256
    nf = ff // tf
    row = lambda w: pl.BlockSpec((1, tm, w), lambda b, i, j: (b, i, 0))
    vec = pl.BlockSpec((1, d), lambda b, i, j: (0, 0))
    return pl.pallas_call(
        _ffn_kernel,
        grid=(nb, seq // tm, nf),
        in_specs=[
            row(d),
            pl.BlockSpec((d, tf), lambda b, i, j: (0, j)),
            pl.BlockSpec((d, tf), lambda b, i, j: (0, nf + j)),
            pl.BlockSpec((tf, d), lambda b, i, j: (j, 0)),
            row(d),
            pl.BlockSpec((1, 6, d), lambda b, i, j: (b, 0, 0)),
            vec, vec,
        ],
        out_specs=row(d),
        out_shape=jax.ShapeDtypeStruct((nb, seq, d), F32),
        scratch_shapes=[pltpu.VMEM((tm, d), F32)],
        compiler_params=_params(("arbitrary",) * 3),
        name="swiglu_dense",
    )(h, w_up, w_up, w_down, x, mod, lg, lb)


MOE_TOKENS = 1024
MOE_CHUNK = 320


def _moe_kernel(h_ref, comb_ref, wg_ref, wu_ref, wd_ref, x_ref, mod_ref, lg_ref, lb_ref, o_ref,
                posc_ref, posr_ref, cnt_ref):
    e = pl.program_id(2)
    tm = h_ref.shape[1]
    lane = lax.broadcasted_iota(jnp.int32, (1, LANES), 1)

    @pl.when(e == 0)
    def _():
        comb = comb_ref[0]
        sel = jnp.where(comb > 0.0, 1.0, 0.0)
        r = lax.broadcasted_iota(jnp.int32, (tm, tm), 0)
        c = lax.broadcasted_iota(jnp.int32, (tm, tm), 1)
        before = jnp.where(c < r, 1.0, 0.0).astype(BF16)
        rank = _dot(before, sel.astype(BF16))
        posc = jnp.where(comb > 0.0, rank, -1.0)
        posc_ref[...] = posc
        posr_ref[...] = posc.T
        cnt_ref[...] = jnp.sum(sel, axis=0, keepdims=True)
        o_ref[0] = jnp.zeros(o_ref.shape[1:], F32)

    mine = lane == e
    pos_col = jnp.sum(jnp.where(mine, posc_ref[...], 0.0), axis=1, keepdims=True)
    w_col = jnp.sum(jnp.where(mine, comb_ref[0], 0.0), axis=1, keepdims=True)
    pos_row = posr_ref[pl.ds(e, 1), :]
    count = jnp.sum(jnp.where(mine, cnt_ref[...], 0.0))

    for c in range(-(-tm // MOE_CHUNK)):
        @pl.when(count > c * MOE_CHUNK)
        def _():
            slot_r = (c * MOE_CHUNK + lax.broadcasted_iota(jnp.int32, (MOE_CHUNK, 1), 0)).astype(F32)
            slot_c = (c * MOE_CHUNK + lax.broadcasted_iota(jnp.int32, (1, MOE_CHUNK), 1)).astype(F32)
            gather = jnp.where(pos_row == slot_r, 1.0, 0.0).astype(BF16)
            xs = _dot(gather, h_ref[0]).astype(BF16)
            g = _dot(xs, wg_ref[...])
            u = _dot(xs, wu_ref[...])
            a = (g * _sigmoid(g) * u).astype(BF16)
            y = _dot(a, wd_ref[...]).astype(BF16)
            scatter = jnp.where(pos_col == slot_c, 1.0, 0.0).astype(BF16)
            o_ref[0] += w_col * _dot(scatter, y)

    @pl.when(e == pl.num_programs(2) - 1)
    def _():
        gate2 = 1.0 + mod_ref[0, 5:6, :]
        o_ref[0] = _ln(DEEPNORM_ALPHA * x_ref[0] + gate2 * o_ref[0]) * lg_ref[...] + lb_ref[...]


def _moe(h, comb, w_up, w_down, x, mod, lg, lb):
    nb, seq, d = x.shape
    ne, ff, _ = w_down.shape
    tm = min(MOE_TOKENS, seq)
    row = lambda w: pl.BlockSpec((1, tm, w), lambda b, i, e: (b, i, 0))
    vec = pl.BlockSpec((1, d), lambda b, i, e: (0, 0))
    return pl.pallas_call(
        _moe_kernel,
        grid=(nb, seq // tm, ne),
        in_specs=[
            row(d), row(LANES),
            pl.BlockSpec((None, d, ff), lambda b, i, e: (e, 0, 0)),
            pl.BlockSpec((None, d, ff), lambda b, i, e: (e, 0, 1)),
            pl.BlockSpec((None, ff, d), lambda b, i, e: (e, 0, 0)),
            row(d),
            pl.BlockSpec((1, 6, d), lambda b, i, e: (b, 0, 0)),
            vec, vec,
        ],
        out_specs=row(d),
        out_shape=jax.ShapeDtypeStruct((nb, seq, d), F32),
        scratch_shapes=[pltpu.VMEM((tm, LANES), F32), pltpu.VMEM((LANES, tm), F32), pltpu.VMEM((1, LANES), F32)],
        compiler_params=_params(("arbitrary",) * 3),
        name="swiglu_routed",
    )(h, comb, w_up, w_up, w_down, x, mod, lg, lb)


def _dft_tables(seq):
    n = 2 * seq
    k = jnp.arange(seq, dtype=jnp.int32)[:, None]
    s = jnp.arange(n, dtype=jnp.int32)[None, :]
    ang = ((k * s) % n).astype(F32) * (2.0 * math.pi / n)
    cos = jnp.cos(ang)
    msin = -jnp.sin(ang)
    alt = jnp.where(s % 2 == 0, 1.0, -1.0).astype(F32)
    msin = jnp.where(k == 0, alt, msin)
    fwd = jnp.stack([cos, msin])
    scale = jnp.where(k == 0, 1.0 / n, 2.0 / n).astype(F32)
    inv = jnp.stack([cos[:, :seq] * scale, msin[:, :seq] * scale]).transpose(0, 2, 1)
    return fwd.astype(BF16), inv.astype(BF16)


def _rope_tables(seq):
    t = jnp.arange(seq, dtype=jnp.int32)
    inv = ROPE_THETA ** (-jnp.arange(0, ROPE_AXIS, 2, dtype=F32) / ROPE_AXIS)
    ang_r = (t // GRID_W).astype(F32)[:, None] * inv[None, :]
    ang_c = (t % GRID_W).astype(F32)[:, None] * inv[None, :]
    cos = jnp.concatenate([jnp.cos(ang_r)] * 2 + [jnp.cos(ang_c)] * 2, axis=1)
    sin = jnp.concatenate([-jnp.sin(ang_r), jnp.sin(ang_r), -jnp.sin(ang_c), jnp.sin(ang_c)], axis=1)
    return jnp.tile(cos, (1, GQ_HEADS)), jnp.tile(sin, (1, GQ_HEADS))


def _pad_to(a, rows, cols):
    return jnp.pad(a, ((0, rows - a.shape[0]), (0, cols - a.shape[1])))


def _prepare(seq, p):
    fwd, inv = _dft_tables(seq)
    cos, sin = _rope_tables(seq)
    bands = jnp.linspace(1e-4, N_BANDS - 1, N_BANDS, dtype=F32)
    bands_row = jnp.zeros((1, LANES), F32).at[0, 1:1 + N_BANDS].set(bands).at[0, 1 + N_BANDS:1 + 2 * N_BANDS].set(bands)
    deltas = jnp.abs(jnp.linspace(MIN_DECAY, MAX_DECAY, HY_WIDTH, dtype=F32))
    dec = jnp.tile(deltas, HY_ORDER)[None, :]
    da_scale = DA_HALF ** -0.5 * math.log2(math.e)
    fq, fk = _alibi_tables(seq)
    layers = []
    for l in range(DEPTH):
        w_in, b_in = p['w_in'][l], p['b_in'][l]
        a0, b0, c0, g0 = 0, HY_COLS, HY_COLS + DA_COLS, HY_COLS + DA_COLS + GQ_COLS
        w_cols = [w_in[:, g0:], w_in[:, a0:b0], w_in[:, b0:b0 + DA_WIDTH] * da_scale, w_in[:, b0 + DA_WIDTH:c0], w_in[:, c0:g0]]
        b_cols = [b_in[g0:], b_in[a0:b0], b_in[b0:b0 + DA_WIDTH] * da_scale, b_in[b0 + DA_WIDTH:c0], b_in[c0:g0]]
        ts = _filters_time(
            seq, bands_row,
            _pad_to(p['hy_f_w1'][l], LANES, LANES), _pad_to(p['hy_f_b1'][l][None, :], 1, LANES),
            _pad_to(p['hy_f_w2'][l], LANES, LANES), _pad_to(p['hy_f_b2'][l][None, :], 1, LANES),
            _pad_to(p['hy_f_w3'][l], LANES, 2 * HY_ORDER * HY_WIDTH), p['hy_f_b3'][l][None, :], dec)
        kf = _filter_dft(fwd.reshape(2 * seq, 2 * seq), ts).reshape(2, seq, HY_ORDER * HY_WIDTH)
        lay = dict(
            w_in=jnp.concatenate(w_cols, axis=1).astype(BF16),
            b_in=jnp.concatenate(b_cols)[None, :],
            conv_w=p['hy_conv_w'][l], conv_b=p['hy_conv_b'][l][None, :],
            kf=kf, hy_bias=p['hy_bias'][l],
            lam=jnp.stack([p['da_lam_q1'][l], p['da_lam_k1'][l], p['da_lam_q2'][l], p['da_lam_k2'][l]]),
            lam_init=0.8 - 0.6 * math.exp(-0.3 * l),
            subln=jnp.tile(p['da_subln_g'][l], DA_HEADS)[None, :],
            gq=jnp.tile(p['gq_qnorm_g'][l], GQ_HEADS)[None, :],
            gk=jnp.tile(p['gq_knorm_g'][l], GQ_KV)[None, :],
            wa=p['w_br_a'][l].astype(BF16), wb=p['w_br_b'][l].astype(BF16), wc=p['w_br_c'][l].astype(BF16),
            wo=p['w_out'][l].astype(BF16),
            ln1_g=p['ln1_g'][l][None, :], ln1_b=p['ln1_b'][l][None, :],
            ln2_g=p['ln2_g'][l][None, :], ln2_b=p['ln2_b'][l][None, :],
        )
        if l % 2 == 0:
            lay.update(w_up=p['ffn_w_up'][l // 2].astype(BF16), w_down=p['ffn_w_down'][l // 2].astype(BF16))
        else:
            lay.update(w_up=p['moe_w_up'][l // 2].astype(BF16), w_down=p['moe_w_down'][l // 2].astype(BF16),
                       router=(_pad_to(p['moe_w_router'][l // 2], D_MODEL, LANES),
                               _pad_to(p['moe_b_router'][l // 2][None, :], 1, LANES)))
        layers.append(lay)
    return dict(fwd=fwd[:, :, :seq], inv=inv, cos=cos, sin=sin, fq=fq, fk=fk, layers=layers)


def _trunk(x, c, p, prep):
    mods = _ada(c, p['w_ada'], p['b_ada'])
    nb = x.shape[0]
    for l, lay in enumerate(prep['layers']):
        mod = mods[l].reshape(nb, 6, D_MODEL)
        z = _inproj(x, mod, lay['w_in'], lay['b_in'])
        zc = _shortconv(z, lay['conv_w'], lay['conv_b'])
        u = _longconv(zc, 2, zc, 0, prep['fwd'], prep['inv'], lay['kf'], 0, lay['hy_bias'][0:1])
        ya = _longconv(u, 0, zc, 1, prep['fwd'], prep['inv'], lay['kf'], 1, lay['hy_bias'][1:2])
        yb = _diff_attn(z, prep['fq'], prep['fk'], lay['lam'], lay['subln'], lay['lam_init'])
        q, kd, vd = _gqa_prep(z, prep['cos'], prep['sin'], lay['gq'], lay['gk'])
        yc = _gqa_attn(q, kd, vd)
        merged = _merge(ya, yb, yc, z, x, mod, lay['wa'], lay['wb'], lay['wc'], lay['wo'],
                        lay['ln1_g'], lay['ln1_b'], lay.get('router'))
        if l % 2 == 0:
            x, h = merged
            x = _ffn(h, lay['w_up'], lay['w_down'], x, mod, lay['ln2_g'], lay['ln2_b'])
        else:
            x, h, comb = merged
            x = _moe(h, comb, lay['w_up'], lay['w_down'], x, mod, lay['ln2_g'], lay['ln2_b'])
    return x


def kernel(x_prompt, x_sample, c_prompt, c_sample, w_ada, b_ada, w_in, b_in, hy_conv_w, hy_conv_b, hy_f_w1, hy_f_b1, hy_f_w2, hy_f_b2, hy_f_w3, hy_f_b3, hy_bias, da_lam_q1, da_lam_k1, da_lam_q2, da_lam_k2, da_subln_g, gq_qnorm_g, gq_knorm_g, w_br_a, w_br_b, w_br_c, w_out, ln1_g, ln1_b, ffn_w_up, ffn_w_down, moe_w_router, moe_b_router, moe_w_up, moe_w_down, ln2_g, ln2_b):
    p = dict(w_ada=w_ada, b_ada=b_ada, w_in=w_in, b_in=b_in,
             hy_conv_w=hy_conv_w, hy_conv_b=hy_conv_b, hy_f_w1=hy_f_w1, hy_f_b1=hy_f_b1,
             hy_f_w2=hy_f_w2, hy_f_b2=hy_f_b2, hy_f_w3=hy_f_w3, hy_f_b3=hy_f_b3, hy_bias=hy_bias,
             da_lam_q1=da_lam_q1, da_lam_k1=da_lam_k1, da_lam_q2=da_lam_q2, da_lam_k2=da_lam_k2,
             da_subln_g=da_subln_g, gq_qnorm_g=gq_qnorm_g, gq_knorm_g=gq_knorm_g,
             w_br_a=w_br_a, w_br_b=w_br_b, w_br_c=w_br_c, w_out=w_out, ln1_g=ln1_g, ln1_b=ln1_b,
             ffn_w_up=ffn_w_up, ffn_w_down=ffn_w_down, moe_w_router=moe_w_router,
             moe_b_router=moe_b_router, moe_w_up=moe_w_up, moe_w_down=moe_w_down,
             ln2_g=ln2_g, ln2_b=ln2_b)
    assert x_prompt.shape[1] == x_sample.shape[1]
    prep = _prepare(x_prompt.shape[1], p)
    return (_trunk(x_prompt, c_prompt, p, prep), _trunk(x_sample, c_sample, p, prep))
```

```python
import functools
import math

import jax
import jax.numpy as jnp
from jax import lax
from jax.experimental import pallas as pl
from jax.experimental.pallas import tpu as pltpu

F32 = jnp.float32
BF16 = jnp.bfloat16

D_MODEL = 1024
DEPTH = 2
GRID_W = 64
HEAD_DIM = 64
HY_WIDTH = D_MODEL // 4
HY_ORDER = 2
SHORT_CONV = 3
N_BANDS = 16
FILTER_HID = 64
DECAY_TARGET = 1e-2
MIN_DECAY = math.log(DECAY_TARGET) / 1.5
MAX_DECAY = math.log(DECAY_TARGET) / 0.3
DECAY_SHIFT = 0.05
DA_HEADS = 4
DA_HALF = HEAD_DIM // 2
DA_WIDTH = DA_HEADS * HEAD_DIM
GQ_HEADS = 8
GQ_KV = 2
GQ_WIDTH = GQ_HEADS * HEAD_DIM
ROPE_AXIS = HEAD_DIM // 2
ROPE_THETA = 10000.0
HY_COLS = (HY_ORDER + 1) * HY_WIDTH
DA_COLS = 3 * DA_WIDTH
GQ_COLS = GQ_WIDTH + 2 * GQ_KV * HEAD_DIM
GATE_COLS = 3 * D_MODEL
IN_COLS = HY_COLS + DA_COLS + GQ_COLS + GATE_COLS
D_FF = 256 * ((8 * D_MODEL // 3 + 255) // 256)
N_EXPERTS = 8
MOE_FF = D_FF // 2
DEEPNORM_ALPHA = (2.0 * DEPTH) ** 0.25
EPS = 1e-5

LANES = 128
VMEM_LIMIT = 56 * 1024 * 1024

COL_GATE = 0
COL_HY = GATE_COLS
COL_DA = COL_HY + HY_COLS
COL_GQ = COL_DA + DA_COLS


def _params(sem):
    return pltpu.CompilerParams(dimension_semantics=sem, vmem_limit_bytes=VMEM_LIMIT)


def _dot(a, b):
    return jnp.dot(a, b, preferred_element_type=F32)


def _dot_nt(a, b):
    return lax.dot_general(a, b, (((1,), (1,)), ((), ())), preferred_element_type=F32)


def _split(x):
    hi = x.astype(BF16)
    lo = (x - hi.astype(F32)).astype(BF16)
    return hi, lo


def _dot3(a, b):
    ah, al = _split(a)
    bh, bl = _split(b)
    return _dot(ah, bh) + (_dot(ah, bl) + _dot(al, bh))


def _sigmoid(x):
    return 0.5 * jnp.tanh(0.5 * x) + 0.5


def _ln(x):
    mu = jnp.mean(x, axis=-1, keepdims=True)
    xc = x - mu
    var = jnp.mean(xc * xc, axis=-1, keepdims=True)
    return xc * lax.rsqrt(var + EPS)


def _head_ssq(x):
    w = x.shape[1]
    r = lax.broadcasted_iota(jnp.int32, (w, w), 0) // HEAD_DIM
    c = lax.broadcasted_iota(jnp.int32, (w, w), 1) // HEAD_DIM
    ones = jnp.where(r == c, 1.0, 0.0).astype(BF16)
    hi, lo = _split(x * x)
    return _dot(hi, ones) + _dot(lo, ones)


def _ada_kernel(c_ref, w_ref, b_ref, o_ref):
    c = c_ref[...]
    o_ref[...] = _dot3(c * _sigmoid(c), w_ref[...]) + b_ref[...]


def _ada(c, w_ada, b_ada):
    nb = c.shape[0]
    tn = 1536
    return pl.pallas_call(
        _ada_kernel,
        grid=(DEPTH, 6 * D_MODEL // tn),
        in_specs=[
            pl.BlockSpec((nb, D_MODEL), lambda l, j: (0, 0)),
            pl.BlockSpec((None, D_MODEL, tn), lambda l, j: (l, 0, j)),
            pl.BlockSpec((None, 1, tn), lambda l, j: (l, 0, j)),
        ],
        out_specs=pl.BlockSpec((None, nb, tn), lambda l, j: (l, 0, j)),
        out_shape=jax.ShapeDtypeStruct((DEPTH, nb, 6 * D_MODEL), F32),
        compiler_params=_params(("arbitrary", "arbitrary")),
        name="ada_mod",
    )(c, w_ada, b_ada.reshape(DEPTH, 1, 6 * D_MODEL))


LN_ROWS = 256
INPROJ_TN = 768
INPROJ_STEPS = IN_COLS // INPROJ_TN
HY_TILE = COL_HY // INPROJ_TN


def _inproj_kernel(x_ref, mod_ref, w_ref, b_ref, cw_ref, cb_ref, o_ref, ha_ref, hb_ref):
    b = pl.program_id(0)
    j = pl.program_id(1)
    seq = x_ref.shape[1]
    chunks = seq // LN_ROWS
    assert chunks <= INPROJ_STEPS + 1

    @pl.when((b == 0) & (j == 0))
    def _():
        hb_ref[...] = jnp.zeros_like(hb_ref)

    def step(h_ln, h_mm, ln_chunks, conv):
        shift = mod_ref[0, 0:1, :]
        scale = 1.0 + mod_ref[0, 1:2, :]
        for c in ln_chunks:
            r = pl.ds(pl.multiple_of(c * LN_ROWS, LN_ROWS), LN_ROWS)
            h_ln[r, :] = (_ln(x_ref[0, r, :]) * scale + shift).astype(BF16)
        z = _dot(h_mm[...], w_ref[...]) + b_ref[...]
        if conv:
            row = lax.broadcasted_iota(jnp.int32, z.shape, 0)
            prev = jnp.where(row == 0, 0.0, pltpu.roll(z, 1, 0))
            nxt = jnp.where(row == seq - 1, 0.0, pltpu.roll(z, seq - 1, 0))
            z = cb_ref[...] + prev * cw_ref[0:1, :] + z * cw_ref[1:2, :] + nxt * cw_ref[2:3, :]
        o_ref[0] = z.astype(BF16)

    first = [0] + ([INPROJ_STEPS] if chunks > INPROJ_STEPS else [])
    later = [jnp.minimum(j, chunks - 1)]
    for parity, (h_ln, h_mm) in enumerate(((ha_ref, hb_ref), (hb_ref, ha_ref))):
        even = b % 2 == parity
        pl.when(even & (j == 0))(functools.partial(step, h_ln, h_mm, first, False))
        pl.when(even & (j == HY_TILE))(functools.partial(step, h_ln, h_mm, later, True))
        pl.when(even & (j != 0) & (j != HY_TILE))(functools.partial(step, h_ln, h_mm, later, False))


def _inproj(x, mod, w, b, conv_w, conv_b):
    nb, seq, _ = x.shape
    tn = INPROJ_TN
    assert HY_COLS == tn and COL_HY % tn == 0 and HY_TILE != 0 and seq % LN_ROWS == 0
    cur = lambda i, j: (jnp.minimum(i, nb - 1), 0, 0)
    return pl.pallas_call(
        _inproj_kernel,
        grid=(nb + 1, INPROJ_STEPS),
        in_specs=[
            pl.BlockSpec((1, seq, D_MODEL), cur),
            pl.BlockSpec((1, 6, D_MODEL), cur),
            pl.BlockSpec((D_MODEL, tn), lambda i, j: (0, j)),
            pl.BlockSpec((1, tn), lambda i, j: (0, j)),
            pl.BlockSpec((SHORT_CONV, tn), lambda i, j: (0, 0)),
            pl.BlockSpec((1, tn), lambda i, j: (0, 0)),
        ],
        out_specs=pl.BlockSpec((1, seq, tn), lambda i, j: (jnp.maximum(i - 1, 0), 0, jnp.where(i == 0, 0, j))),
        out_shape=jax.ShapeDtypeStruct((nb, seq, IN_COLS), BF16),
        scratch_shapes=[pltpu.VMEM((seq, D_MODEL), BF16), pltpu.VMEM((seq, D_MODEL), BF16)],
        compiler_params=_params(("arbitrary", "arbitrary")),
        name="ln_inproj",
    )(x, mod, w, b, conv_w, conv_b)


FILT_ROWS = 512


def _filter_kernel(bands_ref, w1_ref, b1_ref, w2_ref, b2_ref, w3_ref, b3_ref, dec_ref, o_ref, ts_ref, *, seq):
    n = 2 * seq
    rows = min(FILT_ROWS, n)
    hw = HY_ORDER * HY_WIDTH
    lane = lax.broadcasted_iota(jnp.int32, (1, LANES), 1)

    def fill(i, asum):
        j0 = pl.multiple_of(i * rows, rows)
        j = j0 + lax.broadcasted_iota(jnp.int32, (rows, 1), 0)
        t = jnp.where(j < seq, j, n - j).astype(F32)
        t_norm = t / max(seq - 1, 1)
        ang = (2.0 * math.pi / seq) * t * bands_ref[...]
        feats = jnp.where(lane == 0, t_norm,
                          jnp.where(lane <= N_BANDS, jnp.cos(ang),
                                    jnp.where(lane <= 2 * N_BANDS, -jnp.sin(ang), 0.0)))
        h = jnp.sin(_dot3(feats, w1_ref[...]) + b1_ref[...])
        h = jnp.sin(_dot3(h, w2_ref[...]) + b2_ref[...])
        h = _dot3(h, w3_ref[...]) + b3_ref[...]
        window = jnp.exp(-t_norm * dec_ref[...]) + DECAY_SHIFT
        sel = jnp.where(j < seq, h[:, :hw], h[:, hw:]) * window
        sel = jnp.where(j == seq, 0.0, sel)
        ts_ref[pl.ds(j0, rows), :] = sel
        return asum + jnp.sum(jnp.abs(sel), axis=0, keepdims=True)

    asum = lax.fori_loop(0, n // rows, fill, jnp.zeros((1, hw), F32))
    inv = 1.0 / (asum + EPS)

    def norm(i, carry):
        r = pl.ds(pl.multiple_of(i * rows, rows), rows)
        o_ref[r, :] = (ts_ref[r, :] * inv).astype(BF16)
        return carry

    lax.fori_loop(0, n // rows, norm, 0)


def _filters_time(seq, bands, w1, b1, w2, b2, w3, b3, dec):
    hw = HY_ORDER * HY_WIDTH
    return pl.pallas_call(
        functools.partial(_filter_kernel, seq=seq),
        out_shape=jax.ShapeDtypeStruct((2 * seq, hw), BF16),
        scratch_shapes=[pltpu.VMEM((2 * seq, hw), F32)],
        compiler_params=pltpu.CompilerParams(vmem_limit_bytes=VMEM_LIMIT),
        name="hy_filter_time",
    )(bands, w1, b1, w2, b2, w3, b3, dec)


def _matmul_kernel(a_ref, b_ref, o_ref):
    o_ref[...] = _dot(a_ref[...], b_ref[...])


def _filter_dft(wf, ts):
    n, hw = ts.shape
    tm = min(512, n)
    return pl.pallas_call(
        _matmul_kernel,
        grid=(n // tm,),
        in_specs=[pl.BlockSpec((tm, n), lambda i: (i, 0)), pl.BlockSpec((n, hw), lambda i: (0, 0))],
        out_specs=pl.BlockSpec((tm, hw), lambda i: (i, 0)),
        out_shape=jax.ShapeDtypeStruct((n, hw), F32),
        compiler_params=_params(("arbitrary",)),
        name="hy_filter_dft",
    )(wf, ts)


def _longconv_kernel(v_ref, g_ref, w_ref, wi_ref, kf_ref, bias_ref, o_ref, acc_ref):
    kt = pl.program_id(1)
    nb = v_ref.shape[0]
    tf = w_ref.shape[1]

    @pl.when(kt == 0)
    def _():
        acc_ref[...] = jnp.zeros_like(acc_ref)

    kr = kf_ref[0]
    ki = kf_ref[1]
    first = (lax.broadcasted_iota(jnp.int32, (tf, 1), 0) == 0) & (kt == 0)
    for n in range(nb):
        v = v_ref[n]
        ur = _dot(w_ref[0], v)
        ui = _dot(w_ref[1], v)
        gr = ur * kr - jnp.where(first, 0.0, ui * ki)
        gi = jnp.where(first, ui * ki, ur * ki + ui * kr)
        acc_ref[n] += _dot(wi_ref[0], gr.astype(BF16)) + _dot(wi_ref[1], gi.astype(BF16))

    @pl.when(kt == pl.num_programs(1) - 1)
    def _():
        for n in range(nb):
            vf = v_ref[n].astype(F32)
            y = (acc_ref[n] + vf * bias_ref[...]).astype(BF16)
            o_ref[n] = (g_ref[n] * y).astype(BF16)


def _longconv(v_arr, v_col, g_arr, g_col, w, wi, kf, order, bias):
    nbatch, seq, _ = v_arr.shape
    nb = math.gcd(nbatch, 4)
    tf = min(256, seq)
    c = HY_WIDTH
    return pl.pallas_call(
        _longconv_kernel,
        grid=(nbatch // nb, seq // tf),
        in_specs=[
            pl.BlockSpec((nb, seq, c), lambda i, k: (i, 0, v_col)),
            pl.BlockSpec((nb, seq, c), lambda i, k: (i, 0, g_col)),
            pl.BlockSpec((2, tf, seq), lambda i, k: (0, k, 0)),
            pl.BlockSpec((2, seq, tf), lambda i, k: (0, 0, k)),
            pl.BlockSpec((2, tf, c), lambda i, k: (0, k, order)),
            pl.BlockSpec((1, c), lambda i, k: (0, 0)),
        ],
        out_specs=pl.BlockSpec((nb, seq, c), lambda i, k: (i, 0, 0)),
        out_shape=jax.ShapeDtypeStruct((nbatch, seq, c), BF16),
        scratch_shapes=[pltpu.VMEM((nb, seq, c), F32)],
        compiler_params=_params(("arbitrary", "arbitrary")),
        name="hy_longconv",
    )(v_arr, g_arr, w, wi, kf, bias)


def _lane_mask(lo, width, n=LANES):
    lane = lax.broadcasted_iota(jnp.int32, (1, n), 1)
    return (lane >= lo) & (lane < lo + width)


ALIBI_SPLIT = 3
POS_RADIX = 256


def _da_coef(head):
    return 2.0 ** (-8.0 * (head + 1) / DA_HEADS) * math.log2(math.e)


def _alibi_tables(seq):
    pos = jnp.arange(seq, dtype=jnp.int32)
    hi = ((pos // POS_RADIX) * POS_RADIX).astype(F32)[:, None]
    lo = (pos % POS_RADIX).astype(F32)[:, None]
    ones = jnp.ones((seq, 1), F32)
    fq, fk = [], []
    for head in range(DA_HEADS):
        rest = jnp.float32(_da_coef(head))
        pieces = []
        for _ in range(ALIBI_SPLIT):
            piece = rest.astype(BF16).astype(F32)
            pieces.append(piece)
            rest = rest - piece
        cq = jnp.concatenate([ones * c for c in pieces], axis=1)
        fq.append(jnp.concatenate([hi] * ALIBI_SPLIT + [lo] * ALIBI_SPLIT + [cq, cq], axis=1))
        fk.append(jnp.concatenate([-cq, -cq] + [hi] * ALIBI_SPLIT + [lo] * ALIBI_SPLIT, axis=1))
    pad = lambda t: jnp.pad(t, ((0, 0), (0, 0), (0, LANES - 4 * ALIBI_SPLIT))).astype(BF16)
    fk = jnp.stack(fk)
    tq = min(DA_TQ, seq)
    key_tile = (pos // tq)[None, :, None]
    signed = [pad(fk * jnp.where(key_tile < qi, 1.0, jnp.where(key_tile > qi, -1.0, 0.0))) for qi in range(seq // tq)]
    return pad(jnp.stack(fq)), signed


def _da_kernel(q_ref, k_ref, v_ref, fq_ref, fk_ref, lam_ref, g_ref, o_ref, *, qi, lam_init):
    tq = q_ref.shape[1]
    seq = k_ref.shape[1]
    lv = lam_ref[...]
    lam = (jnp.exp(jnp.sum(lv[0:1] * lv[1:2], axis=1, keepdims=True))
           - jnp.exp(jnp.sum(lv[2:3] * lv[3:4], axis=1, keepdims=True)) + lam_init)
    local = jnp.abs(lax.broadcasted_iota(jnp.int32, (tq, tq), 0)
                    - lax.broadcasted_iota(jnp.int32, (tq, tq), 1)).astype(F32)
    slabs = []
    for s in range(DA_WIDTH // LANES):
        cols = slice(s * LANES, (s + 1) * LANES)
        q_slab = q_ref[0, :, cols]
        k_slab = k_ref[0, :, cols]
        v_slab = v_ref[0, :, cols]
        o_slab = jnp.zeros((tq, LANES), F32)
        for hh in range(LANES // HEAD_DIM):
            head = s * (LANES // HEAD_DIM) + hh
            k_aug = jnp.concatenate([k_slab, fk_ref[head]], axis=1)
            mine = _lane_mask(hh * HEAD_DIM, HEAD_DIM)
            v_ones = jnp.where(mine, v_slab, jnp.ones_like(v_slab))
            maps = []
            for c in range(2):
                keep = jnp.where(_lane_mask(hh * HEAD_DIM + c * DA_HALF, DA_HALF), 1.0, 0.0).astype(BF16)
                q_aug = jnp.concatenate([q_slab * keep, fq_ref[head]], axis=1)
                sc = _dot_nt(q_aug, k_aug)
                pieces = [sc[:, qi * tq:(qi + 1) * tq] - _da_coef(head) * local]
                if qi > 0:
                    pieces.insert(0, sc[:, :qi * tq])
                if (qi + 1) * tq < seq:
                    pieces.append(sc[:, (qi + 1) * tq:])
                sc = jnp.concatenate(pieces, axis=1)
                p = jnp.exp2(sc - jnp.max(sc, axis=1, keepdims=True))
                o2 = _dot(p.astype(BF16), v_ones)
                maps.append((o2, o2[:, (1 - hh) * HEAD_DIM:(1 - hh) * HEAD_DIM + 1]))
            diff = maps[0][0] * (1.0 / maps[0][1]) - maps[1][0] * (lam / maps[1][1])
            o_slab = jnp.where(mine, diff, o_slab)
        slabs.append(o_slab)
    o = jnp.concatenate(slabs, axis=1)
    y = o * lax.rsqrt(_head_ssq(o) * (1.0 / HEAD_DIM) + EPS) * g_ref[...]
    o_ref[0] = (y * (1.0 - lam_init)).astype(BF16)


DA_TQ = 512


def _diff_attn(z, fq, fk_signed, lam_vecs, g_tiled, lam_init):
    nb, seq, _ = z.shape
    tq = min(DA_TQ, seq)
    cb = COL_DA // DA_WIDTH
    tiles = []
    for qi in range(seq // tq):
        tiles.append(pl.pallas_call(
            functools.partial(_da_kernel, qi=qi, lam_init=lam_init),
            grid=(nb,),
            in_specs=[
                pl.BlockSpec((1, tq, DA_WIDTH), lambda b, qi=qi: (b, qi, cb)),
                pl.BlockSpec((1, seq, DA_WIDTH), lambda b: (b, 0, cb + 1)),
                pl.BlockSpec((1, seq, DA_WIDTH), lambda b: (b, 0, cb + 2)),
                pl.BlockSpec((DA_HEADS, tq, LANES), lambda b, qi=qi: (0, qi, 0)),
                pl.BlockSpec((DA_HEADS, seq, LANES), lambda b: (0, 0, 0)),
                pl.BlockSpec((4, DA_HALF), lambda b: (0, 0)),
                pl.BlockSpec((1, DA_WIDTH), lambda b: (0, 0)),
            ],
            out_specs=pl.BlockSpec((1, tq, DA_WIDTH), lambda b: (b, 0, 0)),
            out_shape=jax.ShapeDtypeStruct((nb, tq, DA_WIDTH), BF16),
            compiler_params=_params(("arbitrary",)),
            name="diff_attn",
        )(z, z, z, fq, fk_signed[qi], lam_vecs, g_tiled))
    return jnp.concatenate(tiles, axis=1)


def _rope(x, cos, sin):
    w = x.shape[1]
    lane = lax.broadcasted_iota(jnp.int32, (1, w), 1)
    half = ROPE_AXIS // 2
    partner = jnp.where(lane % ROPE_AXIS < half, pltpu.roll(x, w - half, 1), pltpu.roll(x, half, 1))
    return x * cos + partner * sin


def _dup_heads(x):
    first = _lane_mask(0, HEAD_DIM)
    a = jnp.where(first, x, 0.0)
    b = jnp.where(first, 0.0, x)
    return jnp.concatenate([a + pltpu.roll(a, HEAD_DIM, 1), b + pltpu.roll(b, HEAD_DIM, 1)], axis=1)


def _with_ones(v):
    first = _lane_mask(0, HEAD_DIM)
    swapped = pltpu.roll(v, HEAD_DIM, 1)
    return jnp.concatenate([jnp.where(first, v, 1.0), jnp.where(first, 1.0, swapped),
                            jnp.where(first, swapped, 1.0), jnp.where(first, 1.0, v)], axis=1)


def _gqa_prep_kernel(q_ref, kv_ref, cos_ref, sin_ref, gq_ref, gk_ref, qo_ref, ko_ref, vo_ref):
    kvw = GQ_KV * HEAD_DIM
    q = q_ref[0].astype(F32)
    qn = q * lax.rsqrt(_head_ssq(q) * (1.0 / HEAD_DIM) + EPS) * gq_ref[...]
    qo_ref[0] = (_rope(qn, cos_ref[...], sin_ref[...]) * (HEAD_DIM ** -0.5 * math.log2(math.e))).astype(BF16)
    k = kv_ref[0, :, :kvw].astype(F32)
    kn = k * lax.rsqrt(_head_ssq(k) * (1.0 / HEAD_DIM) + EPS) * gk_ref[...]
    kr = _rope(kn, cos_ref[:, :kvw], sin_ref[:, :kvw])
    ko_ref[0] = _dup_heads(kr).astype(BF16)
    vo_ref[0] = _with_ones(kv_ref[0, :, kvw:].astype(F32)).astype(BF16)


def _gqa_prep(z, cos, sin, gq, gk):
    nb, seq, _ = z.shape
    tl = min(512, seq)
    kvw = GQ_KV * HEAD_DIM
    return pl.pallas_call(
        _gqa_prep_kernel,
        grid=(nb, seq // tl),
        in_specs=[
            pl.BlockSpec((1, tl, GQ_WIDTH), lambda b, i: (b, i, COL_GQ // GQ_WIDTH)),
            pl.BlockSpec((1, tl, 2 * kvw), lambda b, i: (b, i, (COL_GQ + GQ_WIDTH) // (2 * kvw))),
            pl.BlockSpec((tl, GQ_WIDTH), lambda b, i: (i, 0)),
            pl.BlockSpec((tl, GQ_WIDTH), lambda b, i: (i, 0)),
            pl.BlockSpec((1, GQ_WIDTH), lambda b, i: (0, 0)),
            pl.BlockSpec((1, kvw), lambda b, i: (0, 0)),
        ],
        out_specs=[
            pl.BlockSpec((1, tl, GQ_WIDTH), lambda b, i: (b, i, 0)),
            pl.BlockSpec((1, tl, 2 * kvw), lambda b, i: (b, i, 0)),
            pl.BlockSpec((1, tl, 4 * kvw), lambda b, i: (b, i, 0)),
        ],
        out_shape=[
            jax.ShapeDtypeStruct((nb, seq, GQ_WIDTH), BF16),
            jax.ShapeDtypeStruct((nb, seq, 2 * kvw), BF16),
            jax.ShapeDtypeStruct((nb, seq, 4 * kvw), BF16),
        ],
        compiler_params=_params(("arbitrary", "arbitrary")),
        name="gqa_prep",
    )(z, z, cos, sin, gq, gk)


def _gqa_kernel(q_ref, k_ref, v_ref, o_ref):
    tq = q_ref.shape[1]
    per_slab = LANES // HEAD_DIM
    group = GQ_HEADS // GQ_KV
    slabs = []
    for s in range(GQ_WIDTH // LANES):
        q_slab = q_ref[0, :, s * LANES:(s + 1) * LANES]
        kv = (s * per_slab) // group
        k_dup = k_ref[0, :, kv * LANES:(kv + 1) * LANES]
        o_slab = jnp.zeros((tq, LANES), F32)
        for hh in range(per_slab):
            keep = jnp.where(_lane_mask(hh * HEAD_DIM, HEAD_DIM), 1.0, 0.0).astype(BF16)
            sc = _dot_nt(q_slab * keep, k_dup)
            p = jnp.exp2(sc - jnp.max(sc, axis=1, keepdims=True))
            v_ones = v_ref[0, :, (kv * per_slab + hh) * LANES:(kv * per_slab + hh + 1) * LANES]
            o2 = _dot(p.astype(BF16), v_ones)
            sums = o2[:, (1 - hh) * HEAD_DIM:(1 - hh) * HEAD_DIM + 1]
            o_slab = jnp.where(_lane_mask(hh * HEAD_DIM, HEAD_DIM), o2 * (1.0 / sums), o_slab)
        slabs.append(o_slab)
    o_ref[0] = jnp.concatenate(slabs, axis=1).astype(BF16)


def _gqa_attn(q, k, v):
    nb, seq, _ = q.shape
    tq = min(512, seq)
    kvw = 2 * GQ_KV * HEAD_DIM
    return pl.pallas_call(
        _gqa_kernel,
        grid=(nb, seq // tq),
        in_specs=[
            pl.BlockSpec((1, tq, GQ_WIDTH), lambda b, i: (b, i, 0)),
            pl.BlockSpec((1, seq, kvw), lambda b, i: (b, 0, 0)),
            pl.BlockSpec((1, seq, 2 * kvw), lambda b, i: (b, 0, 0)),
        ],
        out_specs=pl.BlockSpec((1, tq, GQ_WIDTH), lambda b, i: (b, i, 0)),
        out_shape=jax.ShapeDtypeStruct((nb, seq, GQ_WIDTH), BF16),
        compiler_params=_params(("arbitrary", "arbitrary")),
        name="gqa_attn",
    )(q, k, v)


def _merge_kernel(ya_ref, yb_ref, yc_ref, zg_ref, x_ref, mod_ref, wa_ref, wb_ref, wc_ref, wo_ref,
                  lg_ref, lb_ref, *rest):
    d = D_MODEL
    m = _sigmoid(zg_ref[0, :, 0:d].astype(F32)) * _dot(ya_ref[0], wa_ref[...])
    m = m + _sigmoid(zg_ref[0, :, d:2 * d].astype(F32)) * _dot(yb_ref[0], wb_ref[...])
    m = m + _sigmoid(zg_ref[0, :, 2 * d:3 * d].astype(F32)) * _dot(yc_ref[0], wc_ref[...])
    y = _dot(m.astype(BF16), wo_ref[...])
    gate1 = 1.0 + mod_ref[0, 2:3, :]
    xn = _ln(DEEPNORM_ALPHA * x_ref[0] + gate1 * y) * lg_ref[...] + lb_ref[...]
    h = _ln(xn) * (1.0 + mod_ref[0, 4:5, :]) + mod_ref[0, 3:4, :]
    if len(rest) == 2:
        xo_ref, h_ref = rest
    else:
        wr_ref, br_ref, xo_ref, h_ref, comb_ref = rest
        lane = lax.broadcasted_iota(jnp.int32, (1, LANES), 1).astype(F32)
        neg = -jnp.inf
        logits = jnp.where(lane < N_EXPERTS, _dot3(h, wr_ref[...]) + br_ref[...], neg)
        v1 = jnp.max(logits, axis=1, keepdims=True)
        i1 = jnp.min(jnp.where(logits == v1, lane, float(LANES)), axis=1, keepdims=True)
        others = jnp.where(lane == i1, neg, logits)
        v2 = jnp.max(others, axis=1, keepdims=True)
        i2 = jnp.min(jnp.where(others == v2, lane, float(LANES)), axis=1, keepdims=True)
        e = jnp.exp(v2 - v1)
        w1 = 1.0 / (1.0 + e)
        comb_ref[0] = jnp.where(lane == i1, w1, 0.0) + jnp.where(lane == i2, e * w1, 0.0)
    xo_ref[0] = xn
    h_ref[0] = h.astype(BF16)


def _merge(ya, yb, yc, z, x, mod, wa, wb, wc, wo, lg, lb, router=None):
    nb, seq, d = x.shape
    tm = min(512, seq)
    full = lambda shape: pl.BlockSpec(shape, lambda b, i: (0,) * len(shape))
    row = lambda w, col=0: pl.BlockSpec((1, tm, w), lambda b, i: (b, i, col))
    in_specs = [
        row(HY_WIDTH), row(DA_WIDTH), row(GQ_WIDTH), row(GATE_COLS, COL_GATE // GATE_COLS), row(d),
        pl.BlockSpec((1, 6, d), lambda b, i: (b, 0, 0)),
        full((HY_WIDTH, d)), full((DA_WIDTH, d)), full((GQ_WIDTH, d)), full((d, d)),
        full((1, d)), full((1, d)),
    ]
    out_specs = [row(d), row(d)]
    out_shape = [jax.ShapeDtypeStruct((nb, seq, d), F32), jax.ShapeDtypeStruct((nb, seq, d), BF16)]
    args = [ya, yb, yc, z, x, mod, wa, wb, wc, wo, lg, lb]
    if router is not None:
        in_specs += [full((d, LANES)), full((1, LANES))]
        out_specs.append(row(LANES))
        out_shape.append(jax.ShapeDtypeStruct((nb, seq, LANES), F32))
        args += list(router)
    return pl.pallas_call(
        _merge_kernel,
        grid=(nb, seq // tm),
        in_specs=in_specs,
        out_specs=out_specs,
        out_shape=out_shape,
        compiler_params=_params(("arbitrary", "arbitrary")),
        name="merge_outproj",
    )(*args)


def _ffn_kernel(h_ref, wg_ref, wu_ref, wd_ref, x_ref, mod_ref, lg_ref, lb_ref, o_ref, acc_ref):
    j = pl.program_id(2)

    @pl.when(j == 0)
    def _():
        acc_ref[...] = jnp.zeros_like(acc_ref)

    h = h_ref[0]
    g = _dot(h, wg_ref[...])
    u = _dot(h, wu_ref[...])
    a = g * _sigmoid(g) * u
    acc_ref[...] += _dot(a.astype(BF16), wd_ref[...])

    @pl.when(j == pl.num_programs(2) - 1)
    def _():
        gate2 = 1.0 + mod_ref[0, 5:6, :]
        o_ref[0] = _ln(DEEPNORM_ALPHA * x_ref[0] + gate2 * acc_ref[...]) * lg_ref[...] + lb_ref[...]


def _ffn(h, w_up, w_down, x, mod, lg, lb):
    nb, seq, d = x.shape
    ff = w_down.shape[0]
    tm = min(1024, seq)
    tf = 256
    nf = ff // tf
    row = lambda w: pl.BlockSpec((1, tm, w), lambda b, i, j: (b, i, 0))
    vec = pl.BlockSpec((1, d), lambda b, i, j: (0, 0))
    return pl.pallas_call(
        _ffn_kernel,
        grid=(nb, seq // tm, nf),
        in_specs=[
            row(d),
            pl.BlockSpec((d, tf), lambda b, i, j: (0, j)),
            pl.BlockSpec((d, tf), lambda b, i, j: (0, nf + j)),
            pl.BlockSpec((tf, d), lambda b, i, j: (j, 0)),
            row(d),
            pl.BlockSpec((1, 6, d), lambda b, i, j: (b, 0, 0)),
            vec, vec,
        ],
        out_specs=row(d),
        out_shape=jax.ShapeDtypeStruct((nb, seq, d), F32),
        scratch_shapes=[pltpu.VMEM((tm, d), F32)],
        compiler_params=_params(("arbitrary",) * 3),
        name="swiglu_dense",
    )(h, w_up, w_up, w_down, x, mod, lg, lb)


MOE_TOKENS = 1024
MOE_CHUNK = 320
MOE_SLOT = 384


def _moe_kernel(h_ref, comb_ref, wg_ref, wu_ref, wd_ref, x_ref, mod_ref, lg_ref, lb_ref, o_ref,
                posc_ref, posr_ref, cnt_ref, ys_ref):
    e = pl.program_id(2)
    tm = h_ref.shape[1]
    n_exp = ys_ref.shape[0] // MOE_SLOT
    lane = lax.broadcasted_iota(jnp.int32, (1, LANES), 1)

    @pl.when(e == 0)
    def _():
        comb = comb_ref[0]
        sel = jnp.where(comb > 0.0, 1.0, 0.0)
        r = lax.broadcasted_iota(jnp.int32, (tm, tm), 0)
        c = lax.broadcasted_iota(jnp.int32, (tm, tm), 1)
        before = jnp.where(c < r, 1.0, 0.0).astype(BF16)
        rank = _dot(before, sel.astype(BF16))
        posc = jnp.where(comb > 0.0, rank, -1.0)
        posc_ref[...] = posc
        posr_ref[...] = posc.T
        cnt_ref[...] = jnp.sum(sel, axis=0, keepdims=True)
        o_ref[0] = jnp.zeros(o_ref.shape[1:], F32)
        ys_ref[...] = jnp.zeros_like(ys_ref)

    mine = lane == e
    pos_row = posr_ref[pl.ds(e, 1), :]
    count = jnp.sum(jnp.where(mine, cnt_ref[...], 0.0))

    def expert_rows(c):
        slot_r = (c * MOE_CHUNK + lax.broadcasted_iota(jnp.int32, (MOE_CHUNK, 1), 0)).astype(F32)
        gather = jnp.where(pos_row == slot_r, 1.0, 0.0).astype(BF16)
        xs = _dot(gather, h_ref[0]).astype(BF16)
        g = _dot(xs, wg_ref[...])
        u = _dot(xs, wu_ref[...])
        a = (g * _sigmoid(g) * u).astype(BF16)
        return _dot(a, wd_ref[...]).astype(BF16)

    ys_ref[pl.ds(pl.multiple_of(e * MOE_SLOT, MOE_SLOT), MOE_CHUNK), :] = expert_rows(0)

    for c in range(1, -(-tm // MOE_CHUNK)):
        @pl.when(count > c * MOE_CHUNK)
        def _():
            pos_col = jnp.sum(jnp.where(mine, posc_ref[...], 0.0), axis=1, keepdims=True)
            w_col = jnp.sum(jnp.where(mine, comb_ref[0], 0.0), axis=1, keepdims=True)
            slot_c = (c * MOE_CHUNK + lax.broadcasted_iota(jnp.int32, (1, MOE_CHUNK), 1)).astype(F32)
            scatter = jnp.where(pos_col == slot_c, 1.0, 0.0).astype(BF16)
            o_ref[0] += w_col * _dot(scatter, expert_rows(c))

    @pl.when(e == pl.num_programs(2) - 1)
    def _():
        slot_c = lax.broadcasted_iota(jnp.int32, (1, MOE_SLOT), 1).astype(F32)
        slot_c = jnp.where(slot_c < MOE_CHUNK, slot_c, -2.0)
        f = o_ref[0]
        for e0 in range(0, n_exp, 2):
            weights = jnp.concatenate(
                [jnp.where(posc_ref[:, ee:ee + 1] == slot_c, comb_ref[0, :, ee:ee + 1], 0.0).astype(BF16)
                 for ee in (e0, e0 + 1)], axis=1)
            f = f + _dot(weights, ys_ref[e0 * MOE_SLOT:(e0 + 2) * MOE_SLOT, :])
        gate2 = 1.0 + mod_ref[0, 5:6, :]
        o_ref[0] = _ln(DEEPNORM_ALPHA * x_ref[0] + gate2 * f) * lg_ref[...] + lb_ref[...]


def _moe(h, comb, w_up, w_down, x, mod, lg, lb):
    nb, seq, d = x.shape
    ne, ff, _ = w_down.shape
    tm = min(MOE_TOKENS, seq)
    row = lambda w: pl.BlockSpec((1, tm, w), lambda b, i, e: (b, i, 0))
    vec = pl.BlockSpec((1, d), lambda b, i, e: (0, 0))
    return pl.pallas_call(
        _moe_kernel,
        grid=(nb, seq // tm, ne),
        in_specs=[
            row(d), row(LANES),
            pl.BlockSpec((None, d, ff), lambda b, i, e: (e, 0, 0)),
            pl.BlockSpec((None, d, ff), lambda b, i, e: (e, 0, 1)),
            pl.BlockSpec((None, ff, d), lambda b, i, e: (e, 0, 0)),
            row(d),
            pl.BlockSpec((1, 6, d), lambda b, i, e: (b, 0, 0)),
            vec, vec,
        ],
        out_specs=row(d),
        out_shape=jax.ShapeDtypeStruct((nb, seq, d), F32),
        scratch_shapes=[pltpu.VMEM((tm, LANES), F32), pltpu.VMEM((LANES, tm), F32), pltpu.VMEM((1, LANES), F32),
                        pltpu.VMEM((ne * MOE_SLOT, d), BF16)],
        compiler_params=_params(("arbitrary",) * 3),
        name="swiglu_routed",
    )(h, comb, w_up, w_up, w_down, x, mod, lg, lb)


def _dft_tables(seq):
    n = 2 * seq
    k = jnp.arange(seq, dtype=jnp.int32)[:, None]
    s = jnp.arange(n, dtype=jnp.int32)[None, :]
    ang = ((k * s) % n).astype(F32) * (2.0 * math.pi / n)
    cos = jnp.cos(ang)
    msin = -jnp.sin(ang)
    alt = jnp.where(s % 2 == 0, 1.0, -1.0).astype(F32)
    msin = jnp.where(k == 0, alt, msin)
    fwd = jnp.stack([cos, msin])
    scale = jnp.where(k == 0, 1.0 / n, 2.0 / n).astype(F32)
    inv = jnp.stack([cos[:, :seq] * scale, msin[:, :seq] * scale]).transpose(0, 2, 1)
    return fwd.astype(BF16), inv.astype(BF16)


def _rope_tables(seq):
    t = jnp.arange(seq, dtype=jnp.int32)
    inv = ROPE_THETA ** (-jnp.arange(0, ROPE_AXIS, 2, dtype=F32) / ROPE_AXIS)
    ang_r = (t // GRID_W).astype(F32)[:, None] * inv[None, :]
    ang_c = (t % GRID_W).astype(F32)[:, None] * inv[None, :]
    cos = jnp.concatenate([jnp.cos(ang_r)] * 2 + [jnp.cos(ang_c)] * 2, axis=1)
    sin = jnp.concatenate([-jnp.sin(ang_r), jnp.sin(ang_r), -jnp.sin(ang_c), jnp.sin(ang_c)], axis=1)
    return jnp.tile(cos, (1, GQ_HEADS)), jnp.tile(sin, (1, GQ_HEADS))


def _pad_to(a, rows, cols):
    return jnp.pad(a, ((0, rows - a.shape[0]), (0, cols - a.shape[1])))


def _prepare(seq, p):
    fwd, inv = _dft_tables(seq)
    cos, sin = _rope_tables(seq)
    bands = jnp.linspace(1e-4, N_BANDS - 1, N_BANDS, dtype=F32)
    bands_row = jnp.zeros((1, LANES), F32).at[0, 1:1 + N_BANDS].set(bands).at[0, 1 + N_BANDS:1 + 2 * N_BANDS].set(bands)
    deltas = jnp.abs(jnp.linspace(MIN_DECAY, MAX_DECAY, HY_WIDTH, dtype=F32))
    dec = jnp.tile(deltas, HY_ORDER)[None, :]
    da_scale = DA_HALF ** -0.5 * math.log2(math.e)
    fq, fk = _alibi_tables(seq)
    layers = []
    for l in range(DEPTH):
        w_in, b_in = p['w_in'][l], p['b_in'][l]
        a0, b0, c0, g0 = 0, HY_COLS, HY_COLS + DA_COLS, HY_COLS + DA_COLS + GQ_COLS
        w_cols = [w_in[:, g0:], w_in[:, a0:b0], w_in[:, b0:b0 + DA_WIDTH] * da_scale, w_in[:, b0 + DA_WIDTH:c0], w_in[:, c0:g0]]
        b_cols = [b_in[g0:], b_in[a0:b0], b_in[b0:b0 + DA_WIDTH] * da_scale, b_in[b0 + DA_WIDTH:c0], b_in[c0:g0]]
        ts = _filters_time(
            seq, bands_row,
            _pad_to(p['hy_f_w1'][l], LANES, LANES), _pad_to(p['hy_f_b1'][l][None, :], 1, LANES),
            _pad_to(p['hy_f_w2'][l], LANES, LANES), _pad_to(p['hy_f_b2'][l][None, :], 1, LANES),
            _pad_to(p['hy_f_w3'][l], LANES, 2 * HY_ORDER * HY_WIDTH), p['hy_f_b3'][l][None, :], dec)
        kf = _filter_dft(fwd.reshape(2 * seq, 2 * seq), ts).reshape(2, seq, HY_ORDER * HY_WIDTH)
        lay = dict(
            w_in=jnp.concatenate(w_cols, axis=1).astype(BF16),
            b_in=jnp.concatenate(b_cols)[None, :],
            conv_w=p['hy_conv_w'][l], conv_b=p['hy_conv_b'][l][None, :],
            kf=kf, hy_bias=p['hy_bias'][l],
            lam=jnp.stack([p['da_lam_q1'][l], p['da_lam_k1'][l], p['da_lam_q2'][l], p['da_lam_k2'][l]]),
            lam_init=0.8 - 0.6 * math.exp(-0.3 * l),
            subln=jnp.tile(p['da_subln_g'][l], DA_HEADS)[None, :],
            gq=jnp.tile(p['gq_qnorm_g'][l], GQ_HEADS)[None, :],
            gk=jnp.tile(p['gq_knorm_g'][l], GQ_KV)[None, :],
            wa=p['w_br_a'][l].astype(BF16), wb=p['w_br_b'][l].astype(BF16), wc=p['w_br_c'][l].astype(BF16),
            wo=p['w_out'][l].astype(BF16),
            ln1_g=p['ln1_g'][l][None, :], ln1_b=p['ln1_b'][l][None, :],
            ln2_g=p['ln2_g'][l][None, :], ln2_b=p['ln2_b'][l][None, :],
        )
        if l % 2 == 0:
            lay.update(w_up=p['ffn_w_up'][l // 2].astype(BF16), w_down=p['ffn_w_down'][l // 2].astype(BF16))
        else:
            lay.update(w_up=p['moe_w_up'][l // 2].astype(BF16), w_down=p['moe_w_down'][l // 2].astype(BF16),
                       router=(_pad_to(p['moe_w_router'][l // 2], D_MODEL, LANES),
                               _pad_to(p['moe_b_router'][l // 2][None, :], 1, LANES)))
        layers.append(lay)
    return dict(fwd=fwd[:, :, :seq], inv=inv, cos=cos, sin=sin, fq=fq, fk=fk, layers=layers)


def _trunk(x, c, p, prep):
    mods = _ada(c, p['w_ada'], p['b_ada'])
    nb = x.shape[0]
    for l, lay in enumerate(prep['layers']):
        mod = mods[l].reshape(nb, 6, D_MODEL)
        z = _inproj(x, mod, lay['w_in'], lay['b_in'], lay['conv_w'], lay['conv_b'])
        hy = COL_HY // HY_WIDTH
        u = _longconv(z, hy + 2, z, hy, prep['fwd'], prep['inv'], lay['kf'], 0, lay['hy_bias'][0:1])
        ya = _longconv(u, 0, z, hy + 1, prep['fwd'], prep['inv'], lay['kf'], 1, lay['hy_bias'][1:2])
        yb = _diff_attn(z, prep['fq'], prep['fk'], lay['lam'], lay['subln'], lay['lam_init'])
        q, kd, vd = _gqa_prep(z, prep['cos'], prep['sin'], lay['gq'], lay['gk'])
        yc = _gqa_attn(q, kd, vd)
        merged = _merge(ya, yb, yc, z, x, mod, lay['wa'], lay['wb'], lay['wc'], lay['wo'],
                        lay['ln1_g'], lay['ln1_b'], lay.get('router'))
        if l % 2 == 0:
            x, h = merged
            x = _ffn(h, lay['w_up'], lay['w_down'], x, mod, lay['ln2_g'], lay['ln2_b'])
        else:
            x, h, comb = merged
            x = _moe(h, comb, lay['w_up'], lay['w_down'], x, mod, lay['ln2_g'], lay['ln2_b'])
    return x


def kernel(x_prompt, x_sample, c_prompt, c_sample, w_ada, b_ada, w_in, b_in, hy_conv_w, hy_conv_b, hy_f_w1, hy_f_b1, hy_f_w2, hy_f_b2, hy_f_w3, hy_f_b3, hy_bias, da_lam_q1, da_lam_k1, da_lam_q2, da_lam_k2, da_subln_g, gq_qnorm_g, gq_knorm_g, w_br_a, w_br_b, w_br_c, w_out, ln1_g, ln1_b, ffn_w_up, ffn_w_down, moe_w_router, moe_b_router, moe_w_up, moe_w_down, ln2_g, ln2_b):
    p = dict(w_ada=w_ada, b_ada=b_ada, w_in=w_in, b_in=b_in,
             hy_conv_w=hy_conv_w, hy_conv_b=hy_conv_b, hy_f_w1=hy_f_w1, hy_f_b1=hy_f_b1,
             hy_f_w2=hy_f_w2, hy_f_b2=hy_f_b2, hy_f_w3=hy_f_w3, hy_f_b3=hy_f_b3, hy_bias=hy_bias,
             da_lam_q1=da_lam_q1, da_lam_k1=da_lam_k1, da_lam_q2=da_lam_q2, da_lam_k2=da_lam_k2,
             da_subln_g=da_subln_g, gq_qnorm_g=gq_qnorm_g, gq_knorm_g=gq_knorm_g,
             w_br_a=w_br_a, w_br_b=w_br_b, w_br_c=w_br_c, w_out=w_out, ln1_g=ln1_g, ln1_b=ln1_b,
             ffn_w_up=ffn_w_up, ffn_w_down=ffn_w_down, moe_w_router=moe_w_router,
             moe_b_router=moe_b_router, moe_w_up=moe_w_up, moe_w_down=moe_w_down,
             ln2_g=ln2_g, ln2_b=ln2_b)
    assert x_prompt.shape[1] == x_sample.shape[1]
    prep = _prepare(x_prompt.shape[1], p)
    return (_trunk(x_prompt, c_prompt, p, prep), _trunk(x_sample, c_sample, p, prep))
```

```python
import functools
import math

import jax
import jax.numpy as jnp
from jax import lax
from jax.experimental import pallas as pl
from jax.experimental.pallas import tpu as pltpu

F32 = jnp.float32
BF16 = jnp.bfloat16

D_MODEL = 1024
DEPTH = 2
GRID_W = 64
HEAD_DIM = 64
HY_WIDTH = D_MODEL // 4
HY_ORDER = 2
SHORT_CONV = 3
N_BANDS = 16
FILTER_HID = 64
DECAY_TARGET = 1e-2
MIN_DECAY = math.log(DECAY_TARGET) / 1.5
MAX_DECAY = math.log(DECAY_TARGET) / 0.3
DECAY_SHIFT = 0.05
DA_HEADS = 4
DA_HALF = HEAD_DIM // 2
DA_WIDTH = DA_HEADS * HEAD_DIM
GQ_HEADS = 8
GQ_KV = 2
GQ_WIDTH = GQ_HEADS * HEAD_DIM
ROPE_AXIS = HEAD_DIM // 2
ROPE_THETA = 10000.0
HY_COLS = (HY_ORDER + 1) * HY_WIDTH
DA_COLS = 3 * DA_WIDTH
GQ_COLS = GQ_WIDTH + 2 * GQ_KV * HEAD_DIM
GATE_COLS = 3 * D_MODEL
IN_COLS = HY_COLS + DA_COLS + GQ_COLS + GATE_COLS
D_FF = 256 * ((8 * D_MODEL // 3 + 255) // 256)
N_EXPERTS = 8
MOE_FF = D_FF // 2
DEEPNORM_ALPHA = (2.0 * DEPTH) ** 0.25
EPS = 1e-5

LANES = 128
VMEM_LIMIT = 56 * 1024 * 1024

COL_GATE = 0
COL_HY = GATE_COLS
COL_DA = COL_HY + HY_COLS
COL_GQ = COL_DA + DA_COLS


def _params(sem):
    return pltpu.CompilerParams(dimension_semantics=sem, vmem_limit_bytes=VMEM_LIMIT)


def _dot(a, b):
    return jnp.dot(a, b, preferred_element_type=F32)


def _dot_nt(a, b):
    return lax.dot_general(a, b, (((1,), (1,)), ((), ())), preferred_element_type=F32)


def _split(x):
    hi = x.astype(BF16)
    lo = (x - hi.astype(F32)).astype(BF16)
    return hi, lo


def _dot3(a, b):
    ah, al = _split(a)
    bh, bl = _split(b)
    return _dot(ah, bh) + (_dot(ah, bl) + _dot(al, bh))


def _sigmoid(x):
    return 0.5 * jnp.tanh(0.5 * x) + 0.5


def _ln(x):
    mu = jnp.mean(x, axis=-1, keepdims=True)
    xc = x - mu
    var = jnp.mean(xc * xc, axis=-1, keepdims=True)
    return xc * lax.rsqrt(var + EPS)


def _head_ssq(x):
    w = x.shape[1]
    r = lax.broadcasted_iota(jnp.int32, (w, w), 0) // HEAD_DIM
    c = lax.broadcasted_iota(jnp.int32, (w, w), 1) // HEAD_DIM
    ones = jnp.where(r == c, 1.0, 0.0).astype(BF16)
    hi, lo = _split(x * x)
    return _dot(hi, ones) + _dot(lo, ones)


def _ada_kernel(c_ref, w_ref, b_ref, o_ref):
    c = c_ref[...]
    o_ref[...] = _dot3(c * _sigmoid(c), w_ref[...]) + b_ref[...]


def _ada(c, w_ada, b_ada):
    nb = c.shape[0]
    tn = 1536
    return pl.pallas_call(
        _ada_kernel,
        grid=(DEPTH, 6 * D_MODEL // tn),
        in_specs=[
            pl.BlockSpec((nb, D_MODEL), lambda l, j: (0, 0)),
            pl.BlockSpec((None, D_MODEL, tn), lambda l, j: (l, 0, j)),
            pl.BlockSpec((None, 1, tn), lambda l, j: (l, 0, j)),
        ],
        out_specs=pl.BlockSpec((None, nb, tn), lambda l, j: (l, 0, j)),
        out_shape=jax.ShapeDtypeStruct((DEPTH, nb, 6 * D_MODEL), F32),
        compiler_params=_params(("arbitrary", "arbitrary")),
        name="ada_mod",
    )(c, w_ada, b_ada.reshape(DEPTH, 1, 6 * D_MODEL))


LN_ROWS = 256
INPROJ_TN = 768
INPROJ_STEPS = IN_COLS // INPROJ_TN
HY_TILE = COL_HY // INPROJ_TN


def _inproj_kernel(x_ref, mod_ref, w_ref, b_ref, cw_ref, cb_ref, o_ref, ha_ref, hb_ref):
    b = pl.program_id(0)
    j = pl.program_id(1)
    seq = x_ref.shape[1]
    chunks = seq // LN_ROWS
    assert chunks <= INPROJ_STEPS + 1

    @pl.when((b == 0) & (j == 0))
    def _():
        hb_ref[...] = jnp.zeros_like(hb_ref)

    def step(h_ln, h_mm, ln_chunks, conv):
        shift = mod_ref[0, 0:1, :]
        scale = 1.0 + mod_ref[0, 1:2, :]
        for c in ln_chunks:
            r = pl.ds(pl.multiple_of(c * LN_ROWS, LN_ROWS), LN_ROWS)
            h_ln[r, :] = (_ln(x_ref[0, r, :]) * scale + shift).astype(BF16)
        z = _dot(h_mm[...], w_ref[...]) + b_ref[...]
        if conv:
            row = lax.broadcasted_iota(jnp.int32, z.shape, 0)
            prev = jnp.where(row == 0, 0.0, pltpu.roll(z, 1, 0))
            nxt = jnp.where(row == seq - 1, 0.0, pltpu.roll(z, seq - 1, 0))
            z = cb_ref[...] + prev * cw_ref[0:1, :] + z * cw_ref[1:2, :] + nxt * cw_ref[2:3, :]
        o_ref[0] = z.astype(BF16)

    first = [0] + ([INPROJ_STEPS] if chunks > INPROJ_STEPS else [])
    later = [jnp.minimum(j, chunks - 1)]
    for parity, (h_ln, h_mm) in enumerate(((ha_ref, hb_ref), (hb_ref, ha_ref))):
        even = b % 2 == parity
        pl.when(even & (j == 0))(functools.partial(step, h_ln, h_mm, first, False))
        pl.when(even & (j == HY_TILE))(functools.partial(step, h_ln, h_mm, later, True))
        pl.when(even & (j != 0) & (j != HY_TILE))(functools.partial(step, h_ln, h_mm, later, False))


def _inproj(x, mod, w, b, conv_w, conv_b):
    nb, seq, _ = x.shape
    tn = INPROJ_TN
    assert HY_COLS == tn and COL_HY % tn == 0 and HY_TILE != 0 and seq % LN_ROWS == 0
    cur = lambda i, j: (jnp.minimum(i, nb - 1), 0, 0)
    return pl.pallas_call(
        _inproj_kernel,
        grid=(nb + 1, INPROJ_STEPS),
        in_specs=[
            pl.BlockSpec((1, seq, D_MODEL), cur),
            pl.BlockSpec((1, 6, D_MODEL), cur),
            pl.BlockSpec((D_MODEL, tn), lambda i, j: (0, j)),
            pl.BlockSpec((1, tn), lambda i, j: (0, j)),
            pl.BlockSpec((SHORT_CONV, tn), lambda i, j: (0, 0)),
            pl.BlockSpec((1, tn), lambda i, j: (0, 0)),
        ],
        out_specs=pl.BlockSpec((1, seq, tn), lambda i, j: (jnp.maximum(i - 1, 0), 0, jnp.where(i == 0, 0, j))),
        out_shape=jax.ShapeDtypeStruct((nb, seq, IN_COLS), BF16),
        scratch_shapes=[pltpu.VMEM((seq, D_MODEL), BF16), pltpu.VMEM((seq, D_MODEL), BF16)],
        compiler_params=_params(("arbitrary", "arbitrary")),
        name="ln_inproj",
    )(x, mod, w, b, conv_w, conv_b)


FILT_ROWS = 512


def _filter_kernel(bands_ref, w1_ref, b1_ref, w2_ref, b2_ref, w3_ref, b3_ref, dec_ref, o_ref, ts_ref, *, seq):
    n = 2 * seq
    rows = min(FILT_ROWS, n)
    hw = HY_ORDER * HY_WIDTH
    lane = lax.broadcasted_iota(jnp.int32, (1, LANES), 1)

    def fill(i, asum):
        j0 = pl.multiple_of(i * rows, rows)
        j = j0 + lax.broadcasted_iota(jnp.int32, (rows, 1), 0)
        t = jnp.where(j < seq, j, n - j).astype(F32)
        t_norm = t / max(seq - 1, 1)
        ang = (2.0 * math.pi / seq) * t * bands_ref[...]
        feats = jnp.where(lane == 0, t_norm,
                          jnp.where(lane <= N_BANDS, jnp.cos(ang),
                                    jnp.where(lane <= 2 * N_BANDS, -jnp.sin(ang), 0.0)))
        h = jnp.sin(_dot3(feats, w1_ref[...]) + b1_ref[...])
        h = jnp.sin(_dot3(h, w2_ref[...]) + b2_ref[...])
        h = _dot3(h, w3_ref[...]) + b3_ref[...]
        window = jnp.exp(-t_norm * dec_ref[...]) + DECAY_SHIFT
        sel = jnp.where(j < seq, h[:, :hw], h[:, hw:]) * window
        sel = jnp.where(j == seq, 0.0, sel)
        ts_ref[pl.ds(j0, rows), :] = sel
        return asum + jnp.sum(jnp.abs(sel), axis=0, keepdims=True)

    asum = lax.fori_loop(0, n // rows, fill, jnp.zeros((1, hw), F32))
    inv = 1.0 / (asum + EPS)

    def norm(i, carry):
        r = pl.ds(pl.multiple_of(i * rows, rows), rows)
        o_ref[r, :] = (ts_ref[r, :] * inv).astype(BF16)
        return carry

    lax.fori_loop(0, n // rows, norm, 0)


def _filters_time(seq, bands, w1, b1, w2, b2, w3, b3, dec):
    hw = HY_ORDER * HY_WIDTH
    return pl.pallas_call(
        functools.partial(_filter_kernel, seq=seq),
        out_shape=jax.ShapeDtypeStruct((2 * seq, hw), BF16),
        scratch_shapes=[pltpu.VMEM((2 * seq, hw), F32)],
        compiler_params=pltpu.CompilerParams(vmem_limit_bytes=VMEM_LIMIT),
        name="hy_filter_time",
    )(bands, w1, b1, w2, b2, w3, b3, dec)


def _matmul_kernel(a_ref, b_ref, o_ref):
    o_ref[...] = _dot(a_ref[...], b_ref[...])


def _filter_dft(wf, ts):
    n, hw = ts.shape
    tm = min(512, n)
    return pl.pallas_call(
        _matmul_kernel,
        grid=(n // tm,),
        in_specs=[pl.BlockSpec((tm, n), lambda i: (i, 0)), pl.BlockSpec((n, hw), lambda i: (0, 0))],
        out_specs=pl.BlockSpec((tm, hw), lambda i: (i, 0)),
        out_shape=jax.ShapeDtypeStruct((n, hw), F32),
        compiler_params=_params(("arbitrary",)),
        name="hy_filter_dft",
    )(wf, ts)


def _longconv_kernel(v_ref, g_ref, w_ref, wi_ref, kf_ref, bias_ref, o_ref, acc_ref):
    kt = pl.program_id(1)
    nb = v_ref.shape[0]
    tf = w_ref.shape[0] // 2

    @pl.when(kt == 0)
    def _():
        acc_ref[...] = jnp.zeros_like(acc_ref)

    kr = kf_ref[0]
    ki = kf_ref[1]
    first = (lax.broadcasted_iota(jnp.int32, (tf, 1), 0) == 0) & (kt == 0)
    for n in range(nb):
        u = _dot(w_ref[...], v_ref[n])
        ur, ui = u[:tf], u[tf:]
        gr = ur * kr - jnp.where(first, 0.0, ui * ki)
        gi = jnp.where(first, ui * ki, ur * ki + ui * kr)
        acc_ref[n] += _dot(wi_ref[...], jnp.concatenate([gr, gi], axis=0).astype(BF16))

    @pl.when(kt == pl.num_programs(1) - 1)
    def _():
        for n in range(nb):
            vf = v_ref[n].astype(F32)
            y = (acc_ref[n] + vf * bias_ref[...]).astype(BF16)
            o_ref[n] = (g_ref[n] * y).astype(BF16)


def _conv_tiles(fwd, inv):
    _, seq, _ = inv.shape
    tf = min(CONV_TF, seq)
    nk = seq // tf
    w = fwd[:, :, :seq].reshape(2, nk, tf, seq).transpose(1, 0, 2, 3).reshape(nk, 2 * tf, seq)
    wi = inv.reshape(2, seq, nk, tf).transpose(2, 1, 0, 3).reshape(nk, seq, 2 * tf)
    return w, wi


CONV_TF = 256


def _longconv(v_arr, v_col, g_arr, g_col, w, wi, kf, order, bias):
    nbatch, seq, _ = v_arr.shape
    nb = math.gcd(nbatch, 4)
    tf = min(CONV_TF, seq)
    c = HY_WIDTH
    return pl.pallas_call(
        _longconv_kernel,
        grid=(nbatch // nb, seq // tf),
        in_specs=[
            pl.BlockSpec((nb, seq, c), lambda i, k: (i, 0, v_col)),
            pl.BlockSpec((nb, seq, c), lambda i, k: (i, 0, g_col)),
            pl.BlockSpec((None, 2 * tf, seq), lambda i, k: (k, 0, 0)),
            pl.BlockSpec((None, seq, 2 * tf), lambda i, k: (k, 0, 0)),
            pl.BlockSpec((2, tf, c), lambda i, k: (0, k, order)),
            pl.BlockSpec((1, c), lambda i, k: (0, 0)),
        ],
        out_specs=pl.BlockSpec((nb, seq, c), lambda i, k: (i, 0, 0)),
        out_shape=jax.ShapeDtypeStruct((nbatch, seq, c), BF16),
        scratch_shapes=[pltpu.VMEM((nb, seq, c), F32)],
        compiler_params=_params(("arbitrary", "arbitrary")),
        name="hy_longconv",
    )(v_arr, g_arr, w, wi, kf, bias)


def _lane_mask(lo, width, n=LANES):
    lane = lax.broadcasted_iota(jnp.int32, (1, n), 1)
    return (lane >= lo) & (lane < lo + width)


ALIBI_SPLIT = 3
POS_RADIX = 256


def _da_coef(head):
    return 2.0 ** (-8.0 * (head + 1) / DA_HEADS) * math.log2(math.e)


def _alibi_tables(seq):
    pos = jnp.arange(seq, dtype=jnp.int32)
    hi = ((pos // POS_RADIX) * POS_RADIX).astype(F32)[:, None]
    lo = (pos % POS_RADIX).astype(F32)[:, None]
    ones = jnp.ones((seq, 1), F32)
    fq, fk = [], []
    for head in range(DA_HEADS):
        rest = jnp.float32(_da_coef(head))
        pieces = []
        for _ in range(ALIBI_SPLIT):
            piece = rest.astype(BF16).astype(F32)
            pieces.append(piece)
            rest = rest - piece
        cq = jnp.concatenate([ones * c for c in pieces], axis=1)
        fq.append(jnp.concatenate([hi] * ALIBI_SPLIT + [lo] * ALIBI_SPLIT + [cq, cq], axis=1))
        fk.append(jnp.concatenate([-cq, -cq] + [hi] * ALIBI_SPLIT + [lo] * ALIBI_SPLIT, axis=1))
    pad = lambda t: jnp.pad(t, ((0, 0), (0, 0), (0, LANES - 4 * ALIBI_SPLIT))).astype(BF16)
    fk = jnp.stack(fk)
    tq = min(DA_TQ, seq)
    key_tile = (pos // tq)[None, :, None]
    signed = [pad(fk * jnp.where(key_tile < qi, 1.0, jnp.where(key_tile > qi, -1.0, 0.0))) for qi in range(seq // tq)]
    return pad(jnp.stack(fq)), signed


def _da_kernel(q_ref, k_ref, v_ref, fq_ref, fk_ref, lam_ref, g_ref, o_ref, *, qi, lam_init):
    tq = q_ref.shape[1]
    seq = k_ref.shape[1]
    lv = lam_ref[...]
    lam = (jnp.exp(jnp.sum(lv[0:1] * lv[1:2], axis=1, keepdims=True))
           - jnp.exp(jnp.sum(lv[2:3] * lv[3:4], axis=1, keepdims=True)) + lam_init)
    local = jnp.abs(lax.broadcasted_iota(jnp.int32, (tq, tq), 0)
                    - lax.broadcasted_iota(jnp.int32, (tq, tq), 1)).astype(F32)
    slabs = []
    for s in range(DA_WIDTH // LANES):
        cols = slice(s * LANES, (s + 1) * LANES)
        q_slab = q_ref[0, :, cols]
        k_slab = k_ref[0, :, cols]
        v_slab = v_ref[0, :, cols]
        o_slab = jnp.zeros((tq, LANES), F32)
        for hh in range(LANES // HEAD_DIM):
            head = s * (LANES // HEAD_DIM) + hh
            k_aug = jnp.concatenate([k_slab, fk_ref[head]], axis=1)
            mine = _lane_mask(hh * HEAD_DIM, HEAD_DIM)
            v_ones = jnp.where(mine, v_slab, jnp.ones_like(v_slab))
            maps = []
            for c in range(2):
                keep = jnp.where(_lane_mask(hh * HEAD_DIM + c * DA_HALF, DA_HALF), 1.0, 0.0).astype(BF16)
                q_aug = jnp.concatenate([q_slab * keep, fq_ref[head]], axis=1)
                sc = _dot_nt(q_aug, k_aug)
                pieces = [sc[:, qi * tq:(qi + 1) * tq] - _da_coef(head) * local]
                if qi > 0:
                    pieces.insert(0, sc[:, :qi * tq])
                if (qi + 1) * tq < seq:
                    pieces.append(sc[:, (qi + 1) * tq:])
                sc = jnp.concatenate(pieces, axis=1)
                p = jnp.exp2(sc - jnp.max(sc, axis=1, keepdims=True))
                o2 = _dot(p.astype(BF16), v_ones)
                maps.append((o2, o2[:, (1 - hh) * HEAD_DIM:(1 - hh) * HEAD_DIM + 1]))
            diff = maps[0][0] * (1.0 / maps[0][1]) - maps[1][0] * (lam / maps[1][1])
            o_slab = jnp.where(mine, diff, o_slab)
        slabs.append(o_slab)
    o = jnp.concatenate(slabs, axis=1)
    y = o * lax.rsqrt(_head_ssq(o) * (1.0 / HEAD_DIM) + EPS) * g_ref[...]
    o_ref[0] = (y * (1.0 - lam_init)).astype(BF16)


DA_TQ = 512


def _diff_attn(z, fq, fk_signed, lam_vecs, g_tiled, lam_init):
    nb, seq, _ = z.shape
    tq = min(DA_TQ, seq)
    cb = COL_DA // DA_WIDTH
    tiles = []
    for qi in range(seq // tq):
        tiles.append(pl.pallas_call(
            functools.partial(_da_kernel, qi=qi, lam_init=lam_init),
            grid=(nb,),
            in_specs=[
                pl.BlockSpec((1, tq, DA_WIDTH), lambda b, qi=qi: (b, qi, cb)),
                pl.BlockSpec((1, seq, DA_WIDTH), lambda b: (b, 0, cb + 1)),
                pl.BlockSpec((1, seq, DA_WIDTH), lambda b: (b, 0, cb + 2)),
                pl.BlockSpec((DA_HEADS, tq, LANES), lambda b, qi=qi: (0, qi, 0)),
                pl.BlockSpec((DA_HEADS, seq, LANES), lambda b: (0, 0, 0)),
                pl.BlockSpec((4, DA_HALF), lambda b: (0, 0)),
                pl.BlockSpec((1, DA_WIDTH), lambda b: (0, 0)),
            ],
            out_specs=pl.BlockSpec((1, tq, DA_WIDTH), lambda b: (b, 0, 0)),
            out_shape=jax.ShapeDtypeStruct((nb, tq, DA_WIDTH), BF16),
            compiler_params=_params(("arbitrary",)),
            name="diff_attn",
        )(z, z, z, fq, fk_signed[qi], lam_vecs, g_tiled))
    return jnp.concatenate(tiles, axis=1)


def _rope(x, cos, sin):
    w = x.shape[1]
    lane = lax.broadcasted_iota(jnp.int32, (1, w), 1)
    half = ROPE_AXIS // 2
    partner = jnp.where(lane % ROPE_AXIS < half, pltpu.roll(x, w - half, 1), pltpu.roll(x, half, 1))
    return x * cos + partner * sin


def _dup_heads(x):
    first = _lane_mask(0, HEAD_DIM)
    a = jnp.where(first, x, 0.0)
    b = jnp.where(first, 0.0, x)
    return jnp.concatenate([a + pltpu.roll(a, HEAD_DIM, 1), b + pltpu.roll(b, HEAD_DIM, 1)], axis=1)


def _with_ones(v):
    first = _lane_mask(0, HEAD_DIM)
    swapped = pltpu.roll(v, HEAD_DIM, 1)
    return jnp.concatenate([jnp.where(first, v, 1.0), jnp.where(first, 1.0, swapped),
                            jnp.where(first, swapped, 1.0), jnp.where(first, 1.0, v)], axis=1)


def _gqa_prep_kernel(q_ref, kv_ref, cos_ref, sin_ref, gq_ref, gk_ref, qo_ref, ko_ref, vo_ref):
    kvw = GQ_KV * HEAD_DIM
    q = q_ref[0].astype(F32)
    qn = q * lax.rsqrt(_head_ssq(q) * (1.0 / HEAD_DIM) + EPS) * gq_ref[...]
    qo_ref[0] = (_rope(qn, cos_ref[...], sin_ref[...]) * (HEAD_DIM ** -0.5 * math.log2(math.e))).astype(BF16)
    k = kv_ref[0, :, :kvw].astype(F32)
    kn = k * lax.rsqrt(_head_ssq(k) * (1.0 / HEAD_DIM) + EPS) * gk_ref[...]
    kr = _rope(kn, cos_ref[:, :kvw], sin_ref[:, :kvw])
    ko_ref[0] = _dup_heads(kr).astype(BF16)
    vo_ref[0] = _with_ones(kv_ref[0, :, kvw:].astype(F32)).astype(BF16)


def _gqa_prep(z, cos, sin, gq, gk):
    nb, seq, _ = z.shape
    tl = min(512, seq)
    kvw = GQ_KV * HEAD_DIM
    return pl.pallas_call(
        _gqa_prep_kernel,
        grid=(nb, seq // tl),
        in_specs=[
            pl.BlockSpec((1, tl, GQ_WIDTH), lambda b, i: (b, i, COL_GQ // GQ_WIDTH)),
            pl.BlockSpec((1, tl, 2 * kvw), lambda b, i: (b, i, (COL_GQ + GQ_WIDTH) // (2 * kvw))),
            pl.BlockSpec((tl, GQ_WIDTH), lambda b, i: (i, 0)),
            pl.BlockSpec((tl, GQ_WIDTH), lambda b, i: (i, 0)),
            pl.BlockSpec((1, GQ_WIDTH), lambda b, i: (0, 0)),
            pl.BlockSpec((1, kvw), lambda b, i: (0, 0)),
        ],
        out_specs=[
            pl.BlockSpec((1, tl, GQ_WIDTH), lambda b, i: (b, i, 0)),
            pl.BlockSpec((1, tl, 2 * kvw), lambda b, i: (b, i, 0)),
            pl.BlockSpec((1, tl, 4 * kvw), lambda b, i: (b, i, 0)),
        ],
        out_shape=[
            jax.ShapeDtypeStruct((nb, seq, GQ_WIDTH), BF16),
            jax.ShapeDtypeStruct((nb, seq, 2 * kvw), BF16),
            jax.ShapeDtypeStruct((nb, seq, 4 * kvw), BF16),
        ],
        compiler_params=_params(("arbitrary", "arbitrary")),
        name="gqa_prep",
    )(z, z, cos, sin, gq, gk)


def _gqa_kernel(q_ref, k_ref, v_ref, o_ref):
    tq = q_ref.shape[1]
    per_slab = LANES // HEAD_DIM
    group = GQ_HEADS // GQ_KV
    slabs = []
    for s in range(GQ_WIDTH // LANES):
        q_slab = q_ref[0, :, s * LANES:(s + 1) * LANES]
        kv = (s * per_slab) // group
        k_dup = k_ref[0, :, kv * LANES:(kv + 1) * LANES]
        o_slab = jnp.zeros((tq, LANES), F32)
        for hh in range(per_slab):
            keep = jnp.where(_lane_mask(hh * HEAD_DIM, HEAD_DIM), 1.0, 0.0).astype(BF16)
            sc = _dot_nt(q_slab * keep, k_dup)
            p = jnp.exp2(sc - jnp.max(sc, axis=1, keepdims=True))
            v_ones = v_ref[0, :, (kv * per_slab + hh) * LANES:(kv * per_slab + hh + 1) * LANES]
            o2 = _dot(p.astype(BF16), v_ones)
            sums = o2[:, (1 - hh) * HEAD_DIM:(1 - hh) * HEAD_DIM + 1]
            o_slab = jnp.where(_lane_mask(hh * HEAD_DIM, HEAD_DIM), o2 * (1.0 / sums), o_slab)
        slabs.append(o_slab)
    o_ref[0] = jnp.concatenate(slabs, axis=1).astype(BF16)


def _gqa_attn(q, k, v):
    nb, seq, _ = q.shape
    tq = min(512, seq)
    kvw = 2 * GQ_KV * HEAD_DIM
    return pl.pallas_call(
        _gqa_kernel,
        grid=(nb, seq // tq),
        in_specs=[
            pl.BlockSpec((1, tq, GQ_WIDTH), lambda b, i: (b, i, 0)),
            pl.BlockSpec((1, seq, kvw), lambda b, i: (b, 0, 0)),
            pl.BlockSpec((1, seq, 2 * kvw), lambda b, i: (b, 0, 0)),
        ],
        out_specs=pl.BlockSpec((1, tq, GQ_WIDTH), lambda b, i: (b, i, 0)),
        out_shape=jax.ShapeDtypeStruct((nb, seq, GQ_WIDTH), BF16),
        compiler_params=_params(("arbitrary", "arbitrary")),
        name="gqa_attn",
    )(q, k, v)


def _merge_kernel(ya_ref, yb_ref, yc_ref, zg_ref, x_ref, mod_ref, wa_ref, wb_ref, wc_ref, wo_ref,
                  lg_ref, lb_ref, *rest):
    d = D_MODEL
    m = (jnp.tanh(zg_ref[0, :, 0:d].astype(F32)) + 1.0) * _dot(ya_ref[0], wa_ref[...])
    m = m + (jnp.tanh(zg_ref[0, :, d:2 * d].astype(F32)) + 1.0) * _dot(yb_ref[0], wb_ref[...])
    m = m + (jnp.tanh(zg_ref[0, :, 2 * d:3 * d].astype(F32)) + 1.0) * _dot(yc_ref[0], wc_ref[...])
    y = _dot(m.astype(BF16), wo_ref[...])
    gate1 = 1.0 + mod_ref[0, 2:3, :]
    xn = _ln(DEEPNORM_ALPHA * x_ref[0] + gate1 * y) * lg_ref[...] + lb_ref[...]
    h = _ln(xn) * (1.0 + mod_ref[0, 4:5, :]) + mod_ref[0, 3:4, :]
    if len(rest) == 2:
        xo_ref, h_ref = rest
    else:
        wr_ref, br_ref, xo_ref, h_ref, comb_ref = rest
        lane = lax.broadcasted_iota(jnp.int32, (1, LANES), 1).astype(F32)
        neg = -jnp.inf
        logits = jnp.where(lane < N_EXPERTS, _dot3(h, wr_ref[...]) + br_ref[...], neg)
        v1 = jnp.max(logits, axis=1, keepdims=True)
        i1 = jnp.min(jnp.where(logits == v1, lane, float(LANES)), axis=1, keepdims=True)
        others = jnp.where(lane == i1, neg, logits)
        v2 = jnp.max(others, axis=1, keepdims=True)
        i2 = jnp.min(jnp.where(others == v2, lane, float(LANES)), axis=1, keepdims=True)
        e = jnp.exp(v2 - v1)
        w1 = 1.0 / (1.0 + e)
        comb_ref[0] = jnp.where(lane == i1, w1, 0.0) + jnp.where(lane == i2, e * w1, 0.0)
    xo_ref[0] = xn
    h_ref[0] = h.astype(BF16)


def _merge(ya, yb, yc, z, x, mod, wa, wb, wc, wo, lg, lb, router=None):
    nb, seq, d = x.shape
    tm = min(512, seq)
    full = lambda shape: pl.BlockSpec(shape, lambda b, i: (0,) * len(shape))
    row = lambda w, col=0: pl.BlockSpec((1, tm, w), lambda b, i: (b, i, col))
    in_specs = [
        row(HY_WIDTH), row(DA_WIDTH), row(GQ_WIDTH), row(GATE_COLS, COL_GATE // GATE_COLS), row(d),
        pl.BlockSpec((1, 6, d), lambda b, i: (b, 0, 0)),
        full((HY_WIDTH, d)), full((DA_WIDTH, d)), full((GQ_WIDTH, d)), full((d, d)),
        full((1, d)), full((1, d)),
    ]
    out_specs = [row(d), row(d)]
    out_shape = [jax.ShapeDtypeStruct((nb, seq, d), F32), jax.ShapeDtypeStruct((nb, seq, d), BF16)]
    args = [ya, yb, yc, z, x, mod, wa, wb, wc, wo, lg, lb]
    if router is not None:
        in_specs += [full((d, LANES)), full((1, LANES))]
        out_specs.append(row(LANES))
        out_shape.append(jax.ShapeDtypeStruct((nb, seq, LANES), F32))
        args += list(router)
    return pl.pallas_call(
        _merge_kernel,
        grid=(nb, seq // tm),
        in_specs=in_specs,
        out_specs=out_specs,
        out_shape=out_shape,
        compiler_params=_params(("arbitrary", "arbitrary")),
        name="merge_outproj",
    )(*args)


def _ffn_kernel(h_ref, wg_ref, wu_ref, wd_ref, x_ref, mod_ref, lg_ref, lb_ref, o_ref, acc_ref):
    j = pl.program_id(2)

    @pl.when(j == 0)
    def _():
        acc_ref[...] = jnp.zeros_like(acc_ref)

    h = h_ref[0]
    g = _dot(h, wg_ref[...])
    u = _dot(h, wu_ref[...])
    a = g * _sigmoid(g) * u
    acc_ref[...] += _dot(a.astype(BF16), wd_ref[...])

    @pl.when(j == pl.num_programs(2) - 1)
    def _():
        gate2 = 1.0 + mod_ref[0, 5:6, :]
        o_ref[0] = _ln(DEEPNORM_ALPHA * x_ref[0] + gate2 * acc_ref[...]) * lg_ref[...] + lb_ref[...]


def _ffn(h, w_up, w_down, x, mod, lg, lb):
    nb, seq, d = x.shape
    ff = w_down.shape[0]
    tm = min(1024, seq)
    tf = 256
    nf = ff // tf
    row = lambda w: pl.BlockSpec((1, tm, w), lambda b, i, j: (b, i, 0))
    vec = pl.BlockSpec((1, d), lambda b, i, j: (0, 0))
    return pl.pallas_call(
        _ffn_kernel,
        grid=(nb, seq // tm, nf),
        in_specs=[
            row(d),
            pl.BlockSpec((d, tf), lambda b, i, j: (0, j)),
            pl.BlockSpec((d, tf), lambda b, i, j: (0, nf + j)),
            pl.BlockSpec((tf, d), lambda b, i, j: (j, 0)),
            row(d),
            pl.BlockSpec((1, 6, d), lambda b, i, j: (b, 0, 0)),
            vec, vec,
        ],
        out_specs=row(d),
        out_shape=jax.ShapeDtypeStruct((nb, seq, d), F32),
        scratch_shapes=[pltpu.VMEM((tm, d), F32)],
        compiler_params=_params(("arbitrary",) * 3),
        name="swiglu_dense",
    )(h, w_up, w_up, w_down, x, mod, lg, lb)


MOE_TOKENS = 1024
MOE_CHUNK = 288
MOE_SLOT = 384


def _moe_kernel(h_ref, comb_ref, wg_ref, wu_ref, wd_ref, x_ref, mod_ref, lg_ref, lb_ref, o_ref,
                posc_ref, posr_ref, cnt_ref, ys_ref):
    e = pl.program_id(2)
    tm = h_ref.shape[1]
    n_exp = ys_ref.shape[0] // MOE_SLOT
    lane = lax.broadcasted_iota(jnp.int32, (1, LANES), 1)

    @pl.when(e == 0)
    def _():
        comb = comb_ref[0]
        sel = jnp.where(comb > 0.0, 1.0, 0.0)
        r = lax.broadcasted_iota(jnp.int32, (tm, tm), 0)
        c = lax.broadcasted_iota(jnp.int32, (tm, tm), 1)
        before = jnp.where(c < r, 1.0, 0.0).astype(BF16)
        rank = _dot(before, sel.astype(BF16))
        posc = jnp.where(comb > 0.0, rank, -1.0)
        posc_ref[...] = posc
        posr_ref[...] = posc.T
        cnt_ref[...] = jnp.sum(sel, axis=0, keepdims=True)
        o_ref[0] = jnp.zeros(o_ref.shape[1:], F32)
        ys_ref[...] = jnp.zeros_like(ys_ref)

    mine = lane == e
    pos_row = posr_ref[pl.ds(e, 1), :]
    count = jnp.sum(jnp.where(mine, cnt_ref[...], 0.0))

    def expert_rows(c):
        slot_r = (c * MOE_CHUNK + lax.broadcasted_iota(jnp.int32, (MOE_CHUNK, 1), 0)).astype(F32)
        gather = jnp.where(pos_row == slot_r, 1.0, 0.0).astype(BF16)
        xs = _dot(gather, h_ref[0]).astype(BF16)
        g = _dot(xs, wg_ref[...])
        u = _dot(xs, wu_ref[...])
        a = (g * _sigmoid(g) * u).astype(BF16)
        return _dot(a, wd_ref[...]).astype(BF16)

    ys_ref[pl.ds(pl.multiple_of(e * MOE_SLOT, MOE_SLOT), MOE_CHUNK), :] = expert_rows(0)

    for c in range(1, -(-tm // MOE_CHUNK)):
        @pl.when(count > c * MOE_CHUNK)
        def _():
            pos_col = jnp.sum(jnp.where(mine, posc_ref[...], 0.0), axis=1, keepdims=True)
            w_col = jnp.sum(jnp.where(mine, comb_ref[0], 0.0), axis=1, keepdims=True)
            slot_c = (c * MOE_CHUNK + lax.broadcasted_iota(jnp.int32, (1, MOE_CHUNK), 1)).astype(F32)
            scatter = jnp.where(pos_col == slot_c, 1.0, 0.0).astype(BF16)
            o_ref[0] += w_col * _dot(scatter, expert_rows(c))

    @pl.when(e == pl.num_programs(2) - 1)
    def _():
        slot_c = lax.broadcasted_iota(jnp.int32, (1, MOE_SLOT), 1).astype(F32)
        slot_c = jnp.where(slot_c < MOE_CHUNK, slot_c, -2.0)
        f = o_ref[0]
        for e0 in range(0, n_exp, 2):
            weights = jnp.concatenate(
                [jnp.where(posc_ref[:, ee:ee + 1] == slot_c, comb_ref[0, :, ee:ee + 1], 0.0).astype(BF16)
                 for ee in (e0, e0 + 1)], axis=1)
            f = f + _dot(weights, ys_ref[e0 * MOE_SLOT:(e0 + 2) * MOE_SLOT, :])
        gate2 = 1.0 + mod_ref[0, 5:6, :]
        o_ref[0] = _ln(DEEPNORM_ALPHA * x_ref[0] + gate2 * f) * lg_ref[...] + lb_ref[...]


def _moe(h, comb, w_up, w_down, x, mod, lg, lb):
    nb, seq, d = x.shape
    ne, ff, _ = w_down.shape
    tm = min(MOE_TOKENS, seq)
    row = lambda w: pl.BlockSpec((1, tm, w), lambda b, i, e: (b, i, 0))
    vec = pl.BlockSpec((1, d), lambda b, i, e: (0, 0))
    return pl.pallas_call(
        _moe_kernel,
        grid=(nb, seq // tm, ne),
        in_specs=[
            row(d), row(LANES),
            pl.BlockSpec((None, d, ff), lambda b, i, e: (e, 0, 0)),
            pl.BlockSpec((None, d, ff), lambda b, i, e: (e, 0, 1)),
            pl.BlockSpec((None, ff, d), lambda b, i, e: (e, 0, 0)),
            row(d),
            pl.BlockSpec((1, 6, d), lambda b, i, e: (b, 0, 0)),
            vec, vec,
        ],
        out_specs=row(d),
        out_shape=jax.ShapeDtypeStruct((nb, seq, d), F32),
        scratch_shapes=[pltpu.VMEM((tm, LANES), F32), pltpu.VMEM((LANES, tm), F32), pltpu.VMEM((1, LANES), F32),
                        pltpu.VMEM((ne * MOE_SLOT, d), BF16)],
        compiler_params=_params(("arbitrary",) * 3),
        name="swiglu_routed",
    )(h, comb, w_up, w_up, w_down, x, mod, lg, lb)


def _dft_tables(seq):
    n = 2 * seq
    k = jnp.arange(seq, dtype=jnp.int32)[:, None]
    s = jnp.arange(n, dtype=jnp.int32)[None, :]
    ang = ((k * s) % n).astype(F32) * (2.0 * math.pi / n)
    cos = jnp.cos(ang)
    msin = -jnp.sin(ang)
    alt = jnp.where(s % 2 == 0, 1.0, -1.0).astype(F32)
    msin = jnp.where(k == 0, alt, msin)
    fwd = jnp.stack([cos, msin])
    scale = jnp.where(k == 0, 1.0 / n, 2.0 / n).astype(F32)
    inv = jnp.stack([cos[:, :seq] * scale, msin[:, :seq] * scale]).transpose(0, 2, 1)
    return fwd.astype(BF16), inv.astype(BF16)


def _rope_tables(seq):
    t = jnp.arange(seq, dtype=jnp.int32)
    inv = ROPE_THETA ** (-jnp.arange(0, ROPE_AXIS, 2, dtype=F32) / ROPE_AXIS)
    ang_r = (t // GRID_W).astype(F32)[:, None] * inv[None, :]
    ang_c = (t % GRID_W).astype(F32)[:, None] * inv[None, :]
    cos = jnp.concatenate([jnp.cos(ang_r)] * 2 + [jnp.cos(ang_c)] * 2, axis=1)
    sin = jnp.concatenate([-jnp.sin(ang_r), jnp.sin(ang_r), -jnp.sin(ang_c), jnp.sin(ang_c)], axis=1)
    return jnp.tile(cos, (1, GQ_HEADS)), jnp.tile(sin, (1, GQ_HEADS))


def _pad_to(a, rows, cols):
    return jnp.pad(a, ((0, rows - a.shape[0]), (0, cols - a.shape[1])))


def _prepare(seq, p):
    fwd, inv = _dft_tables(seq)
    cos, sin = _rope_tables(seq)
    bands = jnp.linspace(1e-4, N_BANDS - 1, N_BANDS, dtype=F32)
    bands_row = jnp.zeros((1, LANES), F32).at[0, 1:1 + N_BANDS].set(bands).at[0, 1 + N_BANDS:1 + 2 * N_BANDS].set(bands)
    deltas = jnp.abs(jnp.linspace(MIN_DECAY, MAX_DECAY, HY_WIDTH, dtype=F32))
    dec = jnp.tile(deltas, HY_ORDER)[None, :]
    da_scale = DA_HALF ** -0.5 * math.log2(math.e)
    fq, fk = _alibi_tables(seq)
    layers = []
    for l in range(DEPTH):
        w_in, b_in = p['w_in'][l], p['b_in'][l]
        a0, b0, c0, g0 = 0, HY_COLS, HY_COLS + DA_COLS, HY_COLS + DA_COLS + GQ_COLS
        w_cols = [w_in[:, g0:] * 0.5, w_in[:, a0:b0], w_in[:, b0:b0 + DA_WIDTH] * da_scale, w_in[:, b0 + DA_WIDTH:c0], w_in[:, c0:g0]]
        b_cols = [b_in[g0:] * 0.5, b_in[a0:b0], b_in[b0:b0 + DA_WIDTH] * da_scale, b_in[b0 + DA_WIDTH:c0], b_in[c0:g0]]
        ts = _filters_time(
            seq, bands_row,
            _pad_to(p['hy_f_w1'][l], LANES, LANES), _pad_to(p['hy_f_b1'][l][None, :], 1, LANES),
            _pad_to(p['hy_f_w2'][l], LANES, LANES), _pad_to(p['hy_f_b2'][l][None, :], 1, LANES),
            _pad_to(p['hy_f_w3'][l], LANES, 2 * HY_ORDER * HY_WIDTH), p['hy_f_b3'][l][None, :], dec)
        kf = _filter_dft(fwd.reshape(2 * seq, 2 * seq), ts).reshape(2, seq, HY_ORDER * HY_WIDTH)
        lay = dict(
            w_in=jnp.concatenate(w_cols, axis=1).astype(BF16),
            b_in=jnp.concatenate(b_cols)[None, :],
            conv_w=p['hy_conv_w'][l], conv_b=p['hy_conv_b'][l][None, :],
            kf=kf, hy_bias=p['hy_bias'][l],
            lam=jnp.stack([p['da_lam_q1'][l], p['da_lam_k1'][l], p['da_lam_q2'][l], p['da_lam_k2'][l]]),
            lam_init=0.8 - 0.6 * math.exp(-0.3 * l),
            subln=jnp.tile(p['da_subln_g'][l], DA_HEADS)[None, :],
            gq=jnp.tile(p['gq_qnorm_g'][l], GQ_HEADS)[None, :],
            gk=jnp.tile(p['gq_knorm_g'][l], GQ_KV)[None, :],
            wa=(p['w_br_a'][l] * 0.5).astype(BF16), wb=(p['w_br_b'][l] * 0.5).astype(BF16),
            wc=(p['w_br_c'][l] * 0.5).astype(BF16),
            wo=p['w_out'][l].astype(BF16),
            ln1_g=p['ln1_g'][l][None, :], ln1_b=p['ln1_b'][l][None, :],
            ln2_g=p['ln2_g'][l][None, :], ln2_b=p['ln2_b'][l][None, :],
        )
        if l % 2 == 0:
            lay.update(w_up=p['ffn_w_up'][l // 2].astype(BF16), w_down=p['ffn_w_down'][l // 2].astype(BF16))
        else:
            lay.update(w_up=p['moe_w_up'][l // 2].astype(BF16), w_down=p['moe_w_down'][l // 2].astype(BF16),
                       router=(_pad_to(p['moe_w_router'][l // 2], D_MODEL, LANES),
                               _pad_to(p['moe_b_router'][l // 2][None, :], 1, LANES)))
        layers.append(lay)
    fwd, inv = _conv_tiles(fwd, inv)
    return dict(fwd=fwd, inv=inv, cos=cos, sin=sin, fq=fq, fk=fk, layers=layers)


def _trunk(x, c, p, prep):
    mods = _ada(c, p['w_ada'], p['b_ada'])
    nb = x.shape[0]
    for l, lay in enumerate(prep['layers']):
        mod = mods[l].reshape(nb, 6, D_MODEL)
        z = _inproj(x, mod, lay['w_in'], lay['b_in'], lay['conv_w'], lay['conv_b'])
        hy = COL_HY // HY_WIDTH
        u = _longconv(z, hy + 2, z, hy, prep['fwd'], prep['inv'], lay['kf'], 0, lay['hy_bias'][0:1])
        ya = _longconv(u, 0, z, hy + 1, prep['fwd'], prep['inv'], lay['kf'], 1, lay['hy_bias'][1:2])
        yb = _diff_attn(z, prep['fq'], prep['fk'], lay['lam'], lay['subln'], lay['lam_init'])
        q, kd, vd = _gqa_prep(z, prep['cos'], prep['sin'], lay['gq'], lay['gk'])
        yc = _gqa_attn(q, kd, vd)
        merged = _merge(ya, yb, yc, z, x, mod, lay['wa'], lay['wb'], lay['wc'], lay['wo'],
                        lay['ln1_g'], lay['ln1_b'], lay.get('router'))
        if l % 2 == 0:
            x, h = merged
            x = _ffn(h, lay['w_up'], lay['w_down'], x, mod, lay['ln2_g'], lay['ln2_b'])
        else:
            x, h, comb = merged
            x = _moe(h, comb, lay['w_up'], lay['w_down'], x, mod, lay['ln2_g'], lay['ln2_b'])
    return x


def kernel(x_prompt, x_sample, c_prompt, c_sample, w_ada, b_ada, w_in, b_in, hy_conv_w, hy_conv_b, hy_f_w1, hy_f_b1, hy_f_w2, hy_f_b2, hy_f_w3, hy_f_b3, hy_bias, da_lam_q1, da_lam_k1, da_lam_q2, da_lam_k2, da_subln_g, gq_qnorm_g, gq_knorm_g, w_br_a, w_br_b, w_br_c, w_out, ln1_g, ln1_b, ffn_w_up, ffn_w_down, moe_w_router, moe_b_router, moe_w_up, moe_w_down, ln2_g, ln2_b):
    p = dict(w_ada=w_ada, b_ada=b_ada, w_in=w_in, b_in=b_in,
             hy_conv_w=hy_conv_w, hy_conv_b=hy_conv_b, hy_f_w1=hy_f_w1, hy_f_b1=hy_f_b1,
             hy_f_w2=hy_f_w2, hy_f_b2=hy_f_b2, hy_f_w3=hy_f_w3, hy_f_b3=hy_f_b3, hy_bias=hy_bias,
             da_lam_q1=da_lam_q1, da_lam_k1=da_lam_k1, da_lam_q2=da_lam_q2, da_lam_k2=da_lam_k2,
             da_subln_g=da_subln_g, gq_qnorm_g=gq_qnorm_g, gq_knorm_g=gq_knorm_g,
             w_br_a=w_br_a, w_br_b=w_br_b, w_br_c=w_br_c, w_out=w_out, ln1_g=ln1_g, ln1_b=ln1_b,
             ffn_w_up=ffn_w_up, ffn_w_down=ffn_w_down, moe_w_router=moe_w_router,
             moe_b_router=moe_b_router, moe_w_up=moe_w_up, moe_w_down=moe_w_down,
             ln2_g=ln2_g, ln2_b=ln2_b)
    assert x_prompt.shape[1] == x_sample.shape[1]
    prep = _prepare(x_prompt.shape[1], p)
    return (_trunk(x_prompt, c_prompt, p, prep), _trunk(x_sample, c_sample, p, prep))
```

```python
import functools
import math

import jax
import jax.numpy as jnp
from jax import lax
from jax.experimental import pallas as pl
from jax.experimental.pallas import tpu as pltpu

F32 = jnp.float32
BF16 = jnp.bfloat16

D_MODEL = 1024
DEPTH = 2
GRID_W = 64
HEAD_DIM = 64
HY_WIDTH = D_MODEL // 4
HY_ORDER = 2
SHORT_CONV = 3
N_BANDS = 16
FILTER_HID = 64
DECAY_TARGET = 1e-2
MIN_DECAY = math.log(DECAY_TARGET) / 1.5
MAX_DECAY = math.log(DECAY_TARGET) / 0.3
DECAY_SHIFT = 0.05
DA_HEADS = 4
DA_HALF = HEAD_DIM // 2
DA_WIDTH = DA_HEADS * HEAD_DIM
GQ_HEADS = 8
GQ_KV = 2
GQ_WIDTH = GQ_HEADS * HEAD_DIM
ROPE_AXIS = HEAD_DIM // 2
ROPE_THETA = 10000.0
HY_COLS = (HY_ORDER + 1) * HY_WIDTH
DA_COLS = 3 * DA_WIDTH
GQ_COLS = GQ_WIDTH + 2 * GQ_KV * HEAD_DIM
GATE_COLS = 3 * D_MODEL
IN_COLS = HY_COLS + DA_COLS + GQ_COLS + GATE_COLS
D_FF = 256 * ((8 * D_MODEL // 3 + 255) // 256)
N_EXPERTS = 8
MOE_FF = D_FF // 2
DEEPNORM_ALPHA = (2.0 * DEPTH) ** 0.25
EPS = 1e-5

LANES = 128
VMEM_LIMIT = 56 * 1024 * 1024

COL_GATE = 0
COL_HY = GATE_COLS
COL_DA = COL_HY + HY_COLS
COL_GQ = COL_DA + DA_COLS


def _params(sem):
    return pltpu.CompilerParams(dimension_semantics=sem, vmem_limit_bytes=VMEM_LIMIT)


def _dot(a, b):
    return jnp.dot(a, b, preferred_element_type=F32)


def _dot_nt(a, b):
    return lax.dot_general(a, b, (((1,), (1,)), ((), ())), preferred_element_type=F32)


def _split(x):
    hi = x.astype(BF16)
    lo = (x - hi.astype(F32)).astype(BF16)
    return hi, lo


def _dot3(a, b):
    ah, al = _split(a)
    bh, bl = _split(b)
    return _dot(ah, bh) + (_dot(ah, bl) + _dot(al, bh))


def _sigmoid(x):
    return 0.5 * jnp.tanh(0.5 * x) + 0.5


def _ln(x):
    mu = jnp.mean(x, axis=-1, keepdims=True)
    xc = x - mu
    var = jnp.mean(xc * xc, axis=-1, keepdims=True)
    return xc * lax.rsqrt(var + EPS)


def _head_ssq(x):
    w = x.shape[1]
    r = lax.broadcasted_iota(jnp.int32, (w, w), 0) // HEAD_DIM
    c = lax.broadcasted_iota(jnp.int32, (w, w), 1) // HEAD_DIM
    ones = jnp.where(r == c, 1.0, 0.0).astype(BF16)
    hi, lo = _split(x * x)
    return _dot(hi, ones) + _dot(lo, ones)


def _ada_kernel(c_ref, w_ref, b_ref, o_ref):
    c = c_ref[...]
    o_ref[...] = _dot3(c * _sigmoid(c), w_ref[...]) + b_ref[...]


def _ada(c, w_ada, b_ada):
    nb = c.shape[0]
    tn = 1536
    return pl.pallas_call(
        _ada_kernel,
        grid=(DEPTH, 6 * D_MODEL // tn),
        in_specs=[
            pl.BlockSpec((nb, D_MODEL), lambda l, j: (0, 0)),
            pl.BlockSpec((None, D_MODEL, tn), lambda l, j: (l, 0, j)),
            pl.BlockSpec((None, 1, tn), lambda l, j: (l, 0, j)),
        ],
        out_specs=pl.BlockSpec((None, nb, tn), lambda l, j: (l, 0, j)),
        out_shape=jax.ShapeDtypeStruct((DEPTH, nb, 6 * D_MODEL), F32),
        compiler_params=_params(("arbitrary", "arbitrary")),
        name="ada_mod",
    )(c, w_ada, b_ada.reshape(DEPTH, 1, 6 * D_MODEL))


LN_ROWS = 256
INPROJ_TN = 768
INPROJ_STEPS = IN_COLS // INPROJ_TN
HY_TILE = COL_HY // INPROJ_TN


def _inproj_kernel(x_ref, mod_ref, w_ref, b_ref, cw_ref, cb_ref, o_ref, ha_ref, hb_ref):
    b = pl.program_id(0)
    j = pl.program_id(1)
    seq = x_ref.shape[1]
    chunks = seq // LN_ROWS
    assert chunks <= INPROJ_STEPS + 1

    @pl.when((b == 0) & (j == 0))
    def _():
        hb_ref[...] = jnp.zeros_like(hb_ref)

    def step(h_ln, h_mm, ln_chunks, conv):
        shift = mod_ref[0, 0:1, :]
        scale = 1.0 + mod_ref[0, 1:2, :]
        for c in ln_chunks:
            r = pl.ds(pl.multiple_of(c * LN_ROWS, LN_ROWS), LN_ROWS)
            h_ln[r, :] = (_ln(x_ref[0, r, :]) * scale + shift).astype(BF16)
        z = _dot(h_mm[...], w_ref[...]) + b_ref[...]
        if conv:
            row = lax.broadcasted_iota(jnp.int32, z.shape, 0)
            prev = jnp.where(row == 0, 0.0, pltpu.roll(z, 1, 0))
            nxt = jnp.where(row == seq - 1, 0.0, pltpu.roll(z, seq - 1, 0))
            z = cb_ref[...] + prev * cw_ref[0:1, :] + z * cw_ref[1:2, :] + nxt * cw_ref[2:3, :]
        o_ref[0] = z.astype(BF16)

    first = [0] + ([INPROJ_STEPS] if chunks > INPROJ_STEPS else [])
    later = [jnp.minimum(j, chunks - 1)]
    for parity, (h_ln, h_mm) in enumerate(((ha_ref, hb_ref), (hb_ref, ha_ref))):
        even = b % 2 == parity
        pl.when(even & (j == 0))(functools.partial(step, h_ln, h_mm, first, False))
        pl.when(even & (j == HY_TILE))(functools.partial(step, h_ln, h_mm, later, True))
        pl.when(even & (j != 0) & (j != HY_TILE))(functools.partial(step, h_ln, h_mm, later, False))


def _inproj(x, mod, w, b, conv_w, conv_b):
    nb, seq, _ = x.shape
    tn = INPROJ_TN
    assert HY_COLS == tn and COL_HY % tn == 0 and HY_TILE != 0 and seq % LN_ROWS == 0
    cur = lambda i, j: (jnp.minimum(i, nb - 1), 0, 0)
    return pl.pallas_call(
        _inproj_kernel,
        grid=(nb + 1, INPROJ_STEPS),
        in_specs=[
            pl.BlockSpec((1, seq, D_MODEL), cur),
            pl.BlockSpec((1, 6, D_MODEL), cur),
            pl.BlockSpec((D_MODEL, tn), lambda i, j: (0, j)),
            pl.BlockSpec((1, tn), lambda i, j: (0, j)),
            pl.BlockSpec((SHORT_CONV, tn), lambda i, j: (0, 0)),
            pl.BlockSpec((1, tn), lambda i, j: (0, 0)),
        ],
        out_specs=pl.BlockSpec((1, seq, tn), lambda i, j: (jnp.maximum(i - 1, 0), 0, jnp.where(i == 0, 0, j))),
        out_shape=jax.ShapeDtypeStruct((nb, seq, IN_COLS), BF16),
        scratch_shapes=[pltpu.VMEM((seq, D_MODEL), BF16), pltpu.VMEM((seq, D_MODEL), BF16)],
        compiler_params=_params(("arbitrary", "arbitrary")),
        name="ln_inproj",
    )(x, mod, w, b, conv_w, conv_b)


FILT_ROWS = 512


def _filter_kernel(bands_ref, w1_ref, b1_ref, w2_ref, b2_ref, w3_ref, b3_ref, dec_ref, o_ref, ts_ref, *, seq):
    n = 2 * seq
    rows = min(FILT_ROWS, n)
    hw = HY_ORDER * HY_WIDTH
    lane = lax.broadcasted_iota(jnp.int32, (1, LANES), 1)

    def fill(i, asum):
        j0 = pl.multiple_of(i * rows, rows)
        j = j0 + lax.broadcasted_iota(jnp.int32, (rows, 1), 0)
        t = jnp.where(j < seq, j, n - j).astype(F32)
        t_norm = t / max(seq - 1, 1)
        ang = (2.0 * math.pi / seq) * t * bands_ref[...]
        feats = jnp.where(lane == 0, t_norm,
                          jnp.where(lane <= N_BANDS, jnp.cos(ang),
                                    jnp.where(lane <= 2 * N_BANDS, -jnp.sin(ang), 0.0)))
        h = jnp.sin(_dot3(feats, w1_ref[...]) + b1_ref[...])
        h = jnp.sin(_dot3(h, w2_ref[...]) + b2_ref[...])
        h = _dot3(h, w3_ref[...]) + b3_ref[...]
        window = jnp.exp(-t_norm * dec_ref[...]) + DECAY_SHIFT
        sel = jnp.where(j < seq, h[:, :hw], h[:, hw:]) * window
        sel = jnp.where(j == seq, 0.0, sel)
        ts_ref[pl.ds(j0, rows), :] = sel
        return asum + jnp.sum(jnp.abs(sel), axis=0, keepdims=True)

    asum = lax.fori_loop(0, n // rows, fill, jnp.zeros((1, hw), F32))
    inv = 1.0 / (asum + EPS)

    def norm(i, carry):
        r = pl.ds(pl.multiple_of(i * rows, rows), rows)
        o_ref[r, :] = (ts_ref[r, :] * inv).astype(BF16)
        return carry

    lax.fori_loop(0, n // rows, norm, 0)


def _filters_time(seq, bands, w1, b1, w2, b2, w3, b3, dec):
    hw = HY_ORDER * HY_WIDTH
    return pl.pallas_call(
        functools.partial(_filter_kernel, seq=seq),
        out_shape=jax.ShapeDtypeStruct((2 * seq, hw), BF16),
        scratch_shapes=[pltpu.VMEM((2 * seq, hw), F32)],
        compiler_params=pltpu.CompilerParams(vmem_limit_bytes=VMEM_LIMIT),
        name="hy_filter_time",
    )(bands, w1, b1, w2, b2, w3, b3, dec)


def _matmul_kernel(a_ref, b_ref, o_ref):
    o_ref[...] = _dot(a_ref[...], b_ref[...])


def _filter_dft(wf, ts):
    n, hw = ts.shape
    tm = min(512, n)
    return pl.pallas_call(
        _matmul_kernel,
        grid=(n // tm,),
        in_specs=[pl.BlockSpec((tm, n), lambda i: (i, 0)), pl.BlockSpec((n, hw), lambda i: (0, 0))],
        out_specs=pl.BlockSpec((tm, hw), lambda i: (i, 0)),
        out_shape=jax.ShapeDtypeStruct((n, hw), F32),
        compiler_params=_params(("arbitrary",)),
        name="hy_filter_dft",
    )(wf, ts)


def _longconv_kernel(v_ref, g_ref, w_ref, wi_ref, kf_ref, bias_ref, o_ref, acc_ref):
    kt = pl.program_id(1)
    nb = v_ref.shape[0]
    tf = w_ref.shape[0] // 2

    @pl.when(kt == 0)
    def _():
        acc_ref[...] = jnp.zeros_like(acc_ref)

    kr = kf_ref[:tf, :]
    ki = kf_ref[tf:, :]
    first = (lax.broadcasted_iota(jnp.int32, (tf, 1), 0) == 0) & (kt == 0)
    for n in range(nb):
        u = _dot(w_ref[...], v_ref[n])
        ur, ui = u[:tf], u[tf:]
        gr = ur * kr - jnp.where(first, 0.0, ui * ki)
        gi = jnp.where(first, ui * ki, ur * ki + ui * kr)
        acc_ref[n] += _dot(wi_ref[...], jnp.concatenate([gr, gi], axis=0).astype(BF16))

    @pl.when(kt == pl.num_programs(1) - 1)
    def _():
        for n in range(nb):
            vf = v_ref[n].astype(F32)
            y = (acc_ref[n] + vf * bias_ref[...]).astype(BF16)
            o_ref[n] = (g_ref[n] * y).astype(BF16)


CONV_TF = 512
CONV_NB = 2


def _longconv(v_arr, v_col, g_arr, g_col, w, wi, kf, order, bias):
    nbatch, seq, _ = v_arr.shape
    nb = math.gcd(nbatch, CONV_NB)
    tf = min(CONV_TF, seq)
    c = HY_WIDTH
    return pl.pallas_call(
        _longconv_kernel,
        grid=(nbatch // nb, seq // tf),
        in_specs=[
            pl.BlockSpec((nb, seq, c), lambda i, k: (i, 0, v_col)),
            pl.BlockSpec((nb, seq, c), lambda i, k: (i, 0, g_col)),
            pl.BlockSpec((None, 2 * tf, seq), lambda i, k: (k, 0, 0)),
            pl.BlockSpec((None, seq, 2 * tf), lambda i, k: (k, 0, 0)),
            pl.BlockSpec((None, 2 * tf, c), lambda i, k: (k, 0, order)),
            pl.BlockSpec((1, c), lambda i, k: (0, 0)),
        ],
        out_specs=pl.BlockSpec((nb, seq, c), lambda i, k: (i, 0, 0)),
        out_shape=jax.ShapeDtypeStruct((nbatch, seq, c), BF16),
        scratch_shapes=[pltpu.VMEM((nb, seq, c), F32)],
        compiler_params=_params(("arbitrary", "arbitrary")),
        name="hy_longconv",
    )(v_arr, g_arr, w, wi, kf, bias)


def _lane_mask(lo, width, n=LANES):
    lane = lax.broadcasted_iota(jnp.int32, (1, n), 1)
    return (lane >= lo) & (lane < lo + width)


ALIBI_SPLIT = 3
POS_RADIX = 256


def _da_coef(head):
    return 2.0 ** (-8.0 * (head + 1) / DA_HEADS) * math.log2(math.e)


def _alibi_tables(seq):
    pos = jnp.arange(seq, dtype=jnp.int32)
    hi = ((pos // POS_RADIX) * POS_RADIX).astype(F32)[:, None]
    lo = (pos % POS_RADIX).astype(F32)[:, None]
    ones = jnp.ones((seq, 1), F32)
    fq, fk = [], []
    for head in range(DA_HEADS):
        rest = jnp.float32(_da_coef(head))
        pieces = []
        for _ in range(ALIBI_SPLIT):
            piece = rest.astype(BF16).astype(F32)
            pieces.append(piece)
            rest = rest - piece
        cq = jnp.concatenate([ones * c for c in pieces], axis=1)
        fq.append(jnp.concatenate([hi] * ALIBI_SPLIT + [lo] * ALIBI_SPLIT + [cq, cq], axis=1))
        fk.append(jnp.concatenate([-cq, -cq] + [hi] * ALIBI_SPLIT + [lo] * ALIBI_SPLIT, axis=1))
    pad = lambda t: jnp.pad(t, ((0, 0), (0, 0), (0, LANES - 4 * ALIBI_SPLIT))).astype(BF16)
    fk = jnp.stack(fk)
    tq = min(DA_TQ, seq)
    key_tile = (pos // tq)[None, :, None]
    signed = [pad(fk * jnp.where(key_tile < qi, 1.0, jnp.where(key_tile > qi, -1.0, 0.0))) for qi in range(seq // tq)]
    return pad(jnp.stack(fq)), signed


def _da_kernel(q_ref, k_ref, v_ref, fq_ref, fk_ref, lam_ref, g_ref, o_ref, *, qi, lam_init):
    tq = q_ref.shape[1]
    seq = k_ref.shape[1]
    lv = lam_ref[...]
    lam = (jnp.exp(jnp.sum(lv[0:1] * lv[1:2], axis=1, keepdims=True))
           - jnp.exp(jnp.sum(lv[2:3] * lv[3:4], axis=1, keepdims=True)) + lam_init)
    local = jnp.abs(lax.broadcasted_iota(jnp.int32, (tq, tq), 0)
                    - lax.broadcasted_iota(jnp.int32, (tq, tq), 1)).astype(F32)
    slabs = []
    for s in range(DA_WIDTH // LANES):
        cols = slice(s * LANES, (s + 1) * LANES)
        q_slab = q_ref[0, :, cols]
        k_slab = k_ref[0, :, cols]
        v_slab = v_ref[0, :, cols]
        o_slab = jnp.zeros((tq, LANES), F32)
        for hh in range(LANES // HEAD_DIM):
            head = s * (LANES // HEAD_DIM) + hh
            k_aug = jnp.concatenate([k_slab, fk_ref[head]], axis=1)
            mine = _lane_mask(hh * HEAD_DIM, HEAD_DIM)
            v_ones = jnp.where(mine, v_slab, jnp.ones_like(v_slab))
            maps = []
            for c in range(2):
                keep = jnp.where(_lane_mask(hh * HEAD_DIM + c * DA_HALF, DA_HALF), 1.0, 0.0).astype(BF16)
                q_aug = jnp.concatenate([q_slab * keep, fq_ref[head]], axis=1)
                sc = _dot_nt(q_aug, k_aug)
                pieces = [sc[:, qi * tq:(qi + 1) * tq] - _da_coef(head) * local]
                if qi > 0:
                    pieces.insert(0, sc[:, :qi * tq])
                if (qi + 1) * tq < seq:
                    pieces.append(sc[:, (qi + 1) * tq:])
                sc = jnp.concatenate(pieces, axis=1)
                p = jnp.exp2(sc - jnp.max(sc, axis=1, keepdims=True))
                o2 = _dot(p.astype(BF16), v_ones)
                maps.append((o2, o2[:, (1 - hh) * HEAD_DIM:(1 - hh) * HEAD_DIM + 1]))
            diff = maps[0][0] * (1.0 / maps[0][1]) - maps[1][0] * (lam / maps[1][1])
            o_slab = jnp.where(mine, diff, o_slab)
        slabs.append(o_slab)
    o = jnp.concatenate(slabs, axis=1)
    y = o * lax.rsqrt(_head_ssq(o) * (1.0 / HEAD_DIM) + EPS) * g_ref[...]
    o_ref[0] = (y * (1.0 - lam_init)).astype(BF16)


DA_TQ = 512


def _diff_attn(z, fq, fk_signed, lam_vecs, g_tiled, lam_init):
    nb, seq, _ = z.shape
    tq = min(DA_TQ, seq)
    cb = COL_DA // DA_WIDTH
    tiles = []
    for qi in range(seq // tq):
        tiles.append(pl.pallas_call(
            functools.partial(_da_kernel, qi=qi, lam_init=lam_init),
            grid=(nb,),
            in_specs=[
                pl.BlockSpec((1, tq, DA_WIDTH), lambda b, qi=qi: (b, qi, cb)),
                pl.BlockSpec((1, seq, DA_WIDTH), lambda b: (b, 0, cb + 1)),
                pl.BlockSpec((1, seq, DA_WIDTH), lambda b: (b, 0, cb + 2)),
                pl.BlockSpec((DA_HEADS, tq, LANES), lambda b, qi=qi: (0, qi, 0)),
                pl.BlockSpec((DA_HEADS, seq, LANES), lambda b: (0, 0, 0)),
                pl.BlockSpec((4, DA_HALF), lambda b: (0, 0)),
                pl.BlockSpec((1, DA_WIDTH), lambda b: (0, 0)),
            ],
            out_specs=pl.BlockSpec((1, tq, DA_WIDTH), lambda b: (b, 0, 0)),
            out_shape=jax.ShapeDtypeStruct((nb, tq, DA_WIDTH), BF16),
            compiler_params=_params(("arbitrary",)),
            name="diff_attn",
        )(z, z, z, fq, fk_signed[qi], lam_vecs, g_tiled))
    return jnp.concatenate(tiles, axis=1)


def _rope(x, cos, sin):
    w = x.shape[1]
    lane = lax.broadcasted_iota(jnp.int32, (1, w), 1)
    half = ROPE_AXIS // 2
    partner = jnp.where(lane % ROPE_AXIS < half, pltpu.roll(x, w - half, 1), pltpu.roll(x, half, 1))
    return x * cos + partner * sin


def _dup_heads(x):
    first = _lane_mask(0, HEAD_DIM)
    a = jnp.where(first, x, 0.0)
    b = jnp.where(first, 0.0, x)
    return jnp.concatenate([a + pltpu.roll(a, HEAD_DIM, 1), b + pltpu.roll(b, HEAD_DIM, 1)], axis=1)


def _with_ones(v):
    first = _lane_mask(0, HEAD_DIM)
    swapped = pltpu.roll(v, HEAD_DIM, 1)
    return jnp.concatenate([jnp.where(first, v, 1.0), jnp.where(first, 1.0, swapped),
                            jnp.where(first, swapped, 1.0), jnp.where(first, 1.0, v)], axis=1)


def _gqa_prep_kernel(q_ref, kv_ref, cos_ref, sin_ref, gq_ref, gk_ref, qo_ref, ko_ref, vo_ref):
    kvw = GQ_KV * HEAD_DIM
    q = q_ref[0].astype(F32)
    qn = q * lax.rsqrt(_head_ssq(q) * (1.0 / HEAD_DIM) + EPS) * gq_ref[...]
    qo_ref[0] = (_rope(qn, cos_ref[...], sin_ref[...]) * (HEAD_DIM ** -0.5 * math.log2(math.e))).astype(BF16)
    k = kv_ref[0, :, :kvw].astype(F32)
    kn = k * lax.rsqrt(_head_ssq(k) * (1.0 / HEAD_DIM) + EPS) * gk_ref[...]
    kr = _rope(kn, cos_ref[:, :kvw], sin_ref[:, :kvw])
    ko_ref[0] = _dup_heads(kr).astype(BF16)
    vo_ref[0] = _with_ones(kv_ref[0, :, kvw:].astype(F32)).astype(BF16)


def _gqa_prep(z, cos, sin, gq, gk):
    nb, seq, _ = z.shape
    tl = min(512, seq)
    kvw = GQ_KV * HEAD_DIM
    return pl.pallas_call(
        _gqa_prep_kernel,
        grid=(nb, seq // tl),
        in_specs=[
            pl.BlockSpec((1, tl, GQ_WIDTH), lambda b, i: (b, i, COL_GQ // GQ_WIDTH)),
            pl.BlockSpec((1, tl, 2 * kvw), lambda b, i: (b, i, (COL_GQ + GQ_WIDTH) // (2 * kvw))),
            pl.BlockSpec((tl, GQ_WIDTH), lambda b, i: (i, 0)),
            pl.BlockSpec((tl, GQ_WIDTH), lambda b, i: (i, 0)),
            pl.BlockSpec((1, GQ_WIDTH), lambda b, i: (0, 0)),
            pl.BlockSpec((1, kvw), lambda b, i: (0, 0)),
        ],
        out_specs=[
            pl.BlockSpec((1, tl, GQ_WIDTH), lambda b, i: (b, i, 0)),
            pl.BlockSpec((1, tl, 2 * kvw), lambda b, i: (b, i, 0)),
            pl.BlockSpec((1, tl, 4 * kvw), lambda b, i: (b, i, 0)),
        ],
        out_shape=[
            jax.ShapeDtypeStruct((nb, seq, GQ_WIDTH), BF16),
            jax.ShapeDtypeStruct((nb, seq, 2 * kvw), BF16),
            jax.ShapeDtypeStruct((nb, seq, 4 * kvw), BF16),
        ],
        compiler_params=_params(("arbitrary", "arbitrary")),
        name="gqa_prep",
    )(z, z, cos, sin, gq, gk)


def _gqa_kernel(q_ref, k_ref, v_ref, o_ref):
    tq = q_ref.shape[1]
    per_slab = LANES // HEAD_DIM
    group = GQ_HEADS // GQ_KV
    slabs = []
    for s in range(GQ_WIDTH // LANES):
        q_slab = q_ref[0, :, s * LANES:(s + 1) * LANES]
        kv = (s * per_slab) // group
        k_dup = k_ref[0, :, kv * LANES:(kv + 1) * LANES]
        o_slab = jnp.zeros((tq, LANES), F32)
        for hh in range(per_slab):
            keep = jnp.where(_lane_mask(hh * HEAD_DIM, HEAD_DIM), 1.0, 0.0).astype(BF16)
            sc = _dot_nt(q_slab * keep, k_dup)
            p = jnp.exp2(sc - jnp.max(sc, axis=1, keepdims=True))
            v_ones = v_ref[0, :, (kv * per_slab + hh) * LANES:(kv * per_slab + hh + 1) * LANES]
            o2 = _dot(p.astype(BF16), v_ones)
            sums = o2[:, (1 - hh) * HEAD_DIM:(1 - hh) * HEAD_DIM + 1]
            o_slab = jnp.where(_lane_mask(hh * HEAD_DIM, HEAD_DIM), o2 * (1.0 / sums), o_slab)
        slabs.append(o_slab)
    o_ref[0] = jnp.concatenate(slabs, axis=1).astype(BF16)


def _gqa_attn(q, k, v):
    nb, seq, _ = q.shape
    tq = min(512, seq)
    kvw = 2 * GQ_KV * HEAD_DIM
    return pl.pallas_call(
        _gqa_kernel,
        grid=(nb, seq // tq),
        in_specs=[
            pl.BlockSpec((1, tq, GQ_WIDTH), lambda b, i: (b, i, 0)),
            pl.BlockSpec((1, seq, kvw), lambda b, i: (b, 0, 0)),
            pl.BlockSpec((1, seq, 2 * kvw), lambda b, i: (b, 0, 0)),
        ],
        out_specs=pl.BlockSpec((1, tq, GQ_WIDTH), lambda b, i: (b, i, 0)),
        out_shape=jax.ShapeDtypeStruct((nb, seq, GQ_WIDTH), BF16),
        compiler_params=_params(("arbitrary", "arbitrary")),
        name="gqa_attn",
    )(q, k, v)


def _merge_kernel(ya_ref, yb_ref, yc_ref, zg_ref, x_ref, mod_ref, wa_ref, wb_ref, wc_ref, wo_ref,
                  lg_ref, lb_ref, *rest):
    d = D_MODEL
    m = (jnp.tanh(zg_ref[0, :, 0:d].astype(F32)) + 1.0) * _dot(ya_ref[0], wa_ref[...])
    m = m + (jnp.tanh(zg_ref[0, :, d:2 * d].astype(F32)) + 1.0) * _dot(yb_ref[0], wb_ref[...])
    m = m + (jnp.tanh(zg_ref[0, :, 2 * d:3 * d].astype(F32)) + 1.0) * _dot(yc_ref[0], wc_ref[...])
    y = _dot(m.astype(BF16), wo_ref[...])
    gate1 = 1.0 + mod_ref[0, 2:3, :]
    xn = _ln(DEEPNORM_ALPHA * x_ref[0] + gate1 * y) * lg_ref[...] + lb_ref[...]
    h = _ln(xn) * (1.0 + mod_ref[0, 4:5, :]) + mod_ref[0, 3:4, :]
    if len(rest) == 2:
        xo_ref, h_ref = rest
    else:
        wr_ref, br_ref, xo_ref, h_ref, comb_ref = rest
        lane = lax.broadcasted_iota(jnp.int32, (1, LANES), 1).astype(F32)
        neg = -jnp.inf
        logits = jnp.where(lane < N_EXPERTS, _dot3(h, wr_ref[...]) + br_ref[...], neg)
        v1 = jnp.max(logits, axis=1, keepdims=True)
        i1 = jnp.min(jnp.where(logits == v1, lane, float(LANES)), axis=1, keepdims=True)
        others = jnp.where(lane == i1, neg, logits)
        v2 = jnp.max(others, axis=1, keepdims=True)
        i2 = jnp.min(jnp.where(others == v2, lane, float(LANES)), axis=1, keepdims=True)
        e = jnp.exp(v2 - v1)
        w1 = 1.0 / (1.0 + e)
        comb_ref[0] = jnp.where(lane == i1, w1, 0.0) + jnp.where(lane == i2, e * w1, 0.0)
    xo_ref[0] = xn
    h_ref[0] = h.astype(BF16)


def _merge(ya, yb, yc, z, x, mod, wa, wb, wc, wo, lg, lb, router=None):
    nb, seq, d = x.shape
    tm = min(512, seq)
    full = lambda shape: pl.BlockSpec(shape, lambda b, i: (0,) * len(shape))
    row = lambda w, col=0: pl.BlockSpec((1, tm, w), lambda b, i: (b, i, col))
    in_specs = [
        row(HY_WIDTH), row(DA_WIDTH), row(GQ_WIDTH), row(GATE_COLS, COL_GATE // GATE_COLS), row(d),
        pl.BlockSpec((1, 6, d), lambda b, i: (b, 0, 0)),
        full((HY_WIDTH, d)), full((DA_WIDTH, d)), full((GQ_WIDTH, d)), full((d, d)),
        full((1, d)), full((1, d)),
    ]
    out_specs = [row(d), row(d)]
    out_shape = [jax.ShapeDtypeStruct((nb, seq, d), F32), jax.ShapeDtypeStruct((nb, seq, d), BF16)]
    args = [ya, yb, yc, z, x, mod, wa, wb, wc, wo, lg, lb]
    if router is not None:
        in_specs += [full((d, LANES)), full((1, LANES))]
        out_specs.append(row(LANES))
        out_shape.append(jax.ShapeDtypeStruct((nb, seq, LANES), F32))
        args += list(router)
    return pl.pallas_call(
        _merge_kernel,
        grid=(nb, seq // tm),
        in_specs=in_specs,
        out_specs=out_specs,
        out_shape=out_shape,
        compiler_params=_params(("arbitrary", "arbitrary")),
        name="merge_outproj",
    )(*args)


def _ffn_kernel(h_ref, wg_ref, wu_ref, wd_ref, x_ref, mod_ref, lg_ref, lb_ref, o_ref, acc_ref):
    j = pl.program_id(2)

    @pl.when(j == 0)
    def _():
        acc_ref[...] = jnp.zeros_like(acc_ref)

    h = h_ref[0]
    g = _dot(h, wg_ref[...])
    u = _dot(h, wu_ref[...])
    a = g * _sigmoid(g) * u
    acc_ref[...] += _dot(a.astype(BF16), wd_ref[...])

    @pl.when(j == pl.num_programs(2) - 1)
    def _():
        gate2 = 1.0 + mod_ref[0, 5:6, :]
        o_ref[0] = _ln(DEEPNORM_ALPHA * x_ref[0] + gate2 * acc_ref[...]) * lg_ref[...] + lb_ref[...]


def _ffn(h, w_up, w_down, x, mod, lg, lb):
    nb, seq, d = x.shape
    ff = w_down.shape[0]
    tm = min(1024, seq)
    tf = 256
    nf = ff // tf
    row = lambda w: pl.BlockSpec((1, tm, w), lambda b, i, j: (b, i, 0))
    vec = pl.BlockSpec((1, d), lambda b, i, j: (0, 0))
    return pl.pallas_call(
        _ffn_kernel,
        grid=(nb, seq // tm, nf),
        in_specs=[
            row(d),
            pl.BlockSpec((d, tf), lambda b, i, j: (0, j)),
            pl.BlockSpec((d, tf), lambda b, i, j: (0, nf + j)),
            pl.BlockSpec((tf, d), lambda b, i, j: (j, 0)),
            row(d),
            pl.BlockSpec((1, 6, d), lambda b, i, j: (b, 0, 0)),
            vec, vec,
        ],
        out_specs=row(d),
        out_shape=jax.ShapeDtypeStruct((nb, seq, d), F32),
        scratch_shapes=[pltpu.VMEM((tm, d), F32)],
        compiler_params=_params(("arbitrary",) * 3),
        name="swiglu_dense",
    )(h, w_up, w_up, w_down, x, mod, lg, lb)


MOE_TOKENS = 1024
MOE_CHUNK = 288
MOE_SLOT = 384


def _moe_kernel(h_ref, comb_ref, wg_ref, wu_ref, wd_ref, x_ref, mod_ref, lg_ref, lb_ref, o_ref,
                posc_ref, posr_ref, cnt_ref, ys_ref):
    e = pl.program_id(2)
    tm = h_ref.shape[1]
    n_exp = ys_ref.shape[0] // MOE_SLOT
    lane = lax.broadcasted_iota(jnp.int32, (1, LANES), 1)

    @pl.when(e == 0)
    def _():
        comb = comb_ref[0]
        sel = jnp.where(comb > 0.0, 1.0, 0.0)
        r = lax.broadcasted_iota(jnp.int32, (tm, tm), 0)
        c = lax.broadcasted_iota(jnp.int32, (tm, tm), 1)
        before = jnp.where(c < r, 1.0, 0.0).astype(BF16)
        rank = _dot(before, sel.astype(BF16))
        posc = jnp.where(comb > 0.0, rank, -1.0)
        posc_ref[...] = posc
        posr_ref[...] = posc.T
        cnt_ref[...] = jnp.sum(sel, axis=0, keepdims=True)
        o_ref[0] = jnp.zeros(o_ref.shape[1:], F32)
        ys_ref[...] = jnp.zeros_like(ys_ref)

    mine = lane == e
    pos_row = posr_ref[pl.ds(e, 1), :]
    count = jnp.sum(jnp.where(mine, cnt_ref[...], 0.0))

    def expert_rows(c):
        slot_r = (c * MOE_CHUNK + lax.broadcasted_iota(jnp.int32, (MOE_CHUNK, 1), 0)).astype(F32)
        gather = jnp.where(pos_row == slot_r, 1.0, 0.0).astype(BF16)
        xs = _dot(gather, h_ref[0]).astype(BF16)
        g = _dot(xs, wg_ref[...])
        u = _dot(xs, wu_ref[...])
        a = (g * _sigmoid(g) * u).astype(BF16)
        return _dot(a, wd_ref[...]).astype(BF16)

    ys_ref[pl.ds(pl.multiple_of(e * MOE_SLOT, MOE_SLOT), MOE_CHUNK), :] = expert_rows(0)

    for c in range(1, -(-tm // MOE_CHUNK)):
        @pl.when(count > c * MOE_CHUNK)
        def _():
            pos_col = jnp.sum(jnp.where(mine, posc_ref[...], 0.0), axis=1, keepdims=True)
            w_col = jnp.sum(jnp.where(mine, comb_ref[0], 0.0), axis=1, keepdims=True)
            slot_c = (c * MOE_CHUNK + lax.broadcasted_iota(jnp.int32, (1, MOE_CHUNK), 1)).astype(F32)
            scatter = jnp.where(pos_col == slot_c, 1.0, 0.0).astype(BF16)
            o_ref[0] += w_col * _dot(scatter, expert_rows(c))

    @pl.when(e == pl.num_programs(2) - 1)
    def _():
        slot_c = lax.broadcasted_iota(jnp.int32, (1, MOE_SLOT), 1).astype(F32)
        slot_c = jnp.where(slot_c < MOE_CHUNK, slot_c, -2.0)
        f = o_ref[0]
        for e0 in range(0, n_exp, 2):
            weights = jnp.concatenate(
                [jnp.where(posc_ref[:, ee:ee + 1] == slot_c, comb_ref[0, :, ee:ee + 1], 0.0).astype(BF16)
                 for ee in (e0, e0 + 1)], axis=1)
            f = f + _dot(weights, ys_ref[e0 * MOE_SLOT:(e0 + 2) * MOE_SLOT, :])
        gate2 = 1.0 + mod_ref[0, 5:6, :]
        o_ref[0] = _ln(DEEPNORM_ALPHA * x_ref[0] + gate2 * f) * lg_ref[...] + lb_ref[...]


def _moe(h, comb, w_up, w_down, x, mod, lg, lb):
    nb, seq, d = x.shape
    ne, ff, _ = w_down.shape
    tm = min(MOE_TOKENS, seq)
    row = lambda w: pl.BlockSpec((1, tm, w), lambda b, i, e: (b, i, 0))
    vec = pl.BlockSpec((1, d), lambda b, i, e: (0, 0))
    return pl.pallas_call(
        _moe_kernel,
        grid=(nb, seq // tm, ne),
        in_specs=[
            row(d), row(LANES),
            pl.BlockSpec((None, d, ff), lambda b, i, e: (e, 0, 0)),
            pl.BlockSpec((None, d, ff), lambda b, i, e: (e, 0, 1)),
            pl.BlockSpec((None, ff, d), lambda b, i, e: (e, 0, 0)),
            row(d),
            pl.BlockSpec((1, 6, d), lambda b, i, e: (b, 0, 0)),
            vec, vec,
        ],
        out_specs=row(d),
        out_shape=jax.ShapeDtypeStruct((nb, seq, d), F32),
        scratch_shapes=[pltpu.VMEM((tm, LANES), F32), pltpu.VMEM((LANES, tm), F32), pltpu.VMEM((1, LANES), F32),
                        pltpu.VMEM((ne * MOE_SLOT, d), BF16)],
        compiler_params=_params(("arbitrary",) * 3),
        name="swiglu_routed",
    )(h, comb, w_up, w_up, w_down, x, mod, lg, lb)


def _dft_tables(seq):
    n = 2 * seq
    tf = min(CONV_TF, seq)
    nk = seq // tf
    s = jnp.arange(n, dtype=jnp.int32)[None, :]

    def phase(mult):
        ang = ((mult * s) % n).astype(F32) * (2.0 * math.pi / n)
        return jnp.cos(ang), jnp.sin(ang)

    ca, sa = phase(jnp.arange(nk, dtype=jnp.int32)[:, None] * tf)
    cb, sb = phase(jnp.arange(tf, dtype=jnp.int32)[:, None])
    cos = ca[:, None, :] * cb[None] - sa[:, None, :] * sb[None]
    msin = -(sa[:, None, :] * cb[None] + ca[:, None, :] * sb[None])
    k = (jnp.arange(nk, dtype=jnp.int32)[:, None] * tf + jnp.arange(tf, dtype=jnp.int32)[None, :])[:, :, None]
    msin = jnp.where(k == 0, jnp.where(s % 2 == 0, 1.0, -1.0)[None], msin)
    fwd = jnp.concatenate([cos, msin], axis=1)
    scale = jnp.where(k == 0, 1.0 / n, 2.0 / n).astype(F32)
    inv = jnp.concatenate([cos[:, :, :seq] * scale, msin[:, :, :seq] * scale], axis=1).transpose(0, 2, 1)
    return fwd.astype(BF16), inv.astype(BF16)


def _rope_tables(seq):
    t = jnp.arange(seq, dtype=jnp.int32)
    inv = ROPE_THETA ** (-jnp.arange(0, ROPE_AXIS, 2, dtype=F32) / ROPE_AXIS)
    ang_r = (t // GRID_W).astype(F32)[:, None] * inv[None, :]
    ang_c = (t % GRID_W).astype(F32)[:, None] * inv[None, :]
    cos = jnp.concatenate([jnp.cos(ang_r)] * 2 + [jnp.cos(ang_c)] * 2, axis=1)
    sin = jnp.concatenate([-jnp.sin(ang_r), jnp.sin(ang_r), -jnp.sin(ang_c), jnp.sin(ang_c)], axis=1)
    return jnp.tile(cos, (1, GQ_HEADS)), jnp.tile(sin, (1, GQ_HEADS))


def _pad_to(a, rows, cols):
    return jnp.pad(a, ((0, rows - a.shape[0]), (0, cols - a.shape[1])))


def _prepare(seq, p):
    fwd, inv = _dft_tables(seq)
    cos, sin = _rope_tables(seq)
    bands = jnp.linspace(1e-4, N_BANDS - 1, N_BANDS, dtype=F32)
    bands_row = jnp.zeros((1, LANES), F32).at[0, 1:1 + N_BANDS].set(bands).at[0, 1 + N_BANDS:1 + 2 * N_BANDS].set(bands)
    deltas = jnp.abs(jnp.linspace(MIN_DECAY, MAX_DECAY, HY_WIDTH, dtype=F32))
    dec = jnp.tile(deltas, HY_ORDER)[None, :]
    da_scale = DA_HALF ** -0.5 * math.log2(math.e)
    fq, fk = _alibi_tables(seq)
    layers = []
    for l in range(DEPTH):
        w_in, b_in = p['w_in'][l], p['b_in'][l]
        a0, b0, c0, g0 = 0, HY_COLS, HY_COLS + DA_COLS, HY_COLS + DA_COLS + GQ_COLS
        w_cols = [w_in[:, g0:] * 0.5, w_in[:, a0:b0], w_in[:, b0:b0 + DA_WIDTH] * da_scale, w_in[:, b0 + DA_WIDTH:c0], w_in[:, c0:g0]]
        b_cols = [b_in[g0:] * 0.5, b_in[a0:b0], b_in[b0:b0 + DA_WIDTH] * da_scale, b_in[b0 + DA_WIDTH:c0], b_in[c0:g0]]
        ts = _filters_time(
            seq, bands_row,
            _pad_to(p['hy_f_w1'][l], LANES, LANES), _pad_to(p['hy_f_b1'][l][None, :], 1, LANES),
            _pad_to(p['hy_f_w2'][l], LANES, LANES), _pad_to(p['hy_f_b2'][l][None, :], 1, LANES),
            _pad_to(p['hy_f_w3'][l], LANES, 2 * HY_ORDER * HY_WIDTH), p['hy_f_b3'][l][None, :], dec)
        kf = _filter_dft(fwd.reshape(2 * seq, 2 * seq), ts).reshape(fwd.shape[0], -1, HY_ORDER * HY_WIDTH)
        lay = dict(
            w_in=jnp.concatenate(w_cols, axis=1).astype(BF16),
            b_in=jnp.concatenate(b_cols)[None, :],
            conv_w=p['hy_conv_w'][l], conv_b=p['hy_conv_b'][l][None, :],
            kf=kf, hy_bias=p['hy_bias'][l],
            lam=jnp.stack([p['da_lam_q1'][l], p['da_lam_k1'][l], p['da_lam_q2'][l], p['da_lam_k2'][l]]),
            lam_init=0.8 - 0.6 * math.exp(-0.3 * l),
            subln=jnp.tile(p['da_subln_g'][l], DA_HEADS)[None, :],
            gq=jnp.tile(p['gq_qnorm_g'][l], GQ_HEADS)[None, :],
            gk=jnp.tile(p['gq_knorm_g'][l], GQ_KV)[None, :],
            wa=(p['w_br_a'][l] * 0.5).astype(BF16), wb=(p['w_br_b'][l] * 0.5).astype(BF16),
            wc=(p['w_br_c'][l] * 0.5).astype(BF16),
            wo=p['w_out'][l].astype(BF16),
            ln1_g=p['ln1_g'][l][None, :], ln1_b=p['ln1_b'][l][None, :],
            ln2_g=p['ln2_g'][l][None, :], ln2_b=p['ln2_b'][l][None, :],
        )
        if l % 2 == 0:
            lay.update(w_up=p['ffn_w_up'][l // 2].astype(BF16), w_down=p['ffn_w_down'][l // 2].astype(BF16))
        else:
            lay.update(w_up=p['moe_w_up'][l // 2].astype(BF16), w_down=p['moe_w_down'][l // 2].astype(BF16),
                       router=(_pad_to(p['moe_w_router'][l // 2], D_MODEL, LANES),
                               _pad_to(p['moe_b_router'][l // 2][None, :], 1, LANES)))
        layers.append(lay)
    return dict(fwd=fwd, inv=inv, cos=cos, sin=sin, fq=fq, fk=fk, layers=layers)


def _trunk(x, c, p, prep):
    mods = _ada(c, p['w_ada'], p['b_ada'])
    nb = x.shape[0]
    for l, lay in enumerate(prep['layers']):
        mod = mods[l].reshape(nb, 6, D_MODEL)
        z = _inproj(x, mod, lay['w_in'], lay['b_in'], lay['conv_w'], lay['conv_b'])
        hy = COL_HY // HY_WIDTH
        u = _longconv(z, hy + 2, z, hy, prep['fwd'], prep['inv'], lay['kf'], 0, lay['hy_bias'][0:1])
        ya = _longconv(u, 0, z, hy + 1, prep['fwd'], prep['inv'], lay['kf'], 1, lay['hy_bias'][1:2])
        yb = _diff_attn(z, prep['fq'], prep['fk'], lay['lam'], lay['subln'], lay['lam_init'])
        q, kd, vd = _gqa_prep(z, prep['cos'], prep['sin'], lay['gq'], lay['gk'])
        yc = _gqa_attn(q, kd, vd)
        merged = _merge(ya, yb, yc, z, x, mod, lay['wa'], lay['wb'], lay['wc'], lay['wo'],
                        lay['ln1_g'], lay['ln1_b'], lay.get('router'))
        if l % 2 == 0:
            x, h = merged
            x = _ffn(h, lay['w_up'], lay['w_down'], x, mod, lay['ln2_g'], lay['ln2_b'])
        else:
            x, h, comb = merged
            x = _moe(h, comb, lay['w_up'], lay['w_down'], x, mod, lay['ln2_g'], lay['ln2_b'])
    return x


def kernel(x_prompt, x_sample, c_prompt, c_sample, w_ada, b_ada, w_in, b_in, hy_conv_w, hy_conv_b, hy_f_w1, hy_f_b1, hy_f_w2, hy_f_b2, hy_f_w3, hy_f_b3, hy_bias, da_lam_q1, da_lam_k1, da_lam_q2, da_lam_k2, da_subln_g, gq_qnorm_g, gq_knorm_g, w_br_a, w_br_b, w_br_c, w_out, ln1_g, ln1_b, ffn_w_up, ffn_w_down, moe_w_router, moe_b_router, moe_w_up, moe_w_down, ln2_g, ln2_b):
    p = dict(w_ada=w_ada, b_ada=b_ada, w_in=w_in, b_in=b_in,
             hy_conv_w=hy_conv_w, hy_conv_b=hy_conv_b, hy_f_w1=hy_f_w1, hy_f_b1=hy_f_b1,
             hy_f_w2=hy_f_w2, hy_f_b2=hy_f_b2, hy_f_w3=hy_f_w3, hy_f_b3=hy_f_b3, hy_bias=hy_bias,
             da_lam_q1=da_lam_q1, da_lam_k1=da_lam_k1, da_lam_q2=da_lam_q2, da_lam_k2=da_lam_k2,
             da_subln_g=da_subln_g, gq_qnorm_g=gq_qnorm_g, gq_knorm_g=gq_knorm_g,
             w_br_a=w_br_a, w_br_b=w_br_b, w_br_c=w_br_c, w_out=w_out, ln1_g=ln1_g, ln1_b=ln1_b,
             ffn_w_up=ffn_w_up, ffn_w_down=ffn_w_down, moe_w_router=moe_w_router,
             moe_b_router=moe_b_router, moe_w_up=moe_w_up, moe_w_down=moe_w_down,
             ln2_g=ln2_g, ln2_b=ln2_b)
    assert x_prompt.shape[1] == x_sample.shape[1]
    prep = _prepare(x_prompt.shape[1], p)
    return (_trunk(x_prompt, c_prompt, p, prep), _trunk(x_sample, c_sample, p, prep))
```

```python
import functools
import math

import jax
import jax.numpy as jnp
from jax import lax
from jax.experimental import pallas as pl
from jax.experimental.pallas import tpu as pltpu

F32 = jnp.float32
BF16 = jnp.bfloat16

D_MODEL = 1024
DEPTH = 2
GRID_W = 64
HEAD_DIM = 64
HY_WIDTH = D_MODEL // 4
HY_ORDER = 2
SHORT_CONV = 3
N_BANDS = 16
FILTER_HID = 64
DECAY_TARGET = 1e-2
MIN_DECAY = math.log(DECAY_TARGET) / 1.5
MAX_DECAY = math.log(DECAY_TARGET) / 0.3
DECAY_SHIFT = 0.05
DA_HEADS = 4
DA_HALF = HEAD_DIM // 2
DA_WIDTH = DA_HEADS * HEAD_DIM
GQ_HEADS = 8
GQ_KV = 2
GQ_WIDTH = GQ_HEADS * HEAD_DIM
ROPE_AXIS = HEAD_DIM // 2
ROPE_THETA = 10000.0
HY_COLS = (HY_ORDER + 1) * HY_WIDTH
DA_COLS = 3 * DA_WIDTH
GQ_COLS = GQ_WIDTH + 2 * GQ_KV * HEAD_DIM
GATE_COLS = 3 * D_MODEL
IN_COLS = HY_COLS + DA_COLS + GQ_COLS + GATE_COLS
D_FF = 256 * ((8 * D_MODEL // 3 + 255) // 256)
N_EXPERTS = 8
MOE_FF = D_FF // 2
DEEPNORM_ALPHA = (2.0 * DEPTH) ** 0.25
EPS = 1e-5

LANES = 128
VMEM_LIMIT = 56 * 1024 * 1024

COL_GATE = 0
COL_HY = GATE_COLS
COL_DA = COL_HY + HY_COLS
COL_GQ = COL_DA + DA_COLS


def _params(sem):
    return pltpu.CompilerParams(dimension_semantics=sem, vmem_limit_bytes=VMEM_LIMIT)


def _dot(a, b):
    return jnp.dot(a, b, preferred_element_type=F32)


def _dot_nt(a, b):
    return lax.dot_general(a, b, (((1,), (1,)), ((), ())), preferred_element_type=F32)


def _split(x):
    hi = x.astype(BF16)
    lo = (x - hi.astype(F32)).astype(BF16)
    return hi, lo


def _dot3(a, b):
    ah, al = _split(a)
    bh, bl = _split(b)
    return _dot(ah, bh) + (_dot(ah, bl) + _dot(al, bh))


def _sigmoid(x):
    return 0.5 * jnp.tanh(0.5 * x) + 0.5


def _ln(x):
    mu = jnp.mean(x, axis=-1, keepdims=True)
    xc = x - mu
    var = jnp.mean(xc * xc, axis=-1, keepdims=True)
    return xc * lax.rsqrt(var + EPS)


def _head_ssq(x):
    w = x.shape[1]
    r = lax.broadcasted_iota(jnp.int32, (w, w), 0) // HEAD_DIM
    c = lax.broadcasted_iota(jnp.int32, (w, w), 1) // HEAD_DIM
    ones = jnp.where(r == c, 1.0, 0.0).astype(BF16)
    hi, lo = _split(x * x)
    return _dot(hi, ones) + _dot(lo, ones)


def _ada_kernel(c_ref, w_ref, b_ref, o_ref):
    c = c_ref[...]
    o_ref[...] = _dot3(c * _sigmoid(c), w_ref[...]) + b_ref[...]


def _ada(c, w_ada, b_ada):
    nb = c.shape[0]
    tn = 1536
    return pl.pallas_call(
        _ada_kernel,
        grid=(DEPTH, 6 * D_MODEL // tn),
        in_specs=[
            pl.BlockSpec((nb, D_MODEL), lambda l, j: (0, 0)),
            pl.BlockSpec((None, D_MODEL, tn), lambda l, j: (l, 0, j)),
            pl.BlockSpec((None, 1, tn), lambda l, j: (l, 0, j)),
        ],
        out_specs=pl.BlockSpec((None, nb, tn), lambda l, j: (l, 0, j)),
        out_shape=jax.ShapeDtypeStruct((DEPTH, nb, 6 * D_MODEL), F32),
        compiler_params=_params(("arbitrary", "arbitrary")),
        name="ada_mod",
    )(c, w_ada, b_ada.reshape(DEPTH, 1, 6 * D_MODEL))


LN_ROWS = 256
INPROJ_TN = 768
INPROJ_STEPS = IN_COLS // INPROJ_TN
HY_TILE = COL_HY // INPROJ_TN


def _inproj_kernel(x_ref, mod_ref, w_ref, b_ref, cw_ref, cb_ref, o_ref, ha_ref, hb_ref):
    b = pl.program_id(0)
    j = pl.program_id(1)
    seq = x_ref.shape[1]
    chunks = seq // LN_ROWS
    assert chunks <= INPROJ_STEPS + 1

    @pl.when((b == 0) & (j == 0))
    def _():
        hb_ref[...] = jnp.zeros_like(hb_ref)

    def step(h_ln, h_mm, ln_chunks, conv):
        shift = mod_ref[0, 0:1, :]
        scale = 1.0 + mod_ref[0, 1:2, :]
        for c in ln_chunks:
            r = pl.ds(pl.multiple_of(c * LN_ROWS, LN_ROWS), LN_ROWS)
            h_ln[r, :] = (_ln(x_ref[0, r, :]) * scale + shift).astype(BF16)
        z = _dot(h_mm[...], w_ref[...]) + b_ref[...]
        if conv:
            row = lax.broadcasted_iota(jnp.int32, z.shape, 0)
            prev = jnp.where(row == 0, 0.0, pltpu.roll(z, 1, 0))
            nxt = jnp.where(row == seq - 1, 0.0, pltpu.roll(z, seq - 1, 0))
            z = cb_ref[...] + prev * cw_ref[0:1, :] + z * cw_ref[1:2, :] + nxt * cw_ref[2:3, :]
        o_ref[0] = z.astype(BF16)

    first = [0] + ([INPROJ_STEPS] if chunks > INPROJ_STEPS else [])
    later = [jnp.minimum(j, chunks - 1)]
    for parity, (h_ln, h_mm) in enumerate(((ha_ref, hb_ref), (hb_ref, ha_ref))):
        even = b % 2 == parity
        pl.when(even & (j == 0))(functools.partial(step, h_ln, h_mm, first, False))
        pl.when(even & (j == HY_TILE))(functools.partial(step, h_ln, h_mm, later, True))
        pl.when(even & (j != 0) & (j != HY_TILE))(functools.partial(step, h_ln, h_mm, later, False))


def _inproj(x, mod, w, b, conv_w, conv_b):
    nb, seq, _ = x.shape
    tn = INPROJ_TN
    assert HY_COLS == tn and COL_HY % tn == 0 and HY_TILE != 0 and seq % LN_ROWS == 0
    cur = lambda i, j: (jnp.minimum(i, nb - 1), 0, 0)
    return pl.pallas_call(
        _inproj_kernel,
        grid=(nb + 1, INPROJ_STEPS),
        in_specs=[
            pl.BlockSpec((1, seq, D_MODEL), cur),
            pl.BlockSpec((1, 6, D_MODEL), cur),
            pl.BlockSpec((D_MODEL, tn), lambda i, j: (0, j)),
            pl.BlockSpec((1, tn), lambda i, j: (0, j)),
            pl.BlockSpec((SHORT_CONV, tn), lambda i, j: (0, 0)),
            pl.BlockSpec((1, tn), lambda i, j: (0, 0)),
        ],
        out_specs=pl.BlockSpec((1, seq, tn), lambda i, j: (jnp.maximum(i - 1, 0), 0, jnp.where(i == 0, 0, j))),
        out_shape=jax.ShapeDtypeStruct((nb, seq, IN_COLS), BF16),
        scratch_shapes=[pltpu.VMEM((seq, D_MODEL), BF16), pltpu.VMEM((seq, D_MODEL), BF16)],
        compiler_params=_params(("arbitrary", "arbitrary")),
        name="ln_inproj",
    )(x, mod, w, b, conv_w, conv_b)


FILT_ROWS = 512


def _filter_kernel(bands_ref, w1_ref, b1_ref, w2_ref, b2_ref, w3_ref, b3_ref, dec_ref, o_ref, ts_ref, *, seq):
    n = 2 * seq
    rows = min(FILT_ROWS, n)
    hw = HY_ORDER * HY_WIDTH
    lane = lax.broadcasted_iota(jnp.int32, (1, LANES), 1)

    def fill(i, asum):
        j0 = pl.multiple_of(i * rows, rows)
        j = j0 + lax.broadcasted_iota(jnp.int32, (rows, 1), 0)
        t = jnp.where(j < seq, j, n - j).astype(F32)
        t_norm = t / max(seq - 1, 1)
        ang = (2.0 * math.pi / seq) * t * bands_ref[...]
        feats = jnp.where(lane == 0, t_norm,
                          jnp.where(lane <= N_BANDS, jnp.cos(ang),
                                    jnp.where(lane <= 2 * N_BANDS, -jnp.sin(ang), 0.0)))
        h = jnp.sin(_dot3(feats, w1_ref[...]) + b1_ref[...])
        h = jnp.sin(_dot3(h, w2_ref[...]) + b2_ref[...])
        h = _dot3(h, w3_ref[...]) + b3_ref[...]
        window = jnp.exp(-t_norm * dec_ref[...]) + DECAY_SHIFT
        sel = jnp.where(j < seq, h[:, :hw], h[:, hw:]) * window
        sel = jnp.where(j == seq, 0.0, sel)
        ts_ref[pl.ds(j0, rows), :] = sel
        return asum + jnp.sum(jnp.abs(sel), axis=0, keepdims=True)

    asum = lax.fori_loop(0, n // rows, fill, jnp.zeros((1, hw), F32))
    inv = 1.0 / (asum + EPS)

    def norm(i, carry):
        r = pl.ds(pl.multiple_of(i * rows, rows), rows)
        o_ref[r, :] = (ts_ref[r, :] * inv).astype(BF16)
        return carry

    lax.fori_loop(0, n // rows, norm, 0)


def _filters_time(seq, bands, w1, b1, w2, b2, w3, b3, dec):
    hw = HY_ORDER * HY_WIDTH
    return pl.pallas_call(
        functools.partial(_filter_kernel, seq=seq),
        out_shape=jax.ShapeDtypeStruct((2 * seq, hw), BF16),
        scratch_shapes=[pltpu.VMEM((2 * seq, hw), F32)],
        compiler_params=pltpu.CompilerParams(vmem_limit_bytes=VMEM_LIMIT),
        name="hy_filter_time",
    )(bands, w1, b1, w2, b2, w3, b3, dec)


def _matmul_kernel(a_ref, b_ref, o_ref):
    o_ref[...] = _dot(a_ref[...], b_ref[...])


def _filter_dft(wf, ts):
    n, hw = ts.shape
    tm = min(512, n)
    return pl.pallas_call(
        _matmul_kernel,
        grid=(n // tm,),
        in_specs=[pl.BlockSpec((tm, n), lambda i: (i, 0)), pl.BlockSpec((n, hw), lambda i: (0, 0))],
        out_specs=pl.BlockSpec((tm, hw), lambda i: (i, 0)),
        out_shape=jax.ShapeDtypeStruct((n, hw), F32),
        compiler_params=_params(("arbitrary",)),
        name="hy_filter_dft",
    )(wf, ts)


def _longconv_kernel(v_ref, g_ref, w_ref, wi_ref, kf_ref, bias_ref, o_ref, acc_ref):
    kt = pl.program_id(1)
    nb = v_ref.shape[0]
    tf = w_ref.shape[0] // 2

    @pl.when(kt == 0)
    def _():
        acc_ref[...] = jnp.zeros_like(acc_ref)

    kr = kf_ref[:tf, :]
    ki = kf_ref[tf:, :]
    first = (lax.broadcasted_iota(jnp.int32, (tf, 1), 0) == 0) & (kt == 0)
    for n in range(nb):
        u = _dot(w_ref[...], v_ref[n])
        ur, ui = u[:tf], u[tf:]
        gr = ur * kr - jnp.where(first, 0.0, ui * ki)
        gi = jnp.where(first, ui * ki, ur * ki + ui * kr)
        acc_ref[n] += _dot(wi_ref[...], jnp.concatenate([gr, gi], axis=0).astype(BF16))

    @pl.when(kt == pl.num_programs(1) - 1)
    def _():
        for n in range(nb):
            vf = v_ref[n].astype(F32)
            y = (acc_ref[n] + vf * bias_ref[...]).astype(BF16)
            o_ref[n] = (g_ref[n] * y).astype(BF16)


CONV_TF = 256
CONV_NB = 4


def _longconv(v_arr, v_col, g_arr, g_col, w, wi, kf, order, bias):
    nbatch, seq, _ = v_arr.shape
    nb = math.gcd(nbatch, CONV_NB)
    tf = min(CONV_TF, seq)
    c = HY_WIDTH
    return pl.pallas_call(
        _longconv_kernel,
        grid=(nbatch // nb, seq // tf),
        in_specs=[
            pl.BlockSpec((nb, seq, c), lambda i, k: (i, 0, v_col)),
            pl.BlockSpec((nb, seq, c), lambda i, k: (i, 0, g_col)),
            pl.BlockSpec((None, 2 * tf, seq), lambda i, k: (k, 0, 0)),
            pl.BlockSpec((None, seq, 2 * tf), lambda i, k: (k, 0, 0)),
            pl.BlockSpec((None, 2 * tf, c), lambda i, k: (k, 0, order)),
            pl.BlockSpec((1, c), lambda i, k: (0, 0)),
        ],
        out_specs=pl.BlockSpec((nb, seq, c), lambda i, k: (i, 0, 0)),
        out_shape=jax.ShapeDtypeStruct((nbatch, seq, c), BF16),
        scratch_shapes=[pltpu.VMEM((nb, seq, c), F32)],
        compiler_params=_params(("arbitrary", "arbitrary")),
        name="hy_longconv",
    )(v_arr, g_arr, w, wi, kf, bias)


def _lane_mask(lo, width, n=LANES):
    lane = lax.broadcasted_iota(jnp.int32, (1, n), 1)
    return (lane >= lo) & (lane < lo + width)


ALIBI_SPLIT = 3
POS_RADIX = 256


def _da_coef(head):
    return 2.0 ** (-8.0 * (head + 1) / DA_HEADS) * math.log2(math.e)


def _alibi_tables(seq):
    pos = jnp.arange(seq, dtype=jnp.int32)
    hi = ((pos // POS_RADIX) * POS_RADIX).astype(F32)[:, None]
    lo = (pos % POS_RADIX).astype(F32)[:, None]
    ones = jnp.ones((seq, 1), F32)
    fq, fk = [], []
    for head in range(DA_HEADS):
        rest = jnp.float32(_da_coef(head))
        pieces = []
        for _ in range(ALIBI_SPLIT):
            piece = rest.astype(BF16).astype(F32)
            pieces.append(piece)
            rest = rest - piece
        cq = jnp.concatenate([ones * c for c in pieces], axis=1)
        fq.append(jnp.concatenate([hi] * ALIBI_SPLIT + [lo] * ALIBI_SPLIT + [cq, cq], axis=1))
        fk.append(jnp.concatenate([-cq, -cq] + [hi] * ALIBI_SPLIT + [lo] * ALIBI_SPLIT, axis=1))
    pad = lambda t: jnp.pad(t, ((0, 0), (0, 0), (0, LANES - 4 * ALIBI_SPLIT))).astype(BF16)
    fk = jnp.stack(fk)
    tq = min(DA_TQ, seq)
    key_tile = (pos // tq)[None, :, None]
    signed = [pad(fk * jnp.where(key_tile < qi, 1.0, jnp.where(key_tile > qi, -1.0, 0.0))) for qi in range(seq // tq)]
    return pad(jnp.stack(fq)), signed


def _da_kernel(q_ref, k_ref, v_ref, fq_ref, fk_ref, lam_ref, g_ref, o_ref, *, qi, lam_init):
    tq = q_ref.shape[1]
    seq = k_ref.shape[1]
    lv = lam_ref[...]
    lam = (jnp.exp(jnp.sum(lv[0:1] * lv[1:2], axis=1, keepdims=True))
           - jnp.exp(jnp.sum(lv[2:3] * lv[3:4], axis=1, keepdims=True)) + lam_init)
    local = jnp.abs(lax.broadcasted_iota(jnp.int32, (tq, tq), 0)
                    - lax.broadcasted_iota(jnp.int32, (tq, tq), 1)).astype(F32)
    slabs = []
    for s in range(DA_WIDTH // LANES):
        cols = slice(s * LANES, (s + 1) * LANES)
        q_slab = q_ref[0, :, cols]
        k_slab = k_ref[0, :, cols]
        v_slab = v_ref[0, :, cols]
        o_slab = jnp.zeros((tq, LANES), F32)
        for hh in range(LANES // HEAD_DIM):
            head = s * (LANES // HEAD_DIM) + hh
            k_aug = jnp.concatenate([k_slab, fk_ref[head]], axis=1)
            mine = _lane_mask(hh * HEAD_DIM, HEAD_DIM)
            v_ones = jnp.where(mine, v_slab, jnp.ones_like(v_slab))
            maps = []
            for c in range(2):
                keep = jnp.where(_lane_mask(hh * HEAD_DIM + c * DA_HALF, DA_HALF), 1.0, 0.0).astype(BF16)
                q_aug = jnp.concatenate([q_slab * keep, fq_ref[head]], axis=1)
                sc = _dot_nt(q_aug, k_aug)
                pieces = [sc[:, qi * tq:(qi + 1) * tq] - _da_coef(head) * local]
                if qi > 0:
                    pieces.insert(0, sc[:, :qi * tq])
                if (qi + 1) * tq < seq:
                    pieces.append(sc[:, (qi + 1) * tq:])
                sc = jnp.concatenate(pieces, axis=1)
                p = jnp.exp2(sc - jnp.max(sc, axis=1, keepdims=True))
                o2 = _dot(p.astype(BF16), v_ones)
                maps.append((o2, o2[:, (1 - hh) * HEAD_DIM:(1 - hh) * HEAD_DIM + 1]))
            diff = maps[0][0] * (1.0 / maps[0][1]) - maps[1][0] * (lam / maps[1][1])
            o_slab = jnp.where(mine, diff, o_slab)
        slabs.append(o_slab)
    o = jnp.concatenate(slabs, axis=1)
    y = o * lax.rsqrt(_head_ssq(o) * (1.0 / HEAD_DIM) + EPS) * g_ref[...]
    o_ref[0] = (y * (1.0 - lam_init)).astype(BF16)


DA_TQ = 512


def _diff_attn(z, fq, fk_signed, lam_vecs, g_tiled, lam_init):
    nb, seq, _ = z.shape
    tq = min(DA_TQ, seq)
    cb = COL_DA // DA_WIDTH
    tiles = []
    for qi in range(seq // tq):
        tiles.append(pl.pallas_call(
            functools.partial(_da_kernel, qi=qi, lam_init=lam_init),
            grid=(nb,),
            in_specs=[
                pl.BlockSpec((1, tq, DA_WIDTH), lambda b, qi=qi: (b, qi, cb)),
                pl.BlockSpec((1, seq, DA_WIDTH), lambda b: (b, 0, cb + 1)),
                pl.BlockSpec((1, seq, DA_WIDTH), lambda b: (b, 0, cb + 2)),
                pl.BlockSpec((DA_HEADS, tq, LANES), lambda b, qi=qi: (0, qi, 0)),
                pl.BlockSpec((DA_HEADS, seq, LANES), lambda b: (0, 0, 0)),
                pl.BlockSpec((4, DA_HALF), lambda b: (0, 0)),
                pl.BlockSpec((1, DA_WIDTH), lambda b: (0, 0)),
            ],
            out_specs=pl.BlockSpec((1, tq, DA_WIDTH), lambda b: (b, 0, 0)),
            out_shape=jax.ShapeDtypeStruct((nb, tq, DA_WIDTH), BF16),
            compiler_params=_params(("arbitrary",)),
            name="diff_attn",
        )(z, z, z, fq, fk_signed[qi], lam_vecs, g_tiled))
    return jnp.concatenate(tiles, axis=1)


def _rope(x, cos, sin):
    w = x.shape[1]
    lane = lax.broadcasted_iota(jnp.int32, (1, w), 1)
    half = ROPE_AXIS // 2
    partner = jnp.where(lane % ROPE_AXIS < half, pltpu.roll(x, w - half, 1), pltpu.roll(x, half, 1))
    return x * cos + partner * sin


def _dup_heads(x):
    first = _lane_mask(0, HEAD_DIM)
    a = jnp.where(first, x, 0.0)
    b = jnp.where(first, 0.0, x)
    return jnp.concatenate([a + pltpu.roll(a, HEAD_DIM, 1), b + pltpu.roll(b, HEAD_DIM, 1)], axis=1)


def _with_ones(v):
    first = _lane_mask(0, HEAD_DIM)
    swapped = pltpu.roll(v, HEAD_DIM, 1)
    return jnp.concatenate([jnp.where(first, v, 1.0), jnp.where(first, 1.0, swapped),
                            jnp.where(first, swapped, 1.0), jnp.where(first, 1.0, v)], axis=1)


def _gqa_kernel(zq_ref, zkv_ref, cosq_ref, sinq_ref, cosk_ref, sink_ref, gq_ref, gk_ref, o_ref, kd_ref, vd_ref):
    tq = zq_ref.shape[1]
    kvw = GQ_KV * HEAD_DIM
    per_slab = LANES // HEAD_DIM
    group = GQ_HEADS // GQ_KV

    @pl.when(pl.program_id(1) == 0)
    def _():
        k = zkv_ref[0, :, :kvw].astype(F32)
        kn = k * lax.rsqrt(_head_ssq(k) * (1.0 / HEAD_DIM) + EPS) * gk_ref[...]
        kd_ref[...] = _dup_heads(_rope(kn, cosk_ref[...], sink_ref[...])).astype(BF16)
        vd_ref[...] = _with_ones(zkv_ref[0, :, kvw:].astype(F32)).astype(BF16)

    q = zq_ref[0].astype(F32)
    qn = q * lax.rsqrt(_head_ssq(q) * (1.0 / HEAD_DIM) + EPS) * gq_ref[...]
    qr = (_rope(qn, cosq_ref[...], sinq_ref[...]) * (HEAD_DIM ** -0.5 * math.log2(math.e))).astype(BF16)
    slabs = []
    for s in range(GQ_WIDTH // LANES):
        q_slab = qr[:, s * LANES:(s + 1) * LANES]
        kv = (s * per_slab) // group
        k_dup = kd_ref[:, kv * LANES:(kv + 1) * LANES]
        o_slab = jnp.zeros((tq, LANES), F32)
        for hh in range(per_slab):
            keep = jnp.where(_lane_mask(hh * HEAD_DIM, HEAD_DIM), 1.0, 0.0).astype(BF16)
            sc = _dot_nt(q_slab * keep, k_dup)
            p = jnp.exp2(sc - jnp.max(sc, axis=1, keepdims=True))
            v_ones = vd_ref[:, (kv * per_slab + hh) * LANES:(kv * per_slab + hh + 1) * LANES]
            o2 = _dot(p.astype(BF16), v_ones)
            sums = o2[:, (1 - hh) * HEAD_DIM:(1 - hh) * HEAD_DIM + 1]
            o_slab = jnp.where(_lane_mask(hh * HEAD_DIM, HEAD_DIM), o2 * (1.0 / sums), o_slab)
        slabs.append(o_slab)
    o_ref[0] = jnp.concatenate(slabs, axis=1).astype(BF16)


def _gqa_attn(z, cos, sin, gq, gk):
    nb, seq, _ = z.shape
    tq = min(512, seq)
    kvw = GQ_KV * HEAD_DIM
    return pl.pallas_call(
        _gqa_kernel,
        grid=(nb, seq // tq),
        in_specs=[
            pl.BlockSpec((1, tq, GQ_WIDTH), lambda b, i: (b, i, COL_GQ // GQ_WIDTH)),
            pl.BlockSpec((1, seq, 2 * kvw), lambda b, i: (b, 0, (COL_GQ + GQ_WIDTH) // (2 * kvw))),
            pl.BlockSpec((tq, GQ_WIDTH), lambda b, i: (i, 0)),
            pl.BlockSpec((tq, GQ_WIDTH), lambda b, i: (i, 0)),
            pl.BlockSpec((seq, kvw), lambda b, i: (0, 0)),
            pl.BlockSpec((seq, kvw), lambda b, i: (0, 0)),
            pl.BlockSpec((1, GQ_WIDTH), lambda b, i: (0, 0)),
            pl.BlockSpec((1, kvw), lambda b, i: (0, 0)),
        ],
        out_specs=pl.BlockSpec((1, tq, GQ_WIDTH), lambda b, i: (b, i, 0)),
        out_shape=jax.ShapeDtypeStruct((nb, seq, GQ_WIDTH), BF16),
        scratch_shapes=[pltpu.VMEM((seq, 2 * kvw), BF16), pltpu.VMEM((seq, 4 * kvw), BF16)],
        compiler_params=_params(("arbitrary", "arbitrary")),
        name="gqa_attn",
    )(z, z, cos, sin, cos, sin, gq, gk)


def _merge_kernel(ya_ref, yb_ref, yc_ref, zg_ref, x_ref, mod_ref, wa_ref, wb_ref, wc_ref, wo_ref,
                  lg_ref, lb_ref, *rest):
    d = D_MODEL
    m = (jnp.tanh(zg_ref[0, :, 0:d].astype(F32)) + 1.0) * _dot(ya_ref[0], wa_ref[...])
    m = m + (jnp.tanh(zg_ref[0, :, d:2 * d].astype(F32)) + 1.0) * _dot(yb_ref[0], wb_ref[...])
    m = m + (jnp.tanh(zg_ref[0, :, 2 * d:3 * d].astype(F32)) + 1.0) * _dot(yc_ref[0], wc_ref[...])
    y = _dot(m.astype(BF16), wo_ref[...])
    gate1 = 1.0 + mod_ref[0, 2:3, :]
    xn = _ln(DEEPNORM_ALPHA * x_ref[0] + gate1 * y) * lg_ref[...] + lb_ref[...]
    h = _ln(xn) * (1.0 + mod_ref[0, 4:5, :]) + mod_ref[0, 3:4, :]
    if len(rest) == 2:
        xo_ref, h_ref = rest
    else:
        wr_ref, br_ref, xo_ref, h_ref, comb_ref = rest
        lane = lax.broadcasted_iota(jnp.int32, (1, LANES), 1).astype(F32)
        neg = -jnp.inf
        logits = jnp.where(lane < N_EXPERTS, _dot3(h, wr_ref[...]) + br_ref[...], neg)
        v1 = jnp.max(logits, axis=1, keepdims=True)
        i1 = jnp.min(jnp.where(logits == v1, lane, float(LANES)), axis=1, keepdims=True)
        others = jnp.where(lane == i1, neg, logits)
        v2 = jnp.max(others, axis=1, keepdims=True)
        i2 = jnp.min(jnp.where(others == v2, lane, float(LANES)), axis=1, keepdims=True)
        e = jnp.exp(v2 - v1)
        w1 = 1.0 / (1.0 + e)
        comb_ref[0] = jnp.where(lane == i1, w1, 0.0) + jnp.where(lane == i2, e * w1, 0.0)
    xo_ref[0] = xn
    h_ref[0] = h.astype(BF16)


def _merge(ya, yb, yc, z, x, mod, wa, wb, wc, wo, lg, lb, router=None):
    nb, seq, d = x.shape
    tm = min(512, seq)
    full = lambda shape: pl.BlockSpec(shape, lambda b, i: (0,) * len(shape))
    row = lambda w, col=0: pl.BlockSpec((1, tm, w), lambda b, i: (b, i, col))
    in_specs = [
        row(HY_WIDTH), row(DA_WIDTH), row(GQ_WIDTH), row(GATE_COLS, COL_GATE // GATE_COLS), row(d),
        pl.BlockSpec((1, 6, d), lambda b, i: (b, 0, 0)),
        full((HY_WIDTH, d)), full((DA_WIDTH, d)), full((GQ_WIDTH, d)), full((d, d)),
        full((1, d)), full((1, d)),
    ]
    out_specs = [row(d), row(d)]
    out_shape = [jax.ShapeDtypeStruct((nb, seq, d), F32), jax.ShapeDtypeStruct((nb, seq, d), BF16)]
    args = [ya, yb, yc, z, x, mod, wa, wb, wc, wo, lg, lb]
    if router is not None:
        in_specs += [full((d, LANES)), full((1, LANES))]
        out_specs.append(row(LANES))
        out_shape.append(jax.ShapeDtypeStruct((nb, seq, LANES), F32))
        args += list(router)
    return pl.pallas_call(
        _merge_kernel,
        grid=(nb, seq // tm),
        in_specs=in_specs,
        out_specs=out_specs,
        out_shape=out_shape,
        compiler_params=_params(("arbitrary", "arbitrary")),
        name="merge_outproj",
    )(*args)


def _ffn_kernel(h_ref, wg_ref, wu_ref, wd_ref, x_ref, mod_ref, lg_ref, lb_ref, o_ref, acc_ref):
    j = pl.program_id(2)

    @pl.when(j == 0)
    def _():
        acc_ref[...] = jnp.zeros_like(acc_ref)

    h = h_ref[0]
    g = _dot(h, wg_ref[...])
    u = _dot(h, wu_ref[...])
    a = g * _sigmoid(g) * u
    acc_ref[...] += _dot(a.astype(BF16), wd_ref[...])

    @pl.when(j == pl.num_programs(2) - 1)
    def _():
        gate2 = 1.0 + mod_ref[0, 5:6, :]
        o_ref[0] = _ln(DEEPNORM_ALPHA * x_ref[0] + gate2 * acc_ref[...]) * lg_ref[...] + lb_ref[...]


def _ffn(h, w_up, w_down, x, mod, lg, lb):
    nb, seq, d = x.shape
    ff = w_down.shape[0]
    tm = min(1024, seq)
    tf = 256
    nf = ff // tf
    row = lambda w: pl.BlockSpec((1, tm, w), lambda b, i, j: (b, i, 0))
    vec = pl.BlockSpec((1, d), lambda b, i, j: (0, 0))
    return pl.pallas_call(
        _ffn_kernel,
        grid=(nb, seq // tm, nf),
        in_specs=[
            row(d),
            pl.BlockSpec((d, tf), lambda b, i, j: (0, j)),
            pl.BlockSpec((d, tf), lambda b, i, j: (0, nf + j)),
            pl.BlockSpec((tf, d), lambda b, i, j: (j, 0)),
            row(d),
            pl.BlockSpec((1, 6, d), lambda b, i, j: (b, 0, 0)),
            vec, vec,
        ],
        out_specs=row(d),
        out_shape=jax.ShapeDtypeStruct((nb, seq, d), F32),
        scratch_shapes=[pltpu.VMEM((tm, d), F32)],
        compiler_params=_params(("arbitrary",) * 3),
        name="swiglu_dense",
    )(h, w_up, w_up, w_down, x, mod, lg, lb)


MOE_TOKENS = 1024
MOE_CHUNK = 288
MOE_SLOT = 384


def _moe_kernel(h_ref, comb_ref, wg_ref, wu_ref, wd_ref, x_ref, mod_ref, lg_ref, lb_ref, o_ref,
                posc_ref, posr_ref, cnt_ref, ys_ref):
    e = pl.program_id(2)
    tm = h_ref.shape[1]
    n_exp = ys_ref.shape[0] // MOE_SLOT
    lane = lax.broadcasted_iota(jnp.int32, (1, LANES), 1)

    @pl.when(e == 0)
    def _():
        comb = comb_ref[0]
        sel = jnp.where(comb > 0.0, 1.0, 0.0)
        r = lax.broadcasted_iota(jnp.int32, (tm, tm), 0)
        c = lax.broadcasted_iota(jnp.int32, (tm, tm), 1)
        before = jnp.where(c < r, 1.0, 0.0).astype(BF16)
        rank = _dot(before, sel.astype(BF16))
        posc = jnp.where(comb > 0.0, rank, -1.0)
        posc_ref[...] = posc
        posr_ref[...] = posc.T
        cnt_ref[...] = jnp.sum(sel, axis=0, keepdims=True)
        o_ref[0] = jnp.zeros(o_ref.shape[1:], F32)
        ys_ref[...] = jnp.zeros_like(ys_ref)

    mine = lane == e
    pos_row = posr_ref[pl.ds(e, 1), :]
    count = jnp.sum(jnp.where(mine, cnt_ref[...], 0.0))

    def expert_rows(c):
        slot_r = (c * MOE_CHUNK + lax.broadcasted_iota(jnp.int32, (MOE_CHUNK, 1), 0)).astype(F32)
        gather = jnp.where(pos_row == slot_r, 1.0, 0.0).astype(BF16)
        xs = _dot(gather, h_ref[0]).astype(BF16)
        g = _dot(xs, wg_ref[...])
        u = _dot(xs, wu_ref[...])
        a = (g * _sigmoid(g) * u).astype(BF16)
        return _dot(a, wd_ref[...]).astype(BF16)

    ys_ref[pl.ds(pl.multiple_of(e * MOE_SLOT, MOE_SLOT), MOE_CHUNK), :] = expert_rows(0)

    for c in range(1, -(-tm // MOE_CHUNK)):
        @pl.when(count > c * MOE_CHUNK)
        def _():
            pos_col = jnp.sum(jnp.where(mine, posc_ref[...], 0.0), axis=1, keepdims=True)
            w_col = jnp.sum(jnp.where(mine, comb_ref[0], 0.0), axis=1, keepdims=True)
            slot_c = (c * MOE_CHUNK + lax.broadcasted_iota(jnp.int32, (1, MOE_CHUNK), 1)).astype(F32)
            scatter = jnp.where(pos_col == slot_c, 1.0, 0.0).astype(BF16)
            o_ref[0] += w_col * _dot(scatter, expert_rows(c))

    @pl.when(e == pl.num_programs(2) - 1)
    def _():
        slot_c = lax.broadcasted_iota(jnp.int32, (1, MOE_SLOT), 1).astype(F32)
        slot_c = jnp.where(slot_c < MOE_CHUNK, slot_c, -2.0)
        f = o_ref[0]
        for e0 in range(0, n_exp, 2):
            weights = jnp.concatenate(
                [jnp.where(posc_ref[:, ee:ee + 1] == slot_c, comb_ref[0, :, ee:ee + 1], 0.0).astype(BF16)
                 for ee in (e0, e0 + 1)], axis=1)
            f = f + _dot(weights, ys_ref[e0 * MOE_SLOT:(e0 + 2) * MOE_SLOT, :])
        gate2 = 1.0 + mod_ref[0, 5:6, :]
        o_ref[0] = _ln(DEEPNORM_ALPHA * x_ref[0] + gate2 * f) * lg_ref[...] + lb_ref[...]


def _moe(h, comb, w_up, w_down, x, mod, lg, lb):
    nb, seq, d = x.shape
    ne, ff, _ = w_down.shape
    tm = min(MOE_TOKENS, seq)
    row = lambda w: pl.BlockSpec((1, tm, w), lambda b, i, e: (b, i, 0))
    vec = pl.BlockSpec((1, d), lambda b, i, e: (0, 0))
    return pl.pallas_call(
        _moe_kernel,
        grid=(nb, seq // tm, ne),
        in_specs=[
            row(d), row(LANES),
            pl.BlockSpec((None, d, ff), lambda b, i, e: (e, 0, 0)),
            pl.BlockSpec((None, d, ff), lambda b, i, e: (e, 0, 1)),
            pl.BlockSpec((None, ff, d), lambda b, i, e: (e, 0, 0)),
            row(d),
            pl.BlockSpec((1, 6, d), lambda b, i, e: (b, 0, 0)),
            vec, vec,
        ],
        out_specs=row(d),
        out_shape=jax.ShapeDtypeStruct((nb, seq, d), F32),
        scratch_shapes=[pltpu.VMEM((tm, LANES), F32), pltpu.VMEM((LANES, tm), F32), pltpu.VMEM((1, LANES), F32),
                        pltpu.VMEM((ne * MOE_SLOT, d), BF16)],
        compiler_params=_params(("arbitrary",) * 3),
        name="swiglu_routed",
    )(h, comb, w_up, w_up, w_down, x, mod, lg, lb)


def _dft_tables(seq):
    n = 2 * seq
    tf = min(CONV_TF, seq)
    nk = seq // tf
    s = jnp.arange(n, dtype=jnp.int32)[None, :]

    def phase(mult):
        ang = ((mult * s) % n).astype(F32) * (2.0 * math.pi / n)
        return jnp.cos(ang), jnp.sin(ang)

    ca, sa = phase(jnp.arange(nk, dtype=jnp.int32)[:, None] * tf)
    cb, sb = phase(jnp.arange(tf, dtype=jnp.int32)[:, None])
    cos = ca[:, None, :] * cb[None] - sa[:, None, :] * sb[None]
    msin = -(sa[:, None, :] * cb[None] + ca[:, None, :] * sb[None])
    k = (jnp.arange(nk, dtype=jnp.int32)[:, None] * tf + jnp.arange(tf, dtype=jnp.int32)[None, :])[:, :, None]
    msin = jnp.where(k == 0, jnp.where(s % 2 == 0, 1.0, -1.0)[None], msin)
    fwd = jnp.concatenate([cos, msin], axis=1)
    scale = jnp.where(k == 0, 1.0 / n, 2.0 / n).astype(F32)
    inv = jnp.concatenate([cos[:, :, :seq] * scale, msin[:, :, :seq] * scale], axis=1).transpose(0, 2, 1)
    return fwd.astype(BF16), inv.astype(BF16)


def _rope_tables(seq):
    t = jnp.arange(seq, dtype=jnp.int32)
    inv = ROPE_THETA ** (-jnp.arange(0, ROPE_AXIS, 2, dtype=F32) / ROPE_AXIS)
    ang_r = (t // GRID_W).astype(F32)[:, None] * inv[None, :]
    ang_c = (t % GRID_W).astype(F32)[:, None] * inv[None, :]
    cos = jnp.concatenate([jnp.cos(ang_r)] * 2 + [jnp.cos(ang_c)] * 2, axis=1)
    sin = jnp.concatenate([-jnp.sin(ang_r), jnp.sin(ang_r), -jnp.sin(ang_c), jnp.sin(ang_c)], axis=1)
    return jnp.tile(cos, (1, GQ_HEADS)), jnp.tile(sin, (1, GQ_HEADS))


def _pad_to(a, rows, cols):
    return jnp.pad(a, ((0, rows - a.shape[0]), (0, cols - a.shape[1])))


def _prepare(seq, p):
    fwd, inv = _dft_tables(seq)
    cos, sin = _rope_tables(seq)
    bands = jnp.linspace(1e-4, N_BANDS - 1, N_BANDS, dtype=F32)
    bands_row = jnp.zeros((1, LANES), F32).at[0, 1:1 + N_BANDS].set(bands).at[0, 1 + N_BANDS:1 + 2 * N_BANDS].set(bands)
    deltas = jnp.abs(jnp.linspace(MIN_DECAY, MAX_DECAY, HY_WIDTH, dtype=F32))
    dec = jnp.tile(deltas, HY_ORDER)[None, :]
    da_scale = DA_HALF ** -0.5 * math.log2(math.e)
    fq, fk = _alibi_tables(seq)
    layers = []
    for l in range(DEPTH):
        w_in, b_in = p['w_in'][l], p['b_in'][l]
        a0, b0, c0, g0 = 0, HY_COLS, HY_COLS + DA_COLS, HY_COLS + DA_COLS + GQ_COLS
        w_cols = [w_in[:, g0:] * 0.5, w_in[:, a0:b0], w_in[:, b0:b0 + DA_WIDTH] * da_scale, w_in[:, b0 + DA_WIDTH:c0], w_in[:, c0:g0]]
        b_cols = [b_in[g0:] * 0.5, b_in[a0:b0], b_in[b0:b0 + DA_WIDTH] * da_scale, b_in[b0 + DA_WIDTH:c0], b_in[c0:g0]]
        ts = _filters_time(
            seq, bands_row,
            _pad_to(p['hy_f_w1'][l], LANES, LANES), _pad_to(p['hy_f_b1'][l][None, :], 1, LANES),
            _pad_to(p['hy_f_w2'][l], LANES, LANES), _pad_to(p['hy_f_b2'][l][None, :], 1, LANES),
            _pad_to(p['hy_f_w3'][l], LANES, 2 * HY_ORDER * HY_WIDTH), p['hy_f_b3'][l][None, :], dec)
        kf = _filter_dft(fwd.reshape(2 * seq, 2 * seq), ts).reshape(fwd.shape[0], -1, HY_ORDER * HY_WIDTH)
        lay = dict(
            w_in=jnp.concatenate(w_cols, axis=1).astype(BF16),
            b_in=jnp.concatenate(b_cols)[None, :],
            conv_w=p['hy_conv_w'][l], conv_b=p['hy_conv_b'][l][None, :],
            kf=kf, hy_bias=p['hy_bias'][l],
            lam=jnp.stack([p['da_lam_q1'][l], p['da_lam_k1'][l], p['da_lam_q2'][l], p['da_lam_k2'][l]]),
            lam_init=0.8 - 0.6 * math.exp(-0.3 * l),
            subln=jnp.tile(p['da_subln_g'][l], DA_HEADS)[None, :],
            gq=jnp.tile(p['gq_qnorm_g'][l], GQ_HEADS)[None, :],
            gk=jnp.tile(p['gq_knorm_g'][l], GQ_KV)[None, :],
            wa=(p['w_br_a'][l] * 0.5).astype(BF16), wb=(p['w_br_b'][l] * 0.5).astype(BF16),
            wc=(p['w_br_c'][l] * 0.5).astype(BF16),
            wo=p['w_out'][l].astype(BF16),
            ln1_g=p['ln1_g'][l][None, :], ln1_b=p['ln1_b'][l][None, :],
            ln2_g=p['ln2_g'][l][None, :], ln2_b=p['ln2_b'][l][None, :],
        )
        if l % 2 == 0:
            lay.update(w_up=p['ffn_w_up'][l // 2].astype(BF16), w_down=p['ffn_w_down'][l // 2].astype(BF16))
        else:
            lay.update(w_up=p['moe_w_up'][l // 2].astype(BF16), w_down=p['moe_w_down'][l // 2].astype(BF16),
                       router=(_pad_to(p['moe_w_router'][l // 2], D_MODEL, LANES),
                               _pad_to(p['moe_b_router'][l // 2][None, :], 1, LANES)))
        layers.append(lay)
    return dict(fwd=fwd, inv=inv, cos=cos, sin=sin, fq=fq, fk=fk, layers=layers)


def _trunk(x, c, p, prep):
    mods = _ada(c, p['w_ada'], p['b_ada'])
    nb = x.shape[0]
    for l, lay in enumerate(prep['layers']):
        mod = mods[l].reshape(nb, 6, D_MODEL)
        z = _inproj(x, mod, lay['w_in'], lay['b_in'], lay['conv_w'], lay['conv_b'])
        hy = COL_HY // HY_WIDTH
        u = _longconv(z, hy + 2, z, hy, prep['fwd'], prep['inv'], lay['kf'], 0, lay['hy_bias'][0:1])
        ya = _longconv(u, 0, z, hy + 1, prep['fwd'], prep['inv'], lay['kf'], 1, lay['hy_bias'][1:2])
        yb = _diff_attn(z, prep['fq'], prep['fk'], lay['lam'], lay['subln'], lay['lam_init'])
        yc = _gqa_attn(z, prep['cos'], prep['sin'], lay['gq'], lay['gk'])
        merged = _merge(ya, yb, yc, z, x, mod, lay['wa'], lay['wb'], lay['wc'], lay['wo'],
                        lay['ln1_g'], lay['ln1_b'], lay.get('router'))
        if l % 2 == 0:
            x, h = merged
            x = _ffn(h, lay['w_up'], lay['w_down'], x, mod, lay['ln2_g'], lay['ln2_b'])
        else:
            x, h, comb = merged
            x = _moe(h, comb, lay['w_up'], lay['w_down'], x, mod, lay['ln2_g'], lay['ln2_b'])
    return x


def kernel(x_prompt, x_sample, c_prompt, c_sample, w_ada, b_ada, w_in, b_in, hy_conv_w, hy_conv_b, hy_f_w1, hy_f_b1, hy_f_w2, hy_f_b2, hy_f_w3, hy_f_b3, hy_bias, da_lam_q1, da_lam_k1, da_lam_q2, da_lam_k2, da_subln_g, gq_qnorm_g, gq_knorm_g, w_br_a, w_br_b, w_br_c, w_out, ln1_g, ln1_b, ffn_w_up, ffn_w_down, moe_w_router, moe_b_router, moe_w_up, moe_w_down, ln2_g, ln2_b):
    p = dict(w_ada=w_ada, b_ada=b_ada, w_in=w_in, b_in=b_in,
             hy_conv_w=hy_conv_w, hy_conv_b=hy_conv_b, hy_f_w1=hy_f_w1, hy_f_b1=hy_f_b1,
             hy_f_w2=hy_f_w2, hy_f_b2=hy_f_b2, hy_f_w3=hy_f_w3, hy_f_b3=hy_f_b3, hy_bias=hy_bias,
             da_lam_q1=da_lam_q1, da_lam_k1=da_lam_k1, da_lam_q2=da_lam_q2, da_lam_k2=da_lam_k2,
             da_subln_g=da_subln_g, gq_qnorm_g=gq_qnorm_g, gq_knorm_g=gq_knorm_g,
             w_br_a=w_br_a, w_br_b=w_br_b, w_br_c=w_br_c, w_out=w_out, ln1_g=ln1_g, ln1_b=ln1_b,
             ffn_w_up=ffn_w_up, ffn_w_down=ffn_w_down, moe_w_router=moe_w_router,
             moe_b_router=moe_b_router, moe_w_up=moe_w_up, moe_w_down=moe_w_down,
             ln2_g=ln2_g, ln2_b=ln2_b)
    assert x_prompt.shape[1] == x_sample.shape[1]
    prep = _prepare(x_prompt.shape[1], p)
    return (_trunk(x_prompt, c_prompt, p, prep), _trunk(x_sample, c_sample, p, prep))
```

```python
import functools
import math

import jax
import jax.numpy as jnp
from jax import lax
from jax.experimental import pallas as pl
from jax.experimental.pallas import tpu as pltpu

F32 = jnp.float32
BF16 = jnp.bfloat16

D_MODEL = 1024
DEPTH = 2
GRID_W = 64
HEAD_DIM = 64
HY_WIDTH = D_MODEL // 4
HY_ORDER = 2
SHORT_CONV = 3
N_BANDS = 16
FILTER_HID = 64
DECAY_TARGET = 1e-2
MIN_DECAY = math.log(DECAY_TARGET) / 1.5
MAX_DECAY = math.log(DECAY_TARGET) / 0.3
DECAY_SHIFT = 0.05
DA_HEADS = 4
DA_HALF = HEAD_DIM // 2
DA_WIDTH = DA_HEADS * HEAD_DIM
GQ_HEADS = 8
GQ_KV = 2
GQ_WIDTH = GQ_HEADS * HEAD_DIM
ROPE_AXIS = HEAD_DIM // 2
ROPE_THETA = 10000.0
HY_COLS = (HY_ORDER + 1) * HY_WIDTH
DA_COLS = 3 * DA_WIDTH
GQ_COLS = GQ_WIDTH + 2 * GQ_KV * HEAD_DIM
GATE_COLS = 3 * D_MODEL
IN_COLS = HY_COLS + DA_COLS + GQ_COLS + GATE_COLS
D_FF = 256 * ((8 * D_MODEL // 3 + 255) // 256)
N_EXPERTS = 8
MOE_FF = D_FF // 2
DEEPNORM_ALPHA = (2.0 * DEPTH) ** 0.25
EPS = 1e-5

LANES = 128
VMEM_LIMIT = 56 * 1024 * 1024

COL_GATE = 0
COL_HY = GATE_COLS
COL_DA = COL_HY + HY_COLS
COL_GQ = COL_DA + DA_COLS


def _params(sem):
    return pltpu.CompilerParams(dimension_semantics=sem, vmem_limit_bytes=VMEM_LIMIT)


def _dot(a, b):
    return jnp.dot(a, b, preferred_element_type=F32)


def _dot_nt(a, b):
    return lax.dot_general(a, b, (((1,), (1,)), ((), ())), preferred_element_type=F32)


def _split(x):
    hi = x.astype(BF16)
    lo = (x - hi.astype(F32)).astype(BF16)
    return hi, lo


def _dot3(a, b):
    ah, al = _split(a)
    bh, bl = _split(b)
    return _dot(ah, bh) + (_dot(ah, bl) + _dot(al, bh))


def _sigmoid(x):
    return 0.5 * jnp.tanh(0.5 * x) + 0.5


def _ln(x):
    mu = jnp.mean(x, axis=-1, keepdims=True)
    xc = x - mu
    var = jnp.mean(xc * xc, axis=-1, keepdims=True)
    return xc * lax.rsqrt(var + EPS)


def _head_ssq(x):
    w = x.shape[1]
    r = lax.broadcasted_iota(jnp.int32, (w, w), 0) // HEAD_DIM
    c = lax.broadcasted_iota(jnp.int32, (w, w), 1) // HEAD_DIM
    ones = jnp.where(r == c, 1.0, 0.0).astype(BF16)
    hi, lo = _split(x * x)
    return _dot(hi, ones) + _dot(lo, ones)


def _ada_kernel(c_ref, w_ref, b_ref, o_ref):
    c = c_ref[...]
    o_ref[...] = _dot3(c * _sigmoid(c), w_ref[...]) + b_ref[...]


def _ada(c, w_ada, b_ada):
    nb = c.shape[0]
    tn = 1536
    return pl.pallas_call(
        _ada_kernel,
        grid=(DEPTH, 6 * D_MODEL // tn),
        in_specs=[
            pl.BlockSpec((nb, D_MODEL), lambda l, j: (0, 0)),
            pl.BlockSpec((None, D_MODEL, tn), lambda l, j: (l, 0, j)),
            pl.BlockSpec((None, 1, tn), lambda l, j: (l, 0, j)),
        ],
        out_specs=pl.BlockSpec((None, nb, tn), lambda l, j: (l, 0, j)),
        out_shape=jax.ShapeDtypeStruct((DEPTH, nb, 6 * D_MODEL), F32),
        compiler_params=_params(("arbitrary", "arbitrary")),
        name="ada_mod",
    )(c, w_ada, b_ada.reshape(DEPTH, 1, 6 * D_MODEL))


LN_ROWS = 256
INPROJ_TN = 768
INPROJ_STEPS = IN_COLS // INPROJ_TN
HY_TILE = COL_HY // INPROJ_TN


def _inproj_kernel(x_ref, mod_ref, w_ref, b_ref, cw_ref, cb_ref, o_ref, ha_ref, hb_ref):
    b = pl.program_id(0)
    j = pl.program_id(1)
    seq = x_ref.shape[1]
    chunks = seq // LN_ROWS
    assert chunks <= INPROJ_STEPS + 1

    @pl.when((b == 0) & (j == 0))
    def _():
        hb_ref[...] = jnp.zeros_like(hb_ref)

    def step(h_ln, h_mm, ln_chunks, conv, col):
        shift = mod_ref[0, 0:1, :]
        scale = 1.0 + mod_ref[0, 1:2, :]
        for c in ln_chunks:
            r = pl.ds(pl.multiple_of(c * LN_ROWS, LN_ROWS), LN_ROWS)
            h_ln[r, :] = (_ln(x_ref[0, r, :]) * scale + shift).astype(BF16)
        cols = pl.ds(pl.multiple_of(col * INPROJ_TN, INPROJ_TN), INPROJ_TN)
        z = _dot(h_mm[...], w_ref[:, cols]) + b_ref[:, cols]
        if conv:
            row = lax.broadcasted_iota(jnp.int32, z.shape, 0)
            prev = jnp.where(row == 0, 0.0, pltpu.roll(z, 1, 0))
            nxt = jnp.where(row == seq - 1, 0.0, pltpu.roll(z, seq - 1, 0))
            z = cb_ref[...] + prev * cw_ref[0:1, :] + z * cw_ref[1:2, :] + nxt * cw_ref[2:3, :]
        o_ref[0] = z.astype(BF16)

    first = [0] + ([INPROJ_STEPS] if chunks > INPROJ_STEPS else [])
    later = [jnp.minimum(j, chunks - 1)]
    for parity, (h_ln, h_mm) in enumerate(((ha_ref, hb_ref), (hb_ref, ha_ref))):
        even = b % 2 == parity
        pl.when(even & (j == 0))(functools.partial(step, h_ln, h_mm, first, False, 0))
        pl.when(even & (j == HY_TILE))(functools.partial(step, h_ln, h_mm, later, True, HY_TILE))
        pl.when(even & (j != 0) & (j != HY_TILE))(functools.partial(step, h_ln, h_mm, later, False, j))


def _inproj(x, mod, w, b, conv_w, conv_b):
    nb, seq, _ = x.shape
    tn = INPROJ_TN
    assert HY_COLS == tn and COL_HY % tn == 0 and HY_TILE != 0 and seq % LN_ROWS == 0
    cur = lambda i, j: (jnp.minimum(i, nb - 1), 0, 0)
    return pl.pallas_call(
        _inproj_kernel,
        grid=(nb + 1, INPROJ_STEPS),
        in_specs=[
            pl.BlockSpec((1, seq, D_MODEL), cur),
            pl.BlockSpec((1, 6, D_MODEL), cur),
            pl.BlockSpec((D_MODEL, IN_COLS), lambda i, j: (0, 0), pipeline_mode=pl.Buffered(1)),
            pl.BlockSpec((1, IN_COLS), lambda i, j: (0, 0)),
            pl.BlockSpec((SHORT_CONV, tn), lambda i, j: (0, 0)),
            pl.BlockSpec((1, tn), lambda i, j: (0, 0)),
        ],
        out_specs=pl.BlockSpec((1, seq, tn), lambda i, j: (jnp.maximum(i - 1, 0), 0, jnp.where(i == 0, 0, j))),
        out_shape=jax.ShapeDtypeStruct((nb, seq, IN_COLS), BF16),
        scratch_shapes=[pltpu.VMEM((seq, D_MODEL), BF16), pltpu.VMEM((seq, D_MODEL), BF16)],
        compiler_params=_params(("arbitrary", "arbitrary")),
        name="ln_inproj",
    )(x, mod, w, b, conv_w, conv_b)


FILT_ROWS = 512


def _filter_kernel(bands_ref, w1_ref, b1_ref, w2_ref, b2_ref, w3_ref, b3_ref, dec_ref, o_ref, ts_ref, *, seq):
    n = 2 * seq
    rows = min(FILT_ROWS, n)
    hw = HY_ORDER * HY_WIDTH
    lane = lax.broadcasted_iota(jnp.int32, (1, LANES), 1)

    def fill(i, asum):
        j0 = pl.multiple_of(i * rows, rows)
        j = j0 + lax.broadcasted_iota(jnp.int32, (rows, 1), 0)
        t = jnp.where(j < seq, j, n - j).astype(F32)
        t_norm = t / max(seq - 1, 1)
        ang = (2.0 * math.pi / seq) * t * bands_ref[...]
        feats = jnp.where(lane == 0, t_norm,
                          jnp.where(lane <= N_BANDS, jnp.cos(ang),
                                    jnp.where(lane <= 2 * N_BANDS, -jnp.sin(ang), 0.0)))
        h = jnp.sin(_dot3(feats, w1_ref[...]) + b1_ref[...])
        h = jnp.sin(_dot3(h, w2_ref[...]) + b2_ref[...])
        h = _dot3(h, w3_ref[...]) + b3_ref[...]
        window = jnp.exp(-t_norm * dec_ref[...]) + DECAY_SHIFT
        sel = jnp.where(j < seq, h[:, :hw], h[:, hw:]) * window
        sel = jnp.where(j == seq, 0.0, sel)
        ts_ref[pl.ds(j0, rows), :] = sel
        return asum + jnp.sum(jnp.abs(sel), axis=0, keepdims=True)

    asum = lax.fori_loop(0, n // rows, fill, jnp.zeros((1, hw), F32))
    inv = 1.0 / (asum + EPS)

    def norm(i, carry):
        r = pl.ds(pl.multiple_of(i * rows, rows), rows)
        o_ref[r, :] = (ts_ref[r, :] * inv).astype(BF16)
        return carry

    lax.fori_loop(0, n // rows, norm, 0)


def _filters_time(seq, bands, w1, b1, w2, b2, w3, b3, dec):
    hw = HY_ORDER * HY_WIDTH
    return pl.pallas_call(
        functools.partial(_filter_kernel, seq=seq),
        out_shape=jax.ShapeDtypeStruct((2 * seq, hw), BF16),
        scratch_shapes=[pltpu.VMEM((2 * seq, hw), F32)],
        compiler_params=pltpu.CompilerParams(vmem_limit_bytes=VMEM_LIMIT),
        name="hy_filter_time",
    )(bands, w1, b1, w2, b2, w3, b3, dec)


def _matmul_kernel(a_ref, b_ref, o_ref):
    o_ref[...] = _dot(a_ref[...], b_ref[...])


def _filter_dft(wf, ts):
    n, hw = ts.shape
    tm = min(512, n)
    return pl.pallas_call(
        _matmul_kernel,
        grid=(n // tm,),
        in_specs=[pl.BlockSpec((tm, n), lambda i: (i, 0)), pl.BlockSpec((n, hw), lambda i: (0, 0))],
        out_specs=pl.BlockSpec((tm, hw), lambda i: (i, 0)),
        out_shape=jax.ShapeDtypeStruct((n, hw), F32),
        compiler_params=_params(("arbitrary",)),
        name="hy_filter_dft",
    )(wf, ts)


def _longconv_kernel(v_ref, g_ref, w_ref, wi_ref, kf_ref, bias_ref, o_ref, acc_ref):
    kt = pl.program_id(1)
    nb = v_ref.shape[0]
    tf = w_ref.shape[0] // 2

    @pl.when(kt == 0)
    def _():
        acc_ref[...] = jnp.zeros_like(acc_ref)

    kr = kf_ref[:tf, :]
    ki = kf_ref[tf:, :]
    first = (lax.broadcasted_iota(jnp.int32, (tf, 1), 0) == 0) & (kt == 0)
    for n in range(nb):
        u = _dot(w_ref[...], v_ref[n])
        ur, ui = u[:tf], u[tf:]
        gr = ur * kr - jnp.where(first, 0.0, ui * ki)
        gi = jnp.where(first, ui * ki, ur * ki + ui * kr)
        acc_ref[n] += _dot(wi_ref[...], jnp.concatenate([gr, gi], axis=0).astype(BF16))

    @pl.when(kt == pl.num_programs(1) - 1)
    def _():
        for n in range(nb):
            vf = v_ref[n].astype(F32)
            y = (acc_ref[n] + vf * bias_ref[...]).astype(BF16)
            o_ref[n] = (g_ref[n] * y).astype(BF16)


CONV_TF = 256
CONV_NB = 4


def _longconv(v_arr, v_col, g_arr, g_col, w, wi, kf, order, bias):
    nbatch, seq, _ = v_arr.shape
    nb = math.gcd(nbatch, CONV_NB)
    tf = min(CONV_TF, seq)
    c = HY_WIDTH
    return pl.pallas_call(
        _longconv_kernel,
        grid=(nbatch // nb, seq // tf),
        in_specs=[
            pl.BlockSpec((nb, seq, c), lambda i, k: (i, 0, v_col)),
            pl.BlockSpec((nb, seq, c), lambda i, k: (i, 0, g_col)),
            pl.BlockSpec((None, 2 * tf, seq), lambda i, k: (k, 0, 0)),
            pl.BlockSpec((None, seq, 2 * tf), lambda i, k: (k, 0, 0)),
            pl.BlockSpec((None, 2 * tf, c), lambda i, k: (k, 0, order)),
            pl.BlockSpec((1, c), lambda i, k: (0, 0)),
        ],
        out_specs=pl.BlockSpec((nb, seq, c), lambda i, k: (i, 0, 0)),
        out_shape=jax.ShapeDtypeStruct((nbatch, seq, c), BF16),
        scratch_shapes=[pltpu.VMEM((nb, seq, c), F32)],
        compiler_params=_params(("arbitrary", "arbitrary")),
        name="hy_longconv",
    )(v_arr, g_arr, w, wi, kf, bias)


def _lane_mask(lo, width, n=LANES):
    lane = lax.broadcasted_iota(jnp.int32, (1, n), 1)
    return (lane >= lo) & (lane < lo + width)


ALIBI_SPLIT = 3
POS_RADIX = 256


def _da_coef(head):
    return 2.0 ** (-8.0 * (head + 1) / DA_HEADS) * math.log2(math.e)


def _alibi_tables(seq):
    pos = jnp.arange(seq, dtype=jnp.int32)
    hi = ((pos // POS_RADIX) * POS_RADIX).astype(F32)[:, None]
    lo = (pos % POS_RADIX).astype(F32)[:, None]
    ones = jnp.ones((seq, 1), F32)
    fq, fk = [], []
    for head in range(DA_HEADS):
        rest = jnp.float32(_da_coef(head))
        pieces = []
        for _ in range(ALIBI_SPLIT):
            piece = rest.astype(BF16).astype(F32)
            pieces.append(piece)
            rest = rest - piece
        cq = jnp.concatenate([ones * c for c in pieces], axis=1)
        fq.append(jnp.concatenate([hi] * ALIBI_SPLIT + [lo] * ALIBI_SPLIT + [cq, cq], axis=1))
        fk.append(jnp.concatenate([-cq, -cq] + [hi] * ALIBI_SPLIT + [lo] * ALIBI_SPLIT, axis=1))
    pad = lambda t: jnp.pad(t, ((0, 0), (0, 0), (0, LANES - 4 * ALIBI_SPLIT))).astype(BF16)
    fk = jnp.stack(fk)
    tq = min(DA_TQ, seq)
    key_tile = (pos // tq)[None, :, None]
    signed = [pad(fk * jnp.where(key_tile < qi, 1.0, jnp.where(key_tile > qi, -1.0, 0.0))) for qi in range(seq // tq)]
    return pad(jnp.stack(fq)), signed


def _da_kernel(q_ref, k_ref, v_ref, fq_ref, fk_ref, lam_ref, g_ref, o_ref, *, qi, lam_init):
    tq = q_ref.shape[1]
    seq = k_ref.shape[1]
    lv = lam_ref[...]
    lam = (jnp.exp(jnp.sum(lv[0:1] * lv[1:2], axis=1, keepdims=True))
           - jnp.exp(jnp.sum(lv[2:3] * lv[3:4], axis=1, keepdims=True)) + lam_init)
    local = jnp.abs(lax.broadcasted_iota(jnp.int32, (tq, tq), 0)
                    - lax.broadcasted_iota(jnp.int32, (tq, tq), 1)).astype(F32)
    slabs = []
    for s in range(DA_WIDTH // LANES):
        cols = slice(s * LANES, (s + 1) * LANES)
        q_slab = q_ref[0, :, cols]
        k_slab = k_ref[0, :, cols]
        v_slab = v_ref[0, :, cols]
        o_slab = jnp.zeros((tq, LANES), F32)
        for hh in range(LANES // HEAD_DIM):
            head = s * (LANES // HEAD_DIM) + hh
            k_aug = jnp.concatenate([k_slab, fk_ref[head]], axis=1)
            mine = _lane_mask(hh * HEAD_DIM, HEAD_DIM)
            v_ones = jnp.where(mine, v_slab, jnp.ones_like(v_slab))
            maps = []
            for c in range(2):
                keep = jnp.where(_lane_mask(hh * HEAD_DIM + c * DA_HALF, DA_HALF), 1.0, 0.0).astype(BF16)
                q_aug = jnp.concatenate([q_slab * keep, fq_ref[head]], axis=1)
                sc = _dot_nt(q_aug, k_aug)
                pieces = [sc[:, qi * tq:(qi + 1) * tq] - _da_coef(head) * local]
                if qi > 0:
                    pieces.insert(0, sc[:, :qi * tq])
                if (qi + 1) * tq < seq:
                    pieces.append(sc[:, (qi + 1) * tq:])
                sc = jnp.concatenate(pieces, axis=1)
                p = jnp.exp2(sc - jnp.max(sc, axis=1, keepdims=True))
                o2 = _dot(p.astype(BF16), v_ones)
                maps.append((o2, o2[:, (1 - hh) * HEAD_DIM:(1 - hh) * HEAD_DIM + 1]))
            diff = maps[0][0] * (1.0 / maps[0][1]) - maps[1][0] * (lam / maps[1][1])
            o_slab = jnp.where(mine, diff, o_slab)
        slabs.append(o_slab)
    o = jnp.concatenate(slabs, axis=1)
    y = o * lax.rsqrt(_head_ssq(o) * (1.0 / HEAD_DIM) + EPS) * g_ref[...]
    o_ref[0] = (y * (1.0 - lam_init)).astype(BF16)


DA_TQ = 512


def _diff_attn(z, fq, fk_signed, lam_vecs, g_tiled, lam_init):
    nb, seq, _ = z.shape
    tq = min(DA_TQ, seq)
    cb = COL_DA // DA_WIDTH
    tiles = []
    for qi in range(seq // tq):
        tiles.append(pl.pallas_call(
            functools.partial(_da_kernel, qi=qi, lam_init=lam_init),
            grid=(nb,),
            in_specs=[
                pl.BlockSpec((1, tq, DA_WIDTH), lambda b, qi=qi: (b, qi, cb)),
                pl.BlockSpec((1, seq, DA_WIDTH), lambda b: (b, 0, cb + 1)),
                pl.BlockSpec((1, seq, DA_WIDTH), lambda b: (b, 0, cb + 2)),
                pl.BlockSpec((DA_HEADS, tq, LANES), lambda b, qi=qi: (0, qi, 0)),
                pl.BlockSpec((DA_HEADS, seq, LANES), lambda b: (0, 0, 0)),
                pl.BlockSpec((4, DA_HALF), lambda b: (0, 0)),
                pl.BlockSpec((1, DA_WIDTH), lambda b: (0, 0)),
            ],
            out_specs=pl.BlockSpec((1, tq, DA_WIDTH), lambda b: (b, 0, 0)),
            out_shape=jax.ShapeDtypeStruct((nb, tq, DA_WIDTH), BF16),
            compiler_params=_params(("arbitrary",)),
            name="diff_attn",
        )(z, z, z, fq, fk_signed[qi], lam_vecs, g_tiled))
    return jnp.concatenate(tiles, axis=1)


def _rope(x, cos, sin):
    w = x.shape[1]
    lane = lax.broadcasted_iota(jnp.int32, (1, w), 1)
    half = ROPE_AXIS // 2
    partner = jnp.where(lane % ROPE_AXIS < half, pltpu.roll(x, w - half, 1), pltpu.roll(x, half, 1))
    return x * cos + partner * sin


def _dup_heads(x):
    first = _lane_mask(0, HEAD_DIM)
    a = jnp.where(first, x, 0.0)
    b = jnp.where(first, 0.0, x)
    return jnp.concatenate([a + pltpu.roll(a, HEAD_DIM, 1), b + pltpu.roll(b, HEAD_DIM, 1)], axis=1)


def _with_ones(v):
    first = _lane_mask(0, HEAD_DIM)
    swapped = pltpu.roll(v, HEAD_DIM, 1)
    return jnp.concatenate([jnp.where(first, v, 1.0), jnp.where(first, 1.0, swapped),
                            jnp.where(first, swapped, 1.0), jnp.where(first, 1.0, v)], axis=1)


def _gqa_kernel(zq_ref, zkv_ref, cosq_ref, sinq_ref, cosk_ref, sink_ref, gq_ref, gk_ref, o_ref, kd_ref, vd_ref):
    tq = zq_ref.shape[1]
    kvw = GQ_KV * HEAD_DIM
    per_slab = LANES // HEAD_DIM
    group = GQ_HEADS // GQ_KV

    @pl.when(pl.program_id(1) == 0)
    def _():
        k = zkv_ref[0, :, :kvw].astype(F32)
        kn = k * lax.rsqrt(_head_ssq(k) * (1.0 / HEAD_DIM) + EPS) * gk_ref[...]
        kd_ref[...] = _dup_heads(_rope(kn, cosk_ref[...], sink_ref[...])).astype(BF16)
        vd_ref[...] = _with_ones(zkv_ref[0, :, kvw:].astype(F32)).astype(BF16)

    q = zq_ref[0].astype(F32)
    qn = q * lax.rsqrt(_head_ssq(q) * (1.0 / HEAD_DIM) + EPS) * gq_ref[...]
    qr = (_rope(qn, cosq_ref[...], sinq_ref[...]) * (HEAD_DIM ** -0.5 * math.log2(math.e))).astype(BF16)
    slabs = []
    for s in range(GQ_WIDTH // LANES):
        q_slab = qr[:, s * LANES:(s + 1) * LANES]
        kv = (s * per_slab) // group
        k_dup = kd_ref[:, kv * LANES:(kv + 1) * LANES]
        o_slab = jnp.zeros((tq, LANES), F32)
        for hh in range(per_slab):
            keep = jnp.where(_lane_mask(hh * HEAD_DIM, HEAD_DIM), 1.0, 0.0).astype(BF16)
            sc = _dot_nt(q_slab * keep, k_dup)
            p = jnp.exp2(sc - jnp.max(sc, axis=1, keepdims=True))
            v_ones = vd_ref[:, (kv * per_slab + hh) * LANES:(kv * per_slab + hh + 1) * LANES]
            o2 = _dot(p.astype(BF16), v_ones)
            sums = o2[:, (1 - hh) * HEAD_DIM:(1 - hh) * HEAD_DIM + 1]
            o_slab = jnp.where(_lane_mask(hh * HEAD_DIM, HEAD_DIM), o2 * (1.0 / sums), o_slab)
        slabs.append(o_slab)
    o_ref[0] = jnp.concatenate(slabs, axis=1).astype(BF16)


def _gqa_attn(z, cos, sin, gq, gk):
    nb, seq, _ = z.shape
    tq = min(512, seq)
    kvw = GQ_KV * HEAD_DIM
    return pl.pallas_call(
        _gqa_kernel,
        grid=(nb, seq // tq),
        in_specs=[
            pl.BlockSpec((1, tq, GQ_WIDTH), lambda b, i: (b, i, COL_GQ // GQ_WIDTH)),
            pl.BlockSpec((1, seq, 2 * kvw), lambda b, i: (b, 0, (COL_GQ + GQ_WIDTH) // (2 * kvw))),
            pl.BlockSpec((tq, GQ_WIDTH), lambda b, i: (i, 0)),
            pl.BlockSpec((tq, GQ_WIDTH), lambda b, i: (i, 0)),
            pl.BlockSpec((seq, kvw), lambda b, i: (0, 0)),
            pl.BlockSpec((seq, kvw), lambda b, i: (0, 0)),
            pl.BlockSpec((1, GQ_WIDTH), lambda b, i: (0, 0)),
            pl.BlockSpec((1, kvw), lambda b, i: (0, 0)),
        ],
        out_specs=pl.BlockSpec((1, tq, GQ_WIDTH), lambda b, i: (b, i, 0)),
        out_shape=jax.ShapeDtypeStruct((nb, seq, GQ_WIDTH), BF16),
        scratch_shapes=[pltpu.VMEM((seq, 2 * kvw), BF16), pltpu.VMEM((seq, 4 * kvw), BF16)],
        compiler_params=_params(("arbitrary", "arbitrary")),
        name="gqa_attn",
    )(z, z, cos, sin, cos, sin, gq, gk)


def _merge_kernel(ya_ref, yb_ref, yc_ref, zg_ref, x_ref, mod_ref, wa_ref, wb_ref, wc_ref, wo_ref,
                  lg_ref, lb_ref, *rest):
    d = D_MODEL
    m = (jnp.tanh(zg_ref[0, :, 0:d].astype(F32)) + 1.0) * _dot(ya_ref[0], wa_ref[...])
    m = m + (jnp.tanh(zg_ref[0, :, d:2 * d].astype(F32)) + 1.0) * _dot(yb_ref[0], wb_ref[...])
    m = m + (jnp.tanh(zg_ref[0, :, 2 * d:3 * d].astype(F32)) + 1.0) * _dot(yc_ref[0], wc_ref[...])
    y = _dot(m.astype(BF16), wo_ref[...])
    gate1 = 1.0 + mod_ref[0, 2:3, :]
    xn = _ln(DEEPNORM_ALPHA * x_ref[0] + gate1 * y) * lg_ref[...] + lb_ref[...]
    h = _ln(xn) * (1.0 + mod_ref[0, 4:5, :]) + mod_ref[0, 3:4, :]
    if len(rest) == 2:
        xo_ref, h_ref = rest
    else:
        wr_ref, br_ref, xo_ref, h_ref, comb_ref = rest
        lane = lax.broadcasted_iota(jnp.int32, (1, LANES), 1).astype(F32)
        neg = -jnp.inf
        logits = jnp.where(lane < N_EXPERTS, _dot3(h, wr_ref[...]) + br_ref[...], neg)
        v1 = jnp.max(logits, axis=1, keepdims=True)
        i1 = jnp.min(jnp.where(logits == v1, lane, float(LANES)), axis=1, keepdims=True)
        others = jnp.where(lane == i1, neg, logits)
        v2 = jnp.max(others, axis=1, keepdims=True)
        i2 = jnp.min(jnp.where(others == v2, lane, float(LANES)), axis=1, keepdims=True)
        e = jnp.exp(v2 - v1)
        w1 = 1.0 / (1.0 + e)
        comb_ref[0] = jnp.where(lane == i1, w1, 0.0) + jnp.where(lane == i2, e * w1, 0.0)
    xo_ref[0] = xn
    h_ref[0] = h.astype(BF16)


def _merge(ya, yb, yc, z, x, mod, wa, wb, wc, wo, lg, lb, router=None):
    nb, seq, d = x.shape
    tm = min(512, seq)
    full = lambda shape: pl.BlockSpec(shape, lambda b, i: (0,) * len(shape))
    row = lambda w, col=0: pl.BlockSpec((1, tm, w), lambda b, i: (b, i, col))
    in_specs = [
        row(HY_WIDTH), row(DA_WIDTH), row(GQ_WIDTH), row(GATE_COLS, COL_GATE // GATE_COLS), row(d),
        pl.BlockSpec((1, 6, d), lambda b, i: (b, 0, 0)),
        full((HY_WIDTH, d)), full((DA_WIDTH, d)), full((GQ_WIDTH, d)), full((d, d)),
        full((1, d)), full((1, d)),
    ]
    out_specs = [row(d), row(d)]
    out_shape = [jax.ShapeDtypeStruct((nb, seq, d), F32), jax.ShapeDtypeStruct((nb, seq, d), BF16)]
    args = [ya, yb, yc, z, x, mod, wa, wb, wc, wo, lg, lb]
    if router is not None:
        in_specs += [full((d, LANES)), full((1, LANES))]
        out_specs.append(row(LANES))
        out_shape.append(jax.ShapeDtypeStruct((nb, seq, LANES), F32))
        args += list(router)
    return pl.pallas_call(
        _merge_kernel,
        grid=(nb, seq // tm),
        in_specs=in_specs,
        out_specs=out_specs,
        out_shape=out_shape,
        compiler_params=_params(("arbitrary", "arbitrary")),
        name="merge_outproj",
    )(*args)


FFN_TF = 256


def _ffn_kernel(h_ref, wup_ref, wd_ref, x_ref, mod_ref, lg_ref, lb_ref, o_ref):
    h = h_ref[0]
    ff = wd_ref.shape[0]
    acc = None
    for c in range(ff // FFN_TF):
        cols = slice(c * FFN_TF, (c + 1) * FFN_TF)
        g = _dot(h, wup_ref[:, cols])
        u = _dot(h, wup_ref[:, ff + c * FFN_TF:ff + (c + 1) * FFN_TF])
        part = _dot((g * _sigmoid(g) * u).astype(BF16), wd_ref[cols, :])
        acc = part if acc is None else acc + part
    gate2 = 1.0 + mod_ref[0, 5:6, :]
    o_ref[0] = _ln(DEEPNORM_ALPHA * x_ref[0] + gate2 * acc) * lg_ref[...] + lb_ref[...]


def _ffn(h, w_up, w_down, x, mod, lg, lb):
    nb, seq, d = x.shape
    ff = w_down.shape[0]
    assert ff % FFN_TF == 0
    tm = min(1024, seq)
    row = lambda w: pl.BlockSpec((1, tm, w), lambda b, i: (b, i, 0))
    vec = pl.BlockSpec((1, d), lambda b, i: (0, 0))
    once = pl.Buffered(1)
    return pl.pallas_call(
        _ffn_kernel,
        grid=(nb, seq // tm),
        in_specs=[
            row(d),
            pl.BlockSpec((d, 2 * ff), lambda b, i: (0, 0), pipeline_mode=once),
            pl.BlockSpec((ff, d), lambda b, i: (0, 0), pipeline_mode=once),
            row(d),
            pl.BlockSpec((1, 6, d), lambda b, i: (b, 0, 0)),
            vec, vec,
        ],
        out_specs=row(d),
        out_shape=jax.ShapeDtypeStruct((nb, seq, d), F32),
        compiler_params=_params(("arbitrary",) * 2),
        name="swiglu_dense",
    )(h, w_up, w_down, x, mod, lg, lb)


MOE_TOKENS = 1024
MOE_CHUNK = 288
MOE_SLOT = 384


def _moe_kernel(h_ref, comb_ref, wg_ref, wu_ref, wd_ref, x_ref, mod_ref, lg_ref, lb_ref, o_ref,
                posc_ref, posr_ref, cnt_ref, ys_ref):
    e = pl.program_id(2)
    tm = h_ref.shape[1]
    n_exp = ys_ref.shape[0] // MOE_SLOT
    lane = lax.broadcasted_iota(jnp.int32, (1, LANES), 1)

    @pl.when(e == 0)
    def _():
        comb = comb_ref[0]
        sel = jnp.where(comb > 0.0, 1.0, 0.0)
        r = lax.broadcasted_iota(jnp.int32, (tm, tm), 0)
        c = lax.broadcasted_iota(jnp.int32, (tm, tm), 1)
        before = jnp.where(c < r, 1.0, 0.0).astype(BF16)
        rank = _dot(before, sel.astype(BF16))
        posc = jnp.where(comb > 0.0, rank, -1.0)
        posc_ref[...] = posc
        posr_ref[...] = posc.T
        cnt_ref[...] = jnp.sum(sel, axis=0, keepdims=True)
        o_ref[0] = jnp.zeros(o_ref.shape[1:], F32)
        ys_ref[...] = jnp.zeros_like(ys_ref)

    mine = lane == e
    pos_row = posr_ref[pl.ds(e, 1), :]
    count = jnp.sum(jnp.where(mine, cnt_ref[...], 0.0))

    def expert_rows(c):
        slot_r = (c * MOE_CHUNK + lax.broadcasted_iota(jnp.int32, (MOE_CHUNK, 1), 0)).astype(F32)
        gather = jnp.where(pos_row == slot_r, 1.0, 0.0).astype(BF16)
        xs = _dot(gather, h_ref[0]).astype(BF16)
        g = _dot(xs, wg_ref[...])
        u = _dot(xs, wu_ref[...])
        a = (g * _sigmoid(g) * u).astype(BF16)
        return _dot(a, wd_ref[...]).astype(BF16)

    ys_ref[pl.ds(pl.multiple_of(e * MOE_SLOT, MOE_SLOT), MOE_CHUNK), :] = expert_rows(0)

    for c in range(1, -(-tm // MOE_CHUNK)):
        @pl.when(count > c * MOE_CHUNK)
        def _():
            pos_col = jnp.sum(jnp.where(mine, posc_ref[...], 0.0), axis=1, keepdims=True)
            w_col = jnp.sum(jnp.where(mine, comb_ref[0], 0.0), axis=1, keepdims=True)
            slot_c = (c * MOE_CHUNK + lax.broadcasted_iota(jnp.int32, (1, MOE_CHUNK), 1)).astype(F32)
            scatter = jnp.where(pos_col == slot_c, 1.0, 0.0).astype(BF16)
            o_ref[0] += w_col * _dot(scatter, expert_rows(c))

    @pl.when(e == pl.num_programs(2) - 1)
    def _():
        slot_c = lax.broadcasted_iota(jnp.int32, (1, MOE_SLOT), 1).astype(F32)
        slot_c = jnp.where(slot_c < MOE_CHUNK, slot_c, -2.0)
        f = o_ref[0]
        for e0 in range(0, n_exp, 2):
            weights = jnp.concatenate(
                [jnp.where(posc_ref[:, ee:ee + 1] == slot_c, comb_ref[0, :, ee:ee + 1], 0.0).astype(BF16)
                 for ee in (e0, e0 + 1)], axis=1)
            f = f + _dot(weights, ys_ref[e0 * MOE_SLOT:(e0 + 2) * MOE_SLOT, :])
        gate2 = 1.0 + mod_ref[0, 5:6, :]
        o_ref[0] = _ln(DEEPNORM_ALPHA * x_ref[0] + gate2 * f) * lg_ref[...] + lb_ref[...]


def _moe(h, comb, w_up, w_down, x, mod, lg, lb):
    nb, seq, d = x.shape
    ne, ff, _ = w_down.shape
    tm = min(MOE_TOKENS, seq)
    row = lambda w: pl.BlockSpec((1, tm, w), lambda b, i, e: (b, i, 0))
    vec = pl.BlockSpec((1, d), lambda b, i, e: (0, 0))
    return pl.pallas_call(
        _moe_kernel,
        grid=(nb, seq // tm, ne),
        in_specs=[
            row(d), row(LANES),
            pl.BlockSpec((None, d, ff), lambda b, i, e: (e, 0, 0)),
            pl.BlockSpec((None, d, ff), lambda b, i, e: (e, 0, 1)),
            pl.BlockSpec((None, ff, d), lambda b, i, e: (e, 0, 0)),
            row(d),
            pl.BlockSpec((1, 6, d), lambda b, i, e: (b, 0, 0)),
            vec, vec,
        ],
        out_specs=row(d),
        out_shape=jax.ShapeDtypeStruct((nb, seq, d), F32),
        scratch_shapes=[pltpu.VMEM((tm, LANES), F32), pltpu.VMEM((LANES, tm), F32), pltpu.VMEM((1, LANES), F32),
                        pltpu.VMEM((ne * MOE_SLOT, d), BF16)],
        compiler_params=_params(("arbitrary",) * 3),
        name="swiglu_routed",
    )(h, comb, w_up, w_up, w_down, x, mod, lg, lb)


def _dft_tables(seq):
    n = 2 * seq
    tf = min(CONV_TF, seq)
    nk = seq // tf
    s = jnp.arange(n, dtype=jnp.int32)[None, :]

    def phase(mult):
        ang = ((mult * s) % n).astype(F32) * (2.0 * math.pi / n)
        return jnp.cos(ang), jnp.sin(ang)

    ca, sa = phase(jnp.arange(nk, dtype=jnp.int32)[:, None] * tf)
    cb, sb = phase(jnp.arange(tf, dtype=jnp.int32)[:, None])
    cos = ca[:, None, :] * cb[None] - sa[:, None, :] * sb[None]
    msin = -(sa[:, None, :] * cb[None] + ca[:, None, :] * sb[None])
    k = (jnp.arange(nk, dtype=jnp.int32)[:, None] * tf + jnp.arange(tf, dtype=jnp.int32)[None, :])[:, :, None]
    msin = jnp.where(k == 0, jnp.where(s % 2 == 0, 1.0, -1.0)[None], msin)
    fwd = jnp.concatenate([cos, msin], axis=1)
    scale = jnp.where(k == 0, 1.0 / n, 2.0 / n).astype(F32)
    inv = jnp.concatenate([cos[:, :, :seq] * scale, msin[:, :, :seq] * scale], axis=1).transpose(0, 2, 1)
    return fwd.astype(BF16), inv.astype(BF16)


def _rope_tables(seq):
    t = jnp.arange(seq, dtype=jnp.int32)
    inv = ROPE_THETA ** (-jnp.arange(0, ROPE_AXIS, 2, dtype=F32) / ROPE_AXIS)
    ang_r = (t // GRID_W).astype(F32)[:, None] * inv[None, :]
    ang_c = (t % GRID_W).astype(F32)[:, None] * inv[None, :]
    cos = jnp.concatenate([jnp.cos(ang_r)] * 2 + [jnp.cos(ang_c)] * 2, axis=1)
    sin = jnp.concatenate([-jnp.sin(ang_r), jnp.sin(ang_r), -jnp.sin(ang_c), jnp.sin(ang_c)], axis=1)
    return jnp.tile(cos, (1, GQ_HEADS)), jnp.tile(sin, (1, GQ_HEADS))


def _pad_to(a, rows, cols):
    return jnp.pad(a, ((0, rows - a.shape[0]), (0, cols - a.shape[1])))


def _prepare(seq, p):
    fwd, inv = _dft_tables(seq)
    cos, sin = _rope_tables(seq)
    bands = jnp.linspace(1e-4, N_BANDS - 1, N_BANDS, dtype=F32)
    bands_row = jnp.zeros((1, LANES), F32).at[0, 1:1 + N_BANDS].set(bands).at[0, 1 + N_BANDS:1 + 2 * N_BANDS].set(bands)
    deltas = jnp.abs(jnp.linspace(MIN_DECAY, MAX_DECAY, HY_WIDTH, dtype=F32))
    dec = jnp.tile(deltas, HY_ORDER)[None, :]
    da_scale = DA_HALF ** -0.5 * math.log2(math.e)
    fq, fk = _alibi_tables(seq)
    layers = []
    for l in range(DEPTH):
        w_in, b_in = p['w_in'][l], p['b_in'][l]
        a0, b0, c0, g0 = 0, HY_COLS, HY_COLS + DA_COLS, HY_COLS + DA_COLS + GQ_COLS
        w_cols = [w_in[:, g0:] * 0.5, w_in[:, a0:b0], w_in[:, b0:b0 + DA_WIDTH] * da_scale, w_in[:, b0 + DA_WIDTH:c0], w_in[:, c0:g0]]
        b_cols = [b_in[g0:] * 0.5, b_in[a0:b0], b_in[b0:b0 + DA_WIDTH] * da_scale, b_in[b0 + DA_WIDTH:c0], b_in[c0:g0]]
        ts = _filters_time(
            seq, bands_row,
            _pad_to(p['hy_f_w1'][l], LANES, LANES), _pad_to(p['hy_f_b1'][l][None, :], 1, LANES),
            _pad_to(p['hy_f_w2'][l], LANES, LANES), _pad_to(p['hy_f_b2'][l][None, :], 1, LANES),
            _pad_to(p['hy_f_w3'][l], LANES, 2 * HY_ORDER * HY_WIDTH), p['hy_f_b3'][l][None, :], dec)
        kf = _filter_dft(fwd.reshape(2 * seq, 2 * seq), ts).reshape(fwd.shape[0], -1, HY_ORDER * HY_WIDTH)
        lay = dict(
            w_in=jnp.concatenate(w_cols, axis=1).astype(BF16),
            b_in=jnp.concatenate(b_cols)[None, :],
            conv_w=p['hy_conv_w'][l], conv_b=p['hy_conv_b'][l][None, :],
            kf=kf, hy_bias=p['hy_bias'][l],
            lam=jnp.stack([p['da_lam_q1'][l], p['da_lam_k1'][l], p['da_lam_q2'][l], p['da_lam_k2'][l]]),
            lam_init=0.8 - 0.6 * math.exp(-0.3 * l),
            subln=jnp.tile(p['da_subln_g'][l], DA_HEADS)[None, :],
            gq=jnp.tile(p['gq_qnorm_g'][l], GQ_HEADS)[None, :],
            gk=jnp.tile(p['gq_knorm_g'][l], GQ_KV)[None, :],
            wa=(p['w_br_a'][l] * 0.5).astype(BF16), wb=(p['w_br_b'][l] * 0.5).astype(BF16),
            wc=(p['w_br_c'][l] * 0.5).astype(BF16),
            wo=p['w_out'][l].astype(BF16),
            ln1_g=p['ln1_g'][l][None, :], ln1_b=p['ln1_b'][l][None, :],
            ln2_g=p['ln2_g'][l][None, :], ln2_b=p['ln2_b'][l][None, :],
        )
        if l % 2 == 0:
            lay.update(w_up=p['ffn_w_up'][l // 2].astype(BF16), w_down=p['ffn_w_down'][l // 2].astype(BF16))
        else:
            lay.update(w_up=p['moe_w_up'][l // 2].astype(BF16), w_down=p['moe_w_down'][l // 2].astype(BF16),
                       router=(_pad_to(p['moe_w_router'][l // 2], D_MODEL, LANES),
                               _pad_to(p['moe_b_router'][l // 2][None, :], 1, LANES)))
        layers.append(lay)
    return dict(fwd=fwd, inv=inv, cos=cos, sin=sin, fq=fq, fk=fk, layers=layers)


def _trunk(x, c, p, prep):
    mods = _ada(c, p['w_ada'], p['b_ada'])
    nb = x.shape[0]
    for l, lay in enumerate(prep['layers']):
        mod = mods[l].reshape(nb, 6, D_MODEL)
        z = _inproj(x, mod, lay['w_in'], lay['b_in'], lay['conv_w'], lay['conv_b'])
        hy = COL_HY // HY_WIDTH
        u = _longconv(z, hy + 2, z, hy, prep['fwd'], prep['inv'], lay['kf'], 0, lay['hy_bias'][0:1])
        ya = _longconv(u, 0, z, hy + 1, prep['fwd'], prep['inv'], lay['kf'], 1, lay['hy_bias'][1:2])
        yb = _diff_attn(z, prep['fq'], prep['fk'], lay['lam'], lay['subln'], lay['lam_init'])
        yc = _gqa_attn(z, prep['cos'], prep['sin'], lay['gq'], lay['gk'])
        merged = _merge(ya, yb, yc, z, x, mod, lay['wa'], lay['wb'], lay['wc'], lay['wo'],
                        lay['ln1_g'], lay['ln1_b'], lay.get('router'))
        if l % 2 == 0:
            x, h = merged
            x = _ffn(h, lay['w_up'], lay['w_down'], x, mod, lay['ln2_g'], lay['ln2_b'])
        else:
            x, h, comb = merged
            x = _moe(h, comb, lay['w_up'], lay['w_down'], x, mod, lay['ln2_g'], lay['ln2_b'])
    return x


def kernel(x_prompt, x_sample, c_prompt, c_sample, w_ada, b_ada, w_in, b_in, hy_conv_w, hy_conv_b, hy_f_w1, hy_f_b1, hy_f_w2, hy_f_b2, hy_f_w3, hy_f_b3, hy_bias, da_lam_q1, da_lam_k1, da_lam_q2, da_lam_k2, da_subln_g, gq_qnorm_g, gq_knorm_g, w_br_a, w_br_b, w_br_c, w_out, ln1_g, ln1_b, ffn_w_up, ffn_w_down, moe_w_router, moe_b_router, moe_w_up, moe_w_down, ln2_g, ln2_b):
    p = dict(w_ada=w_ada, b_ada=b_ada, w_in=w_in, b_in=b_in,
             hy_conv_w=hy_conv_w, hy_conv_b=hy_conv_b, hy_f_w1=hy_f_w1, hy_f_b1=hy_f_b1,
             hy_f_w2=hy_f_w2, hy_f_b2=hy_f_b2, hy_f_w3=hy_f_w3, hy_f_b3=hy_f_b3, hy_bias=hy_bias,
             da_lam_q1=da_lam_q1, da_lam_k1=da_lam_k1, da_lam_q2=da_lam_q2, da_lam_k2=da_lam_k2,
             da_subln_g=da_subln_g, gq_qnorm_g=gq_qnorm_g, gq_knorm_g=gq_knorm_g,
             w_br_a=w_br_a, w_br_b=w_br_b, w_br_c=w_br_c, w_out=w_out, ln1_g=ln1_g, ln1_b=ln1_b,
             ffn_w_up=ffn_w_up, ffn_w_down=ffn_w_down, moe_w_router=moe_w_router,
             moe_b_router=moe_b_router, moe_w_up=moe_w_up, moe_w_down=moe_w_down,
             ln2_g=ln2_g, ln2_b=ln2_b)
    assert x_prompt.shape[1] == x_sample.shape[1]
    prep = _prepare(x_prompt.shape[1], p)
    return (_trunk(x_prompt, c_prompt, p, prep), _trunk(x_sample, c_sample, p, prep))
```

```python
import functools
import math

import jax
import jax.numpy as jnp
from jax import lax
from jax.experimental import pallas as pl
from jax.experimental.pallas import tpu as pltpu

F32 = jnp.float32
BF16 = jnp.bfloat16

D_MODEL = 1024
DEPTH = 2
GRID_W = 64
HEAD_DIM = 64
HY_WIDTH = D_MODEL // 4
HY_ORDER = 2
SHORT_CONV = 3
N_BANDS = 16
FILTER_HID = 64
DECAY_TARGET = 1e-2
MIN_DECAY = math.log(DECAY_TARGET) / 1.5
MAX_DECAY = math.log(DECAY_TARGET) / 0.3
DECAY_SHIFT = 0.05
DA_HEADS = 4
DA_HALF = HEAD_DIM // 2
DA_WIDTH = DA_HEADS * HEAD_DIM
GQ_HEADS = 8
GQ_KV = 2
GQ_WIDTH = GQ_HEADS * HEAD_DIM
ROPE_AXIS = HEAD_DIM // 2
ROPE_THETA = 10000.0
HY_COLS = (HY_ORDER + 1) * HY_WIDTH
DA_COLS = 3 * DA_WIDTH
GQ_COLS = GQ_WIDTH + 2 * GQ_KV * HEAD_DIM
GATE_COLS = 3 * D_MODEL
IN_COLS = HY_COLS + DA_COLS + GQ_COLS + GATE_COLS
D_FF = 256 * ((8 * D_MODEL // 3 + 255) // 256)
N_EXPERTS = 8
MOE_FF = D_FF // 2
DEEPNORM_ALPHA = (2.0 * DEPTH) ** 0.25
EPS = 1e-5

LANES = 128
VMEM_LIMIT = 56 * 1024 * 1024

COL_GATE = 0
COL_HY = GATE_COLS
COL_DA = COL_HY + HY_COLS
COL_GQ = COL_DA + DA_COLS


def _params(sem):
    return pltpu.CompilerParams(dimension_semantics=sem, vmem_limit_bytes=VMEM_LIMIT)


def _dot(a, b):
    return jnp.dot(a, b, preferred_element_type=F32)


def _dot_nt(a, b):
    return lax.dot_general(a, b, (((1,), (1,)), ((), ())), preferred_element_type=F32)


def _split(x):
    hi = x.astype(BF16)
    lo = (x - hi.astype(F32)).astype(BF16)
    return hi, lo


def _dot3(a, b):
    ah, al = _split(a)
    bh, bl = _split(b)
    return _dot(ah, bh) + (_dot(ah, bl) + _dot(al, bh))


def _sigmoid(x):
    return 0.5 * jnp.tanh(0.5 * x) + 0.5


def _ln(x):
    mu = jnp.mean(x, axis=-1, keepdims=True)
    xc = x - mu
    var = jnp.mean(xc * xc, axis=-1, keepdims=True)
    return xc * lax.rsqrt(var + EPS)


def _head_ssq(x):
    w = x.shape[1]
    r = lax.broadcasted_iota(jnp.int32, (w, w), 0) // HEAD_DIM
    c = lax.broadcasted_iota(jnp.int32, (w, w), 1) // HEAD_DIM
    ones = jnp.where(r == c, 1.0, 0.0).astype(BF16)
    hi, lo = _split(x * x)
    return _dot(hi, ones) + _dot(lo, ones)


def _ada_kernel(c_ref, w_ref, b_ref, o_ref):
    c = c_ref[...]
    o_ref[...] = _dot3(c * _sigmoid(c), w_ref[...]) + b_ref[...]


def _ada(c, w_ada, b_ada):
    nb = c.shape[0]
    tn = 1536
    return pl.pallas_call(
        _ada_kernel,
        grid=(DEPTH, 6 * D_MODEL // tn),
        in_specs=[
            pl.BlockSpec((nb, D_MODEL), lambda l, j: (0, 0)),
            pl.BlockSpec((None, D_MODEL, tn), lambda l, j: (l, 0, j)),
            pl.BlockSpec((None, 1, tn), lambda l, j: (l, 0, j)),
        ],
        out_specs=pl.BlockSpec((None, nb, tn), lambda l, j: (l, 0, j)),
        out_shape=jax.ShapeDtypeStruct((DEPTH, nb, 6 * D_MODEL), F32),
        compiler_params=_params(("arbitrary", "arbitrary")),
        name="ada_mod",
    )(c, w_ada, b_ada.reshape(DEPTH, 1, 6 * D_MODEL))


LN_ROWS = 256
INPROJ_TN = 768
INPROJ_STEPS = IN_COLS // INPROJ_TN
HY_TILE = COL_HY // INPROJ_TN


def _inproj_kernel(x_ref, mod_ref, w_ref, b_ref, cw_ref, cb_ref, o_ref, ha_ref, hb_ref):
    b = pl.program_id(0)
    j = pl.program_id(1)
    seq = x_ref.shape[1]
    chunks = seq // LN_ROWS
    assert chunks <= INPROJ_STEPS + 1

    @pl.when((b == 0) & (j == 0))
    def _():
        hb_ref[...] = jnp.zeros_like(hb_ref)

    def step(h_ln, h_mm, ln_chunks, conv, col):
        shift = mod_ref[0, 0:1, :]
        scale = 1.0 + mod_ref[0, 1:2, :]
        for c in ln_chunks:
            r = pl.ds(pl.multiple_of(c * LN_ROWS, LN_ROWS), LN_ROWS)
            h_ln[r, :] = (_ln(x_ref[0, r, :]) * scale + shift).astype(BF16)
        cols = pl.ds(pl.multiple_of(col * INPROJ_TN, INPROJ_TN), INPROJ_TN)
        z = _dot(h_mm[...], w_ref[:, cols]) + b_ref[:, cols]
        if conv:
            row = lax.broadcasted_iota(jnp.int32, z.shape, 0)
            prev = jnp.where(row == 0, 0.0, pltpu.roll(z, 1, 0))
            nxt = jnp.where(row == seq - 1, 0.0, pltpu.roll(z, seq - 1, 0))
            z = cb_ref[...] + prev * cw_ref[0:1, :] + z * cw_ref[1:2, :] + nxt * cw_ref[2:3, :]
        o_ref[0] = z.astype(BF16)

    first = [0] + ([INPROJ_STEPS] if chunks > INPROJ_STEPS else [])
    later = [jnp.minimum(j, chunks - 1)]
    for parity, (h_ln, h_mm) in enumerate(((ha_ref, hb_ref), (hb_ref, ha_ref))):
        even = b % 2 == parity
        pl.when(even & (j == 0))(functools.partial(step, h_ln, h_mm, first, False, 0))
        pl.when(even & (j == HY_TILE))(functools.partial(step, h_ln, h_mm, later, True, HY_TILE))
        pl.when(even & (j != 0) & (j != HY_TILE))(functools.partial(step, h_ln, h_mm, later, False, j))


def _inproj(x, mod, w, b, conv_w, conv_b):
    nb, seq, _ = x.shape
    tn = INPROJ_TN
    assert HY_COLS == tn and COL_HY % tn == 0 and HY_TILE != 0 and seq % LN_ROWS == 0
    cur = lambda i, j: (jnp.minimum(i, nb - 1), 0, 0)
    return pl.pallas_call(
        _inproj_kernel,
        grid=(nb + 1, INPROJ_STEPS),
        in_specs=[
            pl.BlockSpec((1, seq, D_MODEL), cur),
            pl.BlockSpec((1, 6, D_MODEL), cur),
            pl.BlockSpec((D_MODEL, IN_COLS), lambda i, j: (0, 0), pipeline_mode=pl.Buffered(1)),
            pl.BlockSpec((1, IN_COLS), lambda i, j: (0, 0)),
            pl.BlockSpec((SHORT_CONV, tn), lambda i, j: (0, 0)),
            pl.BlockSpec((1, tn), lambda i, j: (0, 0)),
        ],
        out_specs=pl.BlockSpec((1, seq, tn), lambda i, j: (jnp.maximum(i - 1, 0), 0, jnp.where(i == 0, 0, j))),
        out_shape=jax.ShapeDtypeStruct((nb, seq, IN_COLS), BF16),
        scratch_shapes=[pltpu.VMEM((seq, D_MODEL), BF16), pltpu.VMEM((seq, D_MODEL), BF16)],
        compiler_params=_params(("arbitrary", "arbitrary")),
        name="ln_inproj",
    )(x, mod, w, b, conv_w, conv_b)


FILT_ROWS = 512


def _filter_kernel(bands_ref, w1_ref, b1_ref, w2_ref, b2_ref, w3_ref, b3_ref, dec_ref, o_ref, ts_ref, *, seq):
    n = 2 * seq
    rows = min(FILT_ROWS, n)
    hw = HY_ORDER * HY_WIDTH
    lane = lax.broadcasted_iota(jnp.int32, (1, LANES), 1)

    def fill(i, asum):
        j0 = pl.multiple_of(i * rows, rows)
        j = j0 + lax.broadcasted_iota(jnp.int32, (rows, 1), 0)
        t = jnp.where(j < seq, j, n - j).astype(F32)
        t_norm = t / max(seq - 1, 1)
        ang = (2.0 * math.pi / seq) * t * bands_ref[...]
        feats = jnp.where(lane == 0, t_norm,
                          jnp.where(lane <= N_BANDS, jnp.cos(ang),
                                    jnp.where(lane <= 2 * N_BANDS, -jnp.sin(ang), 0.0)))
        h = jnp.sin(_dot3(feats, w1_ref[...]) + b1_ref[...])
        h = jnp.sin(_dot3(h, w2_ref[...]) + b2_ref[...])
        h = _dot3(h, w3_ref[...]) + b3_ref[...]
        window = jnp.exp(-t_norm * dec_ref[...]) + DECAY_SHIFT
        sel = jnp.where(j < seq, h[:, :hw], h[:, hw:]) * window
        sel = jnp.where(j == seq, 0.0, sel)
        ts_ref[pl.ds(j0, rows), :] = sel
        return asum + jnp.sum(jnp.abs(sel), axis=0, keepdims=True)

    asum = lax.fori_loop(0, n // rows, fill, jnp.zeros((1, hw), F32))
    inv = 1.0 / (asum + EPS)

    def norm(i, carry):
        r = pl.ds(pl.multiple_of(i * rows, rows), rows)
        o_ref[r, :] = (ts_ref[r, :] * inv).astype(BF16)
        return carry

    lax.fori_loop(0, n // rows, norm, 0)


def _filters_time(seq, bands, w1, b1, w2, b2, w3, b3, dec):
    hw = HY_ORDER * HY_WIDTH
    return pl.pallas_call(
        functools.partial(_filter_kernel, seq=seq),
        out_shape=jax.ShapeDtypeStruct((2 * seq, hw), BF16),
        scratch_shapes=[pltpu.VMEM((2 * seq, hw), F32)],
        compiler_params=pltpu.CompilerParams(vmem_limit_bytes=VMEM_LIMIT),
        name="hy_filter_time",
    )(bands, w1, b1, w2, b2, w3, b3, dec)


def _matmul_kernel(a_ref, b_ref, o_ref):
    o_ref[...] = _dot(a_ref[...], b_ref[...])


def _filter_dft(wf, ts):
    n, hw = ts.shape
    tm = min(512, n)
    return pl.pallas_call(
        _matmul_kernel,
        grid=(n // tm,),
        in_specs=[pl.BlockSpec((tm, n), lambda i: (i, 0)), pl.BlockSpec((n, hw), lambda i: (0, 0))],
        out_specs=pl.BlockSpec((tm, hw), lambda i: (i, 0)),
        out_shape=jax.ShapeDtypeStruct((n, hw), F32),
        compiler_params=_params(("arbitrary",)),
        name="hy_filter_dft",
    )(wf, ts)


def _longconv_kernel(v_ref, g_ref, w_ref, wi_ref, kf_ref, bias_ref, o_ref, acc_ref):
    kt = pl.program_id(1)
    nb = v_ref.shape[0]
    tf = w_ref.shape[0] // 2

    @pl.when(kt == 0)
    def _():
        acc_ref[...] = jnp.zeros_like(acc_ref)

    kr = kf_ref[:tf, :]
    ki = kf_ref[tf:, :]
    first = (lax.broadcasted_iota(jnp.int32, (tf, 1), 0) == 0) & (kt == 0)
    for n in range(nb):
        u = _dot(w_ref[...], v_ref[n])
        ur, ui = u[:tf], u[tf:]
        gr = ur * kr - jnp.where(first, 0.0, ui * ki)
        gi = jnp.where(first, ui * ki, ur * ki + ui * kr)
        acc_ref[n] += _dot(wi_ref[...], jnp.concatenate([gr, gi], axis=0).astype(BF16))

    @pl.when(kt == pl.num_programs(1) - 1)
    def _():
        for n in range(nb):
            vf = v_ref[n].astype(F32)
            y = (acc_ref[n] + vf * bias_ref[...]).astype(BF16)
            o_ref[n] = (g_ref[n] * y).astype(BF16)


CONV_TF = 256
CONV_NB = 4


def _longconv(v_arr, v_col, g_arr, g_col, w, wi, kf, order, bias):
    nbatch, seq, _ = v_arr.shape
    nb = math.gcd(nbatch, CONV_NB)
    tf = min(CONV_TF, seq)
    c = HY_WIDTH
    return pl.pallas_call(
        _longconv_kernel,
        grid=(nbatch // nb, seq // tf),
        in_specs=[
            pl.BlockSpec((nb, seq, c), lambda i, k: (i, 0, v_col)),
            pl.BlockSpec((nb, seq, c), lambda i, k: (i, 0, g_col)),
            pl.BlockSpec((None, 2 * tf, seq), lambda i, k: (k, 0, 0)),
            pl.BlockSpec((None, seq, 2 * tf), lambda i, k: (k, 0, 0)),
            pl.BlockSpec((None, 2 * tf, c), lambda i, k: (k, 0, order)),
            pl.BlockSpec((1, c), lambda i, k: (0, 0)),
        ],
        out_specs=pl.BlockSpec((nb, seq, c), lambda i, k: (i, 0, 0)),
        out_shape=jax.ShapeDtypeStruct((nbatch, seq, c), BF16),
        scratch_shapes=[pltpu.VMEM((nb, seq, c), F32)],
        compiler_params=_params(("arbitrary", "arbitrary")),
        name="hy_longconv",
    )(v_arr, g_arr, w, wi, kf, bias)


def _lane_mask(lo, width, n=LANES):
    lane = lax.broadcasted_iota(jnp.int32, (1, n), 1)
    return (lane >= lo) & (lane < lo + width)


ALIBI_SPLIT = 3
POS_RADIX = 256


def _da_coef(head):
    return 2.0 ** (-8.0 * (head + 1) / DA_HEADS) * math.log2(math.e)


def _alibi_tables(seq):
    pos = jnp.arange(seq, dtype=jnp.int32)
    hi = ((pos // POS_RADIX) * POS_RADIX).astype(F32)[:, None]
    lo = (pos % POS_RADIX).astype(F32)[:, None]
    ones = jnp.ones((seq, 1), F32)
    fq, fk = [], []
    for head in range(DA_HEADS):
        rest = jnp.float32(_da_coef(head))
        pieces = []
        for _ in range(ALIBI_SPLIT):
            piece = rest.astype(BF16).astype(F32)
            pieces.append(piece)
            rest = rest - piece
        cq = jnp.concatenate([ones * c for c in pieces], axis=1)
        fq.append(jnp.concatenate([hi] * ALIBI_SPLIT + [lo] * ALIBI_SPLIT + [cq, cq], axis=1))
        fk.append(jnp.concatenate([-cq, -cq] + [hi] * ALIBI_SPLIT + [lo] * ALIBI_SPLIT, axis=1))
    pad = lambda t: jnp.pad(t, ((0, 0), (0, 0), (0, LANES - 4 * ALIBI_SPLIT))).astype(BF16)
    fk = jnp.stack(fk)
    tq = min(DA_TQ, seq)
    key_tile = (pos // tq)[None, :, None]
    signed = [pad(fk * jnp.where(key_tile < qi, 1.0, jnp.where(key_tile > qi, -1.0, 0.0))) for qi in range(seq // tq)]
    return pad(jnp.stack(fq)), signed


def _da_kernel(q_ref, k_ref, v_ref, fq_ref, fk_ref, lam_ref, g_ref, o_ref, *, qi, lam_init):
    tq = q_ref.shape[1]
    seq = k_ref.shape[1]
    lv = lam_ref[...]
    lam = (jnp.exp(jnp.sum(lv[0:1] * lv[1:2], axis=1, keepdims=True))
           - jnp.exp(jnp.sum(lv[2:3] * lv[3:4], axis=1, keepdims=True)) + lam_init)
    local = jnp.abs(lax.broadcasted_iota(jnp.int32, (tq, tq), 0)
                    - lax.broadcasted_iota(jnp.int32, (tq, tq), 1)).astype(F32)
    slabs = []
    for s in range(DA_WIDTH // LANES):
        cols = slice(s * LANES, (s + 1) * LANES)
        q_slab = q_ref[0, :, cols]
        k_slab = k_ref[0, :, cols]
        v_slab = v_ref[0, :, cols]
        o_slab = jnp.zeros((tq, LANES), F32)
        for hh in range(LANES // HEAD_DIM):
            head = s * (LANES // HEAD_DIM) + hh
            k_aug = jnp.concatenate([k_slab, fk_ref[head]], axis=1)
            mine = _lane_mask(hh * HEAD_DIM, HEAD_DIM)
            v_ones = jnp.where(mine, v_slab, jnp.ones_like(v_slab))
            maps = []
            for c in range(2):
                keep = jnp.where(_lane_mask(hh * HEAD_DIM + c * DA_HALF, DA_HALF), 1.0, 0.0).astype(BF16)
                q_aug = jnp.concatenate([q_slab * keep, fq_ref[head]], axis=1)
                sc = _dot_nt(q_aug, k_aug)
                pieces = [sc[:, qi * tq:(qi + 1) * tq] - _da_coef(head) * local]
                if qi > 0:
                    pieces.insert(0, sc[:, :qi * tq])
                if (qi + 1) * tq < seq:
                    pieces.append(sc[:, (qi + 1) * tq:])
                sc = jnp.concatenate(pieces, axis=1)
                p = jnp.exp2(sc - jnp.max(sc, axis=1, keepdims=True))
                o2 = _dot(p.astype(BF16), v_ones)
                maps.append((o2, o2[:, (1 - hh) * HEAD_DIM:(1 - hh) * HEAD_DIM + 1]))
            diff = maps[0][0] * (1.0 / maps[0][1]) - maps[1][0] * (lam / maps[1][1])
            o_slab = jnp.where(mine, diff, o_slab)
        slabs.append(o_slab)
    o = jnp.concatenate(slabs, axis=1)
    y = o * lax.rsqrt(_head_ssq(o) * (1.0 / HEAD_DIM) + EPS) * g_ref[...]
    o_ref[0] = (y * (1.0 - lam_init)).astype(BF16)


DA_TQ = 512


def _diff_attn(z, fq, fk_signed, lam_vecs, g_tiled, lam_init):
    nb, seq, _ = z.shape
    tq = min(DA_TQ, seq)
    cb = COL_DA // DA_WIDTH
    tiles = []
    for qi in range(seq // tq):
        tiles.append(pl.pallas_call(
            functools.partial(_da_kernel, qi=qi, lam_init=lam_init),
            grid=(nb,),
            in_specs=[
                pl.BlockSpec((1, tq, DA_WIDTH), lambda b, qi=qi: (b, qi, cb)),
                pl.BlockSpec((1, seq, DA_WIDTH), lambda b: (b, 0, cb + 1)),
                pl.BlockSpec((1, seq, DA_WIDTH), lambda b: (b, 0, cb + 2)),
                pl.BlockSpec((DA_HEADS, tq, LANES), lambda b, qi=qi: (0, qi, 0)),
                pl.BlockSpec((DA_HEADS, seq, LANES), lambda b: (0, 0, 0)),
                pl.BlockSpec((4, DA_HALF), lambda b: (0, 0)),
                pl.BlockSpec((1, DA_WIDTH), lambda b: (0, 0)),
            ],
            out_specs=pl.BlockSpec((1, tq, DA_WIDTH), lambda b: (b, 0, 0)),
            out_shape=jax.ShapeDtypeStruct((nb, tq, DA_WIDTH), BF16),
            compiler_params=_params(("arbitrary",)),
            name="diff_attn",
        )(z, z, z, fq, fk_signed[qi], lam_vecs, g_tiled))
    return jnp.concatenate(tiles, axis=1)


def _rope(x, cos, sin):
    w = x.shape[1]
    lane = lax.broadcasted_iota(jnp.int32, (1, w), 1)
    half = ROPE_AXIS // 2
    partner = jnp.where(lane % ROPE_AXIS < half, pltpu.roll(x, w - half, 1), pltpu.roll(x, half, 1))
    return x * cos + partner * sin


def _dup_heads(x):
    first = _lane_mask(0, HEAD_DIM)
    a = jnp.where(first, x, 0.0)
    b = jnp.where(first, 0.0, x)
    return jnp.concatenate([a + pltpu.roll(a, HEAD_DIM, 1), b + pltpu.roll(b, HEAD_DIM, 1)], axis=1)


def _with_ones(v):
    first = _lane_mask(0, HEAD_DIM)
    swapped = pltpu.roll(v, HEAD_DIM, 1)
    return jnp.concatenate([jnp.where(first, v, 1.0), jnp.where(first, 1.0, swapped),
                            jnp.where(first, swapped, 1.0), jnp.where(first, 1.0, v)], axis=1)


def _gqa_kernel(zq_ref, zkv_ref, cosq_ref, sinq_ref, cosk_ref, sink_ref, gq_ref, gk_ref, o_ref, kd_ref, vd_ref):
    tq = zq_ref.shape[1]
    kvw = GQ_KV * HEAD_DIM
    per_slab = LANES // HEAD_DIM
    group = GQ_HEADS // GQ_KV

    @pl.when(pl.program_id(1) == 0)
    def _():
        k = zkv_ref[0, :, :kvw].astype(F32)
        kn = k * lax.rsqrt(_head_ssq(k) * (1.0 / HEAD_DIM) + EPS) * gk_ref[...]
        kd_ref[...] = _dup_heads(_rope(kn, cosk_ref[...], sink_ref[...])).astype(BF16)
        vd_ref[...] = _with_ones(zkv_ref[0, :, kvw:].astype(F32)).astype(BF16)

    q = zq_ref[0].astype(F32)
    qn = q * lax.rsqrt(_head_ssq(q) * (1.0 / HEAD_DIM) + EPS) * gq_ref[...]
    qr = (_rope(qn, cosq_ref[...], sinq_ref[...]) * (HEAD_DIM ** -0.5 * math.log2(math.e))).astype(BF16)
    slabs = []
    for s in range(GQ_WIDTH // LANES):
        q_slab = qr[:, s * LANES:(s + 1) * LANES]
        kv = (s * per_slab) // group
        k_dup = kd_ref[:, kv * LANES:(kv + 1) * LANES]
        o_slab = jnp.zeros((tq, LANES), F32)
        for hh in range(per_slab):
            keep = jnp.where(_lane_mask(hh * HEAD_DIM, HEAD_DIM), 1.0, 0.0).astype(BF16)
            sc = _dot_nt(q_slab * keep, k_dup)
            p = jnp.exp2(sc - jnp.max(sc, axis=1, keepdims=True))
            v_ones = vd_ref[:, (kv * per_slab + hh) * LANES:(kv * per_slab + hh + 1) * LANES]
            o2 = _dot(p.astype(BF16), v_ones)
            sums = o2[:, (1 - hh) * HEAD_DIM:(1 - hh) * HEAD_DIM + 1]
            o_slab = jnp.where(_lane_mask(hh * HEAD_DIM, HEAD_DIM), o2 * (1.0 / sums), o_slab)
        slabs.append(o_slab)
    o_ref[0] = jnp.concatenate(slabs, axis=1).astype(BF16)


def _gqa_attn(z, cos, sin, gq, gk):
    nb, seq, _ = z.shape
    tq = min(512, seq)
    kvw = GQ_KV * HEAD_DIM
    return pl.pallas_call(
        _gqa_kernel,
        grid=(nb, seq // tq),
        in_specs=[
            pl.BlockSpec((1, tq, GQ_WIDTH), lambda b, i: (b, i, COL_GQ // GQ_WIDTH)),
            pl.BlockSpec((1, seq, 2 * kvw), lambda b, i: (b, 0, (COL_GQ + GQ_WIDTH) // (2 * kvw))),
            pl.BlockSpec((tq, GQ_WIDTH), lambda b, i: (i, 0)),
            pl.BlockSpec((tq, GQ_WIDTH), lambda b, i: (i, 0)),
            pl.BlockSpec((seq, kvw), lambda b, i: (0, 0)),
            pl.BlockSpec((seq, kvw), lambda b, i: (0, 0)),
            pl.BlockSpec((1, GQ_WIDTH), lambda b, i: (0, 0)),
            pl.BlockSpec((1, kvw), lambda b, i: (0, 0)),
        ],
        out_specs=pl.BlockSpec((1, tq, GQ_WIDTH), lambda b, i: (b, i, 0)),
        out_shape=jax.ShapeDtypeStruct((nb, seq, GQ_WIDTH), BF16),
        scratch_shapes=[pltpu.VMEM((seq, 2 * kvw), BF16), pltpu.VMEM((seq, 4 * kvw), BF16)],
        compiler_params=_params(("arbitrary", "arbitrary")),
        name="gqa_attn",
    )(z, z, cos, sin, cos, sin, gq, gk)


def _merge_kernel(ya_ref, yb_ref, yc_ref, zg_ref, x_ref, mod_ref, wa_ref, wb_ref, wc_ref, wo_ref,
                  lg_ref, lb_ref, *rest):
    d = D_MODEL
    m = (jnp.tanh(zg_ref[0, :, 0:d].astype(F32)) + 1.0) * _dot(ya_ref[0], wa_ref[...])
    m = m + (jnp.tanh(zg_ref[0, :, d:2 * d].astype(F32)) + 1.0) * _dot(yb_ref[0], wb_ref[...])
    m = m + (jnp.tanh(zg_ref[0, :, 2 * d:3 * d].astype(F32)) + 1.0) * _dot(yc_ref[0], wc_ref[...])
    y = _dot(m.astype(BF16), wo_ref[...])
    gate1 = 1.0 + mod_ref[0, 2:3, :]
    xn = _ln(DEEPNORM_ALPHA * x_ref[0] + gate1 * y) * lg_ref[...] + lb_ref[...]
    h = _ln(xn) * (1.0 + mod_ref[0, 4:5, :]) + mod_ref[0, 3:4, :]
    if len(rest) == 2:
        xo_ref, h_ref = rest
    else:
        wr_ref, br_ref, xo_ref, h_ref, comb_ref = rest
        lane = lax.broadcasted_iota(jnp.int32, (1, LANES), 1).astype(F32)
        neg = -jnp.inf
        logits = jnp.where(lane < N_EXPERTS, _dot3(h, wr_ref[...]) + br_ref[...], neg)
        v1 = jnp.max(logits, axis=1, keepdims=True)
        i1 = jnp.min(jnp.where(logits == v1, lane, float(LANES)), axis=1, keepdims=True)
        others = jnp.where(lane == i1, neg, logits)
        v2 = jnp.max(others, axis=1, keepdims=True)
        i2 = jnp.min(jnp.where(others == v2, lane, float(LANES)), axis=1, keepdims=True)
        e = jnp.exp(v2 - v1)
        w1 = 1.0 / (1.0 + e)
        comb_ref[0] = jnp.where(lane == i1, w1, 0.0) + jnp.where(lane == i2, e * w1, 0.0)
    xo_ref[0] = xn
    h_ref[0] = h.astype(BF16)


def _merge(ya, yb, yc, z, x, mod, wa, wb, wc, wo, lg, lb, router=None):
    nb, seq, d = x.shape
    tm = min(512, seq)
    full = lambda shape: pl.BlockSpec(shape, lambda b, i: (0,) * len(shape))
    row = lambda w, col=0: pl.BlockSpec((1, tm, w), lambda b, i: (b, i, col))
    in_specs = [
        row(HY_WIDTH), row(DA_WIDTH), row(GQ_WIDTH), row(GATE_COLS, COL_GATE // GATE_COLS), row(d),
        pl.BlockSpec((1, 6, d), lambda b, i: (b, 0, 0)),
        full((HY_WIDTH, d)), full((DA_WIDTH, d)), full((GQ_WIDTH, d)), full((d, d)),
        full((1, d)), full((1, d)),
    ]
    out_specs = [row(d), row(d)]
    out_shape = [jax.ShapeDtypeStruct((nb, seq, d), F32), jax.ShapeDtypeStruct((nb, seq, d), BF16)]
    args = [ya, yb, yc, z, x, mod, wa, wb, wc, wo, lg, lb]
    if router is not None:
        in_specs += [full((d, LANES)), full((1, LANES))]
        out_specs.append(row(LANES))
        out_shape.append(jax.ShapeDtypeStruct((nb, seq, LANES), F32))
        args += list(router)
    return pl.pallas_call(
        _merge_kernel,
        grid=(nb, seq // tm),
        in_specs=in_specs,
        out_specs=out_specs,
        out_shape=out_shape,
        compiler_params=_params(("arbitrary", "arbitrary")),
        name="merge_outproj",
    )(*args)


FFN_TF = 256


def _ffn_kernel(h_ref, wup_ref, wd_ref, x_ref, mod_ref, lg_ref, lb_ref, o_ref):
    h = h_ref[0]
    ff = wd_ref.shape[0]
    acc = None
    for c in range(ff // FFN_TF):
        cols = slice(c * FFN_TF, (c + 1) * FFN_TF)
        g = _dot(h, wup_ref[:, cols])
        u = _dot(h, wup_ref[:, ff + c * FFN_TF:ff + (c + 1) * FFN_TF])
        part = _dot((g * _sigmoid(g) * u).astype(BF16), wd_ref[cols, :])
        acc = part if acc is None else acc + part
    gate2 = 1.0 + mod_ref[0, 5:6, :]
    o_ref[0] = _ln(DEEPNORM_ALPHA * x_ref[0] + gate2 * acc) * lg_ref[...] + lb_ref[...]


def _ffn(h, w_up, w_down, x, mod, lg, lb):
    nb, seq, d = x.shape
    ff = w_down.shape[0]
    assert ff % FFN_TF == 0
    tm = min(1024, seq)
    row = lambda w: pl.BlockSpec((1, tm, w), lambda b, i: (b, i, 0))
    vec = pl.BlockSpec((1, d), lambda b, i: (0, 0))
    once = pl.Buffered(1)
    return pl.pallas_call(
        _ffn_kernel,
        grid=(nb, seq // tm),
        in_specs=[
            row(d),
            pl.BlockSpec((d, 2 * ff), lambda b, i: (0, 0), pipeline_mode=once),
            pl.BlockSpec((ff, d), lambda b, i: (0, 0), pipeline_mode=once),
            row(d),
            pl.BlockSpec((1, 6, d), lambda b, i: (b, 0, 0)),
            vec, vec,
        ],
        out_specs=row(d),
        out_shape=jax.ShapeDtypeStruct((nb, seq, d), F32),
        compiler_params=_params(("arbitrary",) * 2),
        name="swiglu_dense",
    )(h, w_up, w_down, x, mod, lg, lb)


MOE_TOKENS = 1024
MOE_SLOT = 384
MOE_CLASSES = (256, 320, MOE_SLOT)


def _moe_kernel(h_ref, comb_ref, wg_ref, wu_ref, wd_ref, x_ref, mod_ref, lg_ref, lb_ref, o_ref,
                posc_ref, posr_ref, cnt_ref, ys_ref):
    e = pl.program_id(2)
    tm = h_ref.shape[1]
    n_exp = ys_ref.shape[0] // MOE_SLOT
    lane = lax.broadcasted_iota(jnp.int32, (1, LANES), 1)

    @pl.when(e == 0)
    def _():
        comb = comb_ref[0]
        sel = jnp.where(comb > 0.0, 1.0, 0.0)
        r = lax.broadcasted_iota(jnp.int32, (tm, tm), 0)
        c = lax.broadcasted_iota(jnp.int32, (tm, tm), 1)
        before = jnp.where(c < r, 1.0, 0.0).astype(BF16)
        rank = _dot(before, sel.astype(BF16))
        posc = jnp.where(comb > 0.0, rank, -1.0)
        posc_ref[...] = posc
        posr_ref[...] = posc.T
        cnt_ref[...] = jnp.sum(sel, axis=0, keepdims=True)
        o_ref[0] = jnp.zeros(o_ref.shape[1:], F32)
        ys_ref[...] = jnp.zeros_like(ys_ref)

    mine = lane == e
    pos_row = posr_ref[pl.ds(e, 1), :]
    count = jnp.sum(jnp.where(mine, cnt_ref[...], 0.0))

    def expert_rows(base, rows):
        slot_r = (base + lax.broadcasted_iota(jnp.int32, (rows, 1), 0)).astype(F32)
        gather = jnp.where(pos_row == slot_r, 1.0, 0.0).astype(BF16)
        xs = _dot(gather, h_ref[0]).astype(BF16)
        g = _dot(xs, wg_ref[...])
        u = _dot(xs, wu_ref[...])
        a = (g * _sigmoid(g) * u).astype(BF16)
        return _dot(a, wd_ref[...]).astype(BF16)

    def fill_slot(rows):
        ys_ref[pl.ds(pl.multiple_of(e * MOE_SLOT, MOE_SLOT), rows), :] = expert_rows(0, rows)

    bounds = (0,) + MOE_CLASSES
    for lo, rows in zip(bounds[:-1], bounds[1:]):
        in_class = (count > lo) & (count <= rows) if rows < MOE_SLOT else count > lo
        pl.when(in_class)(functools.partial(fill_slot, rows))

    for c in range(1, -(-tm // MOE_SLOT)):
        @pl.when(count > c * MOE_SLOT)
        def _():
            pos_col = jnp.sum(jnp.where(mine, posc_ref[...], 0.0), axis=1, keepdims=True)
            w_col = jnp.sum(jnp.where(mine, comb_ref[0], 0.0), axis=1, keepdims=True)
            slot_c = (c * MOE_SLOT + lax.broadcasted_iota(jnp.int32, (1, MOE_SLOT), 1)).astype(F32)
            scatter = jnp.where(pos_col == slot_c, 1.0, 0.0).astype(BF16)
            o_ref[0] += w_col * _dot(scatter, expert_rows(c * MOE_SLOT, MOE_SLOT))

    @pl.when(e == pl.num_programs(2) - 1)
    def _():
        slot_c = lax.broadcasted_iota(jnp.int32, (1, MOE_SLOT), 1).astype(F32)
        f = o_ref[0]
        for e0 in range(0, n_exp, 2):
            weights = jnp.concatenate(
                [jnp.where(posc_ref[:, ee:ee + 1] == slot_c, comb_ref[0, :, ee:ee + 1], 0.0).astype(BF16)
                 for ee in (e0, e0 + 1)], axis=1)
            f = f + _dot(weights, ys_ref[e0 * MOE_SLOT:(e0 + 2) * MOE_SLOT, :])
        gate2 = 1.0 + mod_ref[0, 5:6, :]
        o_ref[0] = _ln(DEEPNORM_ALPHA * x_ref[0] + gate2 * f) * lg_ref[...] + lb_ref[...]


def _moe(h, comb, w_up, w_down, x, mod, lg, lb):
    nb, seq, d = x.shape
    ne, ff, _ = w_down.shape
    tm = min(MOE_TOKENS, seq)
    row = lambda w: pl.BlockSpec((1, tm, w), lambda b, i, e: (b, i, 0))
    vec = pl.BlockSpec((1, d), lambda b, i, e: (0, 0))
    return pl.pallas_call(
        _moe_kernel,
        grid=(nb, seq // tm, ne),
        in_specs=[
            row(d), row(LANES),
            pl.BlockSpec((None, d, ff), lambda b, i, e: (e, 0, 0)),
            pl.BlockSpec((None, d, ff), lambda b, i, e: (e, 0, 1)),
            pl.BlockSpec((None, ff, d), lambda b, i, e: (e, 0, 0)),
            row(d),
            pl.BlockSpec((1, 6, d), lambda b, i, e: (b, 0, 0)),
            vec, vec,
        ],
        out_specs=row(d),
        out_shape=jax.ShapeDtypeStruct((nb, seq, d), F32),
        scratch_shapes=[pltpu.VMEM((tm, LANES), F32), pltpu.VMEM((LANES, tm), F32), pltpu.VMEM((1, LANES), F32),
                        pltpu.VMEM((ne * MOE_SLOT, d), BF16)],
        compiler_params=_params(("arbitrary",) * 3),
        name="swiglu_routed",
    )(h, comb, w_up, w_up, w_down, x, mod, lg, lb)


def _dft_tables(seq):
    n = 2 * seq
    tf = min(CONV_TF, seq)
    nk = seq // tf
    s = jnp.arange(n, dtype=jnp.int32)[None, :]

    def phase(mult):
        ang = ((mult * s) % n).astype(F32) * (2.0 * math.pi / n)
        return jnp.cos(ang), jnp.sin(ang)

    ca, sa = phase(jnp.arange(nk, dtype=jnp.int32)[:, None] * tf)
    cb, sb = phase(jnp.arange(tf, dtype=jnp.int32)[:, None])
    cos = ca[:, None, :] * cb[None] - sa[:, None, :] * sb[None]
    msin = -(sa[:, None, :] * cb[None] + ca[:, None, :] * sb[None])
    k = (jnp.arange(nk, dtype=jnp.int32)[:, None] * tf + jnp.arange(tf, dtype=jnp.int32)[None, :])[:, :, None]
    msin = jnp.where(k == 0, jnp.where(s % 2 == 0, 1.0, -1.0)[None], msin)
    fwd = jnp.concatenate([cos, msin], axis=1)
    scale = jnp.where(k == 0, 1.0 / n, 2.0 / n).astype(F32)
    inv = jnp.concatenate([cos[:, :, :seq] * scale, msin[:, :, :seq] * scale], axis=1).transpose(0, 2, 1)
    return fwd.astype(BF16), inv.astype(BF16)


def _rope_tables(seq):
    t = jnp.arange(seq, dtype=jnp.int32)
    inv = ROPE_THETA ** (-jnp.arange(0, ROPE_AXIS, 2, dtype=F32) / ROPE_AXIS)
    ang_r = (t // GRID_W).astype(F32)[:, None] * inv[None, :]
    ang_c = (t % GRID_W).astype(F32)[:, None] * inv[None, :]
    cos = jnp.concatenate([jnp.cos(ang_r)] * 2 + [jnp.cos(ang_c)] * 2, axis=1)
    sin = jnp.concatenate([-jnp.sin(ang_r), jnp.sin(ang_r), -jnp.sin(ang_c), jnp.sin(ang_c)], axis=1)
    return jnp.tile(cos, (1, GQ_HEADS)), jnp.tile(sin, (1, GQ_HEADS))


def _pad_to(a, rows, cols):
    return jnp.pad(a, ((0, rows - a.shape[0]), (0, cols - a.shape[1])))


def _prepare(seq, p):
    fwd, inv = _dft_tables(seq)
    cos, sin = _rope_tables(seq)
    bands = jnp.linspace(1e-4, N_BANDS - 1, N_BANDS, dtype=F32)
    bands_row = jnp.zeros((1, LANES), F32).at[0, 1:1 + N_BANDS].set(bands).at[0, 1 + N_BANDS:1 + 2 * N_BANDS].set(bands)
    deltas = jnp.abs(jnp.linspace(MIN_DECAY, MAX_DECAY, HY_WIDTH, dtype=F32))
    dec = jnp.tile(deltas, HY_ORDER)[None, :]
    da_scale = DA_HALF ** -0.5 * math.log2(math.e)
    fq, fk = _alibi_tables(seq)
    layers = []
    for l in range(DEPTH):
        w_in, b_in = p['w_in'][l], p['b_in'][l]
        a0, b0, c0, g0 = 0, HY_COLS, HY_COLS + DA_COLS, HY_COLS + DA_COLS + GQ_COLS
        w_cols = [w_in[:, g0:] * 0.5, w_in[:, a0:b0], w_in[:, b0:b0 + DA_WIDTH] * da_scale, w_in[:, b0 + DA_WIDTH:c0], w_in[:, c0:g0]]
        b_cols = [b_in[g0:] * 0.5, b_in[a0:b0], b_in[b0:b0 + DA_WIDTH] * da_scale, b_in[b0 + DA_WIDTH:c0], b_in[c0:g0]]
        ts = _filters_time(
            seq, bands_row,
            _pad_to(p['hy_f_w1'][l], LANES, LANES), _pad_to(p['hy_f_b1'][l][None, :], 1, LANES),
            _pad_to(p['hy_f_w2'][l], LANES, LANES), _pad_to(p['hy_f_b2'][l][None, :], 1, LANES),
            _pad_to(p['hy_f_w3'][l], LANES, 2 * HY_ORDER * HY_WIDTH), p['hy_f_b3'][l][None, :], dec)
        kf = _filter_dft(fwd.reshape(2 * seq, 2 * seq), ts).reshape(fwd.shape[0], -1, HY_ORDER * HY_WIDTH)
        lay = dict(
            w_in=jnp.concatenate(w_cols, axis=1).astype(BF16),
            b_in=jnp.concatenate(b_cols)[None, :],
            conv_w=p['hy_conv_w'][l], conv_b=p['hy_conv_b'][l][None, :],
            kf=kf, hy_bias=p['hy_bias'][l],
            lam=jnp.stack([p['da_lam_q1'][l], p['da_lam_k1'][l], p['da_lam_q2'][l], p['da_lam_k2'][l]]),
            lam_init=0.8 - 0.6 * math.exp(-0.3 * l),
            subln=jnp.tile(p['da_subln_g'][l], DA_HEADS)[None, :],
            gq=jnp.tile(p['gq_qnorm_g'][l], GQ_HEADS)[None, :],
            gk=jnp.tile(p['gq_knorm_g'][l], GQ_KV)[None, :],
            wa=(p['w_br_a'][l] * 0.5).astype(BF16), wb=(p['w_br_b'][l] * 0.5).astype(BF16),
            wc=(p['w_br_c'][l] * 0.5).astype(BF16),
            wo=p['w_out'][l].astype(BF16),
            ln1_g=p['ln1_g'][l][None, :], ln1_b=p['ln1_b'][l][None, :],
            ln2_g=p['ln2_g'][l][None, :], ln2_b=p['ln2_b'][l][None, :],
        )
        if l % 2 == 0:
            lay.update(w_up=p['ffn_w_up'][l // 2].astype(BF16), w_down=p['ffn_w_down'][l // 2].astype(BF16))
        else:
            lay.update(w_up=p['moe_w_up'][l // 2].astype(BF16), w_down=p['moe_w_down'][l // 2].astype(BF16),
                       router=(_pad_to(p['moe_w_router'][l // 2], D_MODEL, LANES),
                               _pad_to(p['moe_b_router'][l // 2][None, :], 1, LANES)))
        layers.append(lay)
    return dict(fwd=fwd, inv=inv, cos=cos, sin=sin, fq=fq, fk=fk, layers=layers)


def _trunk(x, c, p, prep):
    mods = _ada(c, p['w_ada'], p['b_ada'])
    nb = x.shape[0]
    for l, lay in enumerate(prep['layers']):
        mod = mods[l].reshape(nb, 6, D_MODEL)
        z = _inproj(x, mod, lay['w_in'], lay['b_in'], lay['conv_w'], lay['conv_b'])
        hy = COL_HY // HY_WIDTH
        u = _longconv(z, hy + 2, z, hy, prep['fwd'], prep['inv'], lay['kf'], 0, lay['hy_bias'][0:1])
        ya = _longconv(u, 0, z, hy + 1, prep['fwd'], prep['inv'], lay['kf'], 1, lay['hy_bias'][1:2])
        yb = _diff_attn(z, prep['fq'], prep['fk'], lay['lam'], lay['subln'], lay['lam_init'])
        yc = _gqa_attn(z, prep['cos'], prep['sin'], lay['gq'], lay['gk'])
        merged = _merge(ya, yb, yc, z, x, mod, lay['wa'], lay['wb'], lay['wc'], lay['wo'],
                        lay['ln1_g'], lay['ln1_b'], lay.get('router'))
        if l % 2 == 0:
            x, h = merged
            x = _ffn(h, lay['w_up'], lay['w_down'], x, mod, lay['ln2_g'], lay['ln2_b'])
        else:
            x, h, comb = merged
            x = _moe(h, comb, lay['w_up'], lay['w_down'], x, mod, lay['ln2_g'], lay['ln2_b'])
    return x


def kernel(x_prompt, x_sample, c_prompt, c_sample, w_ada, b_ada, w_in, b_in, hy_conv_w, hy_conv_b, hy_f_w1, hy_f_b1, hy_f_w2, hy_f_b2, hy_f_w3, hy_f_b3, hy_bias, da_lam_q1, da_lam_k1, da_lam_q2, da_lam_k2, da_subln_g, gq_qnorm_g, gq_knorm_g, w_br_a, w_br_b, w_br_c, w_out, ln1_g, ln1_b, ffn_w_up, ffn_w_down, moe_w_router, moe_b_router, moe_w_up, moe_w_down, ln2_g, ln2_b):
    p = dict(w_ada=w_ada, b_ada=b_ada, w_in=w_in, b_in=b_in,
             hy_conv_w=hy_conv_w, hy_conv_b=hy_conv_b, hy_f_w1=hy_f_w1, hy_f_b1=hy_f_b1,
             hy_f_w2=hy_f_w2, hy_f_b2=hy_f_b2, hy_f_w3=hy_f_w3, hy_f_b3=hy_f_b3, hy_bias=hy_bias,
             da_lam_q1=da_lam_q1, da_lam_k1=da_lam_k1, da_lam_q2=da_lam_q2, da_lam_k2=da_lam_k2,
             da_subln_g=da_subln_g, gq_qnorm_g=gq_qnorm_g, gq_knorm_g=gq_knorm_g,
             w_br_a=w_br_a, w_br_b=w_br_b, w_br_c=w_br_c, w_out=w_out, ln1_g=ln1_g, ln1_b=ln1_b,
             ffn_w_up=ffn_w_up, ffn_w_down=ffn_w_down, moe_w_router=moe_w_router,
             moe_b_router=moe_b_router, moe_w_up=moe_w_up, moe_w_down=moe_w_down,
             ln2_g=ln2_g, ln2_b=ln2_b)
    assert x_prompt.shape[1] == x_sample.shape[1]
    prep = _prepare(x_prompt.shape[1], p)
    return (_trunk(x_prompt, c_prompt, p, prep), _trunk(x_sample, c_sample, p, prep))
```

```python
import functools
import math

import jax
import jax.numpy as jnp
from jax import lax
from jax.experimental import pallas as pl
from jax.experimental.pallas import tpu as pltpu

F32 = jnp.float32
BF16 = jnp.bfloat16

D_MODEL = 1024
DEPTH = 2
GRID_W = 64
HEAD_DIM = 64
HY_WIDTH = D_MODEL // 4
HY_ORDER = 2
SHORT_CONV = 3
N_BANDS = 16
FILTER_HID = 64
DECAY_TARGET = 1e-2
MIN_DECAY = math.log(DECAY_TARGET) / 1.5
MAX_DECAY = math.log(DECAY_TARGET) / 0.3
DECAY_SHIFT = 0.05
DA_HEADS = 4
DA_HALF = HEAD_DIM // 2
DA_WIDTH = DA_HEADS * HEAD_DIM
GQ_HEADS = 8
GQ_KV = 2
GQ_WIDTH = GQ_HEADS * HEAD_DIM
ROPE_AXIS = HEAD_DIM // 2
ROPE_THETA = 10000.0
HY_COLS = (HY_ORDER + 1) * HY_WIDTH
DA_COLS = 3 * DA_WIDTH
GQ_COLS = GQ_WIDTH + 2 * GQ_KV * HEAD_DIM
GATE_COLS = 3 * D_MODEL
IN_COLS = HY_COLS + DA_COLS + GQ_COLS + GATE_COLS
D_FF = 256 * ((8 * D_MODEL // 3 + 255) // 256)
N_EXPERTS = 8
MOE_FF = D_FF // 2
DEEPNORM_ALPHA = (2.0 * DEPTH) ** 0.25
EPS = 1e-5

LANES = 128
VMEM_LIMIT = 56 * 1024 * 1024

COL_GATE = 0
COL_HY = GATE_COLS
COL_DA = COL_HY + HY_COLS
COL_GQ = COL_DA + DA_COLS


def _params(sem):
    return pltpu.CompilerParams(dimension_semantics=sem, vmem_limit_bytes=VMEM_LIMIT)


def _dot(a, b):
    return jnp.dot(a, b, preferred_element_type=F32)


def _dot_nt(a, b):
    return lax.dot_general(a, b, (((1,), (1,)), ((), ())), preferred_element_type=F32)


def _split(x):
    hi = x.astype(BF16)
    lo = (x - hi.astype(F32)).astype(BF16)
    return hi, lo


def _dot3(a, b):
    ah, al = _split(a)
    bh, bl = _split(b)
    return _dot(ah, bh) + (_dot(ah, bl) + _dot(al, bh))


def _sigmoid(x):
    return 0.5 * jnp.tanh(0.5 * x) + 0.5


def _ln(x):
    mu = jnp.mean(x, axis=-1, keepdims=True)
    xc = x - mu
    var = jnp.mean(xc * xc, axis=-1, keepdims=True)
    return xc * lax.rsqrt(var + EPS)


def _head_ssq(x):
    w = x.shape[1]
    r = lax.broadcasted_iota(jnp.int32, (w, w), 0) // HEAD_DIM
    c = lax.broadcasted_iota(jnp.int32, (w, w), 1) // HEAD_DIM
    ones = jnp.where(r == c, 1.0, 0.0).astype(BF16)
    hi, lo = _split(x * x)
    return _dot(hi, ones) + _dot(lo, ones)


def _ada_kernel(c_ref, w_ref, b_ref, o_ref):
    c = c_ref[...]
    o_ref[...] = _dot3(c * _sigmoid(c), w_ref[...]) + b_ref[...]


def _ada(c, w_ada, b_ada):
    nb = c.shape[0]
    tn = 1536
    return pl.pallas_call(
        _ada_kernel,
        grid=(DEPTH, 6 * D_MODEL // tn),
        in_specs=[
            pl.BlockSpec((nb, D_MODEL), lambda l, j: (0, 0)),
            pl.BlockSpec((None, D_MODEL, tn), lambda l, j: (l, 0, j)),
            pl.BlockSpec((None, 1, tn), lambda l, j: (l, 0, j)),
        ],
        out_specs=pl.BlockSpec((None, nb, tn), lambda l, j: (l, 0, j)),
        out_shape=jax.ShapeDtypeStruct((DEPTH, nb, 6 * D_MODEL), F32),
        compiler_params=_params(("arbitrary", "arbitrary")),
        name="ada_mod",
    )(c, w_ada, b_ada.reshape(DEPTH, 1, 6 * D_MODEL))


LN_ROWS = 256
INPROJ_TN = 768
INPROJ_STEPS = IN_COLS // INPROJ_TN
HY_TILE = COL_HY // INPROJ_TN


def _inproj_kernel(x_ref, mod_ref, w_ref, b_ref, cw_ref, cb_ref, o_ref, ha_ref, hb_ref):
    b = pl.program_id(0)
    j = pl.program_id(1)
    seq = x_ref.shape[1]
    chunks = seq // LN_ROWS
    assert chunks <= INPROJ_STEPS + 1

    @pl.when((b == 0) & (j == 0))
    def _():
        hb_ref[...] = jnp.zeros_like(hb_ref)

    def step(h_ln, h_mm, ln_chunks, conv, col):
        shift = mod_ref[0, 0:1, :]
        scale = 1.0 + mod_ref[0, 1:2, :]
        for c in ln_chunks:
            r = pl.ds(pl.multiple_of(c * LN_ROWS, LN_ROWS), LN_ROWS)
            h_ln[r, :] = (_ln(x_ref[0, r, :]) * scale + shift).astype(BF16)
        cols = pl.ds(pl.multiple_of(col * INPROJ_TN, INPROJ_TN), INPROJ_TN)
        z = _dot(h_mm[...], w_ref[:, cols]) + b_ref[:, cols]
        if conv:
            row = lax.broadcasted_iota(jnp.int32, z.shape, 0)
            prev = jnp.where(row == 0, 0.0, pltpu.roll(z, 1, 0))
            nxt = jnp.where(row == seq - 1, 0.0, pltpu.roll(z, seq - 1, 0))
            z = cb_ref[...] + prev * cw_ref[0:1, :] + z * cw_ref[1:2, :] + nxt * cw_ref[2:3, :]
        o_ref[0] = z.astype(BF16)

    first = [0] + ([INPROJ_STEPS] if chunks > INPROJ_STEPS else [])
    later = [jnp.minimum(j, chunks - 1)]
    for parity, (h_ln, h_mm) in enumerate(((ha_ref, hb_ref), (hb_ref, ha_ref))):
        even = b % 2 == parity
        pl.when(even & (j == 0))(functools.partial(step, h_ln, h_mm, first, False, 0))
        pl.when(even & (j == HY_TILE))(functools.partial(step, h_ln, h_mm, later, True, HY_TILE))
        pl.when(even & (j != 0) & (j != HY_TILE))(functools.partial(step, h_ln, h_mm, later, False, j))


def _inproj(x, mod, w, b, conv_w, conv_b):
    nb, seq, _ = x.shape
    tn = INPROJ_TN
    assert HY_COLS == tn and COL_HY % tn == 0 and HY_TILE != 0 and seq % LN_ROWS == 0
    cur = lambda i, j: (jnp.minimum(i, nb - 1), 0, 0)
    return pl.pallas_call(
        _inproj_kernel,
        grid=(nb + 1, INPROJ_STEPS),
        in_specs=[
            pl.BlockSpec((1, seq, D_MODEL), cur),
            pl.BlockSpec((1, 6, D_MODEL), cur),
            pl.BlockSpec((D_MODEL, IN_COLS), lambda i, j: (0, 0), pipeline_mode=pl.Buffered(1)),
            pl.BlockSpec((1, IN_COLS), lambda i, j: (0, 0)),
            pl.BlockSpec((SHORT_CONV, tn), lambda i, j: (0, 0)),
            pl.BlockSpec((1, tn), lambda i, j: (0, 0)),
        ],
        out_specs=pl.BlockSpec((1, seq, tn), lambda i, j: (jnp.maximum(i - 1, 0), 0, jnp.where(i == 0, 0, j))),
        out_shape=jax.ShapeDtypeStruct((nb, seq, IN_COLS), BF16),
        scratch_shapes=[pltpu.VMEM((seq, D_MODEL), BF16), pltpu.VMEM((seq, D_MODEL), BF16)],
        compiler_params=_params(("arbitrary", "arbitrary")),
        name="ln_inproj",
    )(x, mod, w, b, conv_w, conv_b)


FILT_ROWS = 512


def _filter_kernel(bands_ref, w1_ref, b1_ref, w2_ref, b2_ref, w3_ref, b3_ref, dec_ref, o_ref, ts_ref, *, seq):
    n = 2 * seq
    rows = min(FILT_ROWS, n)
    hw = HY_ORDER * HY_WIDTH
    lane = lax.broadcasted_iota(jnp.int32, (1, LANES), 1)

    def fill(i, asum):
        j0 = pl.multiple_of(i * rows, rows)
        j = j0 + lax.broadcasted_iota(jnp.int32, (rows, 1), 0)
        t = jnp.where(j < seq, j, n - j).astype(F32)
        t_norm = t / max(seq - 1, 1)
        ang = (2.0 * math.pi / seq) * t * bands_ref[...]
        feats = jnp.where(lane == 0, t_norm,
                          jnp.where(lane <= N_BANDS, jnp.cos(ang),
                                    jnp.where(lane <= 2 * N_BANDS, -jnp.sin(ang), 0.0)))
        h = jnp.sin(_dot3(feats, w1_ref[...]) + b1_ref[...])
        h = jnp.sin(_dot3(h, w2_ref[...]) + b2_ref[...])
        h = _dot3(h, w3_ref[...]) + b3_ref[...]
        window = jnp.exp(-t_norm * dec_ref[...]) + DECAY_SHIFT
        sel = jnp.where(j < seq, h[:, :hw], h[:, hw:]) * window
        sel = jnp.where(j == seq, 0.0, sel)
        ts_ref[pl.ds(j0, rows), :] = sel
        return asum + jnp.sum(jnp.abs(sel), axis=0, keepdims=True)

    asum = lax.fori_loop(0, n // rows, fill, jnp.zeros((1, hw), F32))
    inv = 1.0 / (asum + EPS)

    def norm(i, carry):
        r = pl.ds(pl.multiple_of(i * rows, rows), rows)
        o_ref[r, :] = (ts_ref[r, :] * inv).astype(BF16)
        return carry

    lax.fori_loop(0, n // rows, norm, 0)


def _filters_time(seq, bands, w1, b1, w2, b2, w3, b3, dec):
    hw = HY_ORDER * HY_WIDTH
    return pl.pallas_call(
        functools.partial(_filter_kernel, seq=seq),
        out_shape=jax.ShapeDtypeStruct((2 * seq, hw), BF16),
        scratch_shapes=[pltpu.VMEM((2 * seq, hw), F32)],
        compiler_params=pltpu.CompilerParams(vmem_limit_bytes=VMEM_LIMIT),
        name="hy_filter_time",
    )(bands, w1, b1, w2, b2, w3, b3, dec)


def _matmul_kernel(a_ref, b_ref, o_ref):
    o_ref[...] = _dot(a_ref[...], b_ref[...])


def _filter_dft(wf, ts):
    n, hw = ts.shape
    tm = min(512, n)
    return pl.pallas_call(
        _matmul_kernel,
        grid=(n // tm,),
        in_specs=[pl.BlockSpec((tm, n), lambda i: (i, 0)), pl.BlockSpec((n, hw), lambda i: (0, 0))],
        out_specs=pl.BlockSpec((tm, hw), lambda i: (i, 0)),
        out_shape=jax.ShapeDtypeStruct((n, hw), F32),
        compiler_params=_params(("arbitrary",)),
        name="hy_filter_dft",
    )(wf, ts)


def _longconv_kernel(v_ref, g_ref, w_ref, wi_ref, kf_ref, bias_ref, o_ref):
    nb = v_ref.shape[0]
    nk = w_ref.shape[0]
    tf = w_ref.shape[1] // 2
    first = lax.broadcasted_iota(jnp.int32, (tf, 1), 0) == 0
    for n in range(nb):
        v = v_ref[n]
        acc = None
        for kt in range(nk):
            u = _dot(w_ref[kt], v)
            ur, ui = u[:tf], u[tf:]
            kr = kf_ref[kt, :tf, :]
            ki = kf_ref[kt, tf:, :]
            if kt == 0:
                gr = ur * kr - jnp.where(first, 0.0, ui * ki)
                gi = jnp.where(first, ui * ki, ur * ki + ui * kr)
            else:
                gr = ur * kr - ui * ki
                gi = ur * ki + ui * kr
            part = _dot(wi_ref[kt], jnp.concatenate([gr, gi], axis=0).astype(BF16))
            acc = part if acc is None else acc + part
        y = (acc + v.astype(F32) * bias_ref[...]).astype(BF16)
        o_ref[n] = (g_ref[n] * y).astype(BF16)


CONV_TF = 256
CONV_NB = 2


def _longconv(v_arr, v_col, g_arr, g_col, w, wi, kf, order, bias):
    nbatch, seq, _ = v_arr.shape
    nb = math.gcd(nbatch, CONV_NB)
    nk, tf2, _ = w.shape
    c = HY_WIDTH
    once = pl.Buffered(1)
    return pl.pallas_call(
        _longconv_kernel,
        grid=(nbatch // nb,),
        in_specs=[
            pl.BlockSpec((nb, seq, c), lambda i: (i, 0, v_col)),
            pl.BlockSpec((nb, seq, c), lambda i: (i, 0, g_col)),
            pl.BlockSpec((nk, tf2, seq), lambda i: (0, 0, 0), pipeline_mode=once),
            pl.BlockSpec((nk, seq, tf2), lambda i: (0, 0, 0), pipeline_mode=once),
            pl.BlockSpec((nk, tf2, c), lambda i: (0, 0, order), pipeline_mode=once),
            pl.BlockSpec((1, c), lambda i: (0, 0)),
        ],
        out_specs=pl.BlockSpec((nb, seq, c), lambda i: (i, 0, 0)),
        out_shape=jax.ShapeDtypeStruct((nbatch, seq, c), BF16),
        compiler_params=_params(("arbitrary",)),
        name="hy_longconv",
    )(v_arr, g_arr, w, wi, kf, bias)


def _lane_mask(lo, width, n=LANES):
    lane = lax.broadcasted_iota(jnp.int32, (1, n), 1)
    return (lane >= lo) & (lane < lo + width)


ALIBI_SPLIT = 3
POS_RADIX = 256


def _da_coef(head):
    return 2.0 ** (-8.0 * (head + 1) / DA_HEADS) * math.log2(math.e)


def _alibi_tables(seq):
    pos = jnp.arange(seq, dtype=jnp.int32)
    hi = ((pos // POS_RADIX) * POS_RADIX).astype(F32)[:, None]
    lo = (pos % POS_RADIX).astype(F32)[:, None]
    ones = jnp.ones((seq, 1), F32)
    fq, fk = [], []
    for head in range(DA_HEADS):
        rest = jnp.float32(_da_coef(head))
        pieces = []
        for _ in range(ALIBI_SPLIT):
            piece = rest.astype(BF16).astype(F32)
            pieces.append(piece)
            rest = rest - piece
        cq = jnp.concatenate([ones * c for c in pieces], axis=1)
        fq.append(jnp.concatenate([hi] * ALIBI_SPLIT + [lo] * ALIBI_SPLIT + [cq, cq], axis=1))
        fk.append(jnp.concatenate([-cq, -cq] + [hi] * ALIBI_SPLIT + [lo] * ALIBI_SPLIT, axis=1))
    pad = lambda t: jnp.pad(t, ((0, 0), (0, 0), (0, LANES - 4 * ALIBI_SPLIT))).astype(BF16)
    fk = jnp.stack(fk)
    tq = min(DA_TQ, seq)
    key_tile = (pos // tq)[None, :, None]
    signed = [pad(fk * jnp.where(key_tile < qi, 1.0, jnp.where(key_tile > qi, -1.0, 0.0))) for qi in range(seq // tq)]
    return pad(jnp.stack(fq)), signed


def _da_kernel(q_ref, k_ref, v_ref, fq_ref, fk_ref, lam_ref, g_ref, o_ref, *, qi, lam_init):
    tq = q_ref.shape[1]
    seq = k_ref.shape[1]
    lv = lam_ref[...]
    lam = (jnp.exp(jnp.sum(lv[0:1] * lv[1:2], axis=1, keepdims=True))
           - jnp.exp(jnp.sum(lv[2:3] * lv[3:4], axis=1, keepdims=True)) + lam_init)
    local = jnp.abs(lax.broadcasted_iota(jnp.int32, (tq, tq), 0)
                    - lax.broadcasted_iota(jnp.int32, (tq, tq), 1)).astype(F32)
    slabs = []
    for s in range(DA_WIDTH // LANES):
        cols = slice(s * LANES, (s + 1) * LANES)
        q_slab = q_ref[0, :, cols]
        k_slab = k_ref[0, :, cols]
        v_slab = v_ref[0, :, cols]
        o_slab = jnp.zeros((tq, LANES), F32)
        for hh in range(LANES // HEAD_DIM):
            head = s * (LANES // HEAD_DIM) + hh
            k_aug = jnp.concatenate([k_slab, fk_ref[head]], axis=1)
            mine = _lane_mask(hh * HEAD_DIM, HEAD_DIM)
            v_ones = jnp.where(mine, v_slab, jnp.ones_like(v_slab))
            maps = []
            for c in range(2):
                keep = jnp.where(_lane_mask(hh * HEAD_DIM + c * DA_HALF, DA_HALF), 1.0, 0.0).astype(BF16)
                q_aug = jnp.concatenate([q_slab * keep, fq_ref[head]], axis=1)
                sc = _dot_nt(q_aug, k_aug)
                pieces = [sc[:, qi * tq:(qi + 1) * tq] - _da_coef(head) * local]
                if qi > 0:
                    pieces.insert(0, sc[:, :qi * tq])
                if (qi + 1) * tq < seq:
                    pieces.append(sc[:, (qi + 1) * tq:])
                sc = jnp.concatenate(pieces, axis=1)
                p = jnp.exp2(sc - jnp.max(sc, axis=1, keepdims=True))
                o2 = _dot(p.astype(BF16), v_ones)
                maps.append((o2, o2[:, (1 - hh) * HEAD_DIM:(1 - hh) * HEAD_DIM + 1]))
            diff = maps[0][0] * (1.0 / maps[0][1]) - maps[1][0] * (lam / maps[1][1])
            o_slab = jnp.where(mine, diff, o_slab)
        slabs.append(o_slab)
    o = jnp.concatenate(slabs, axis=1)
    y = o * lax.rsqrt(_head_ssq(o) * (1.0 / HEAD_DIM) + EPS) * g_ref[...]
    o_ref[0] = (y * (1.0 - lam_init)).astype(BF16)


DA_TQ = 512


def _diff_attn(z, fq, fk_signed, lam_vecs, g_tiled, lam_init):
    nb, seq, _ = z.shape
    tq = min(DA_TQ, seq)
    cb = COL_DA // DA_WIDTH
    tiles = []
    for qi in range(seq // tq):
        tiles.append(pl.pallas_call(
            functools.partial(_da_kernel, qi=qi, lam_init=lam_init),
            grid=(nb,),
            in_specs=[
                pl.BlockSpec((1, tq, DA_WIDTH), lambda b, qi=qi: (b, qi, cb)),
                pl.BlockSpec((1, seq, DA_WIDTH), lambda b: (b, 0, cb + 1)),
                pl.BlockSpec((1, seq, DA_WIDTH), lambda b: (b, 0, cb + 2)),
                pl.BlockSpec((DA_HEADS, tq, LANES), lambda b, qi=qi: (0, qi, 0)),
                pl.BlockSpec((DA_HEADS, seq, LANES), lambda b: (0, 0, 0)),
                pl.BlockSpec((4, DA_HALF), lambda b: (0, 0)),
                pl.BlockSpec((1, DA_WIDTH), lambda b: (0, 0)),
            ],
            out_specs=pl.BlockSpec((1, tq, DA_WIDTH), lambda b: (b, 0, 0)),
            out_shape=jax.ShapeDtypeStruct((nb, tq, DA_WIDTH), BF16),
            compiler_params=_params(("arbitrary",)),
            name="diff_attn",
        )(z, z, z, fq, fk_signed[qi], lam_vecs, g_tiled))
    return jnp.concatenate(tiles, axis=1)


def _rope(x, cos, sin):
    w = x.shape[1]
    lane = lax.broadcasted_iota(jnp.int32, (1, w), 1)
    half = ROPE_AXIS // 2
    partner = jnp.where(lane % ROPE_AXIS < half, pltpu.roll(x, w - half, 1), pltpu.roll(x, half, 1))
    return x * cos + partner * sin


def _dup_heads(x):
    first = _lane_mask(0, HEAD_DIM)
    a = jnp.where(first, x, 0.0)
    b = jnp.where(first, 0.0, x)
    return jnp.concatenate([a + pltpu.roll(a, HEAD_DIM, 1), b + pltpu.roll(b, HEAD_DIM, 1)], axis=1)


def _with_ones(v):
    first = _lane_mask(0, HEAD_DIM)
    swapped = pltpu.roll(v, HEAD_DIM, 1)
    return jnp.concatenate([jnp.where(first, v, 1.0), jnp.where(first, 1.0, swapped),
                            jnp.where(first, swapped, 1.0), jnp.where(first, 1.0, v)], axis=1)


def _gqa_kernel(zq_ref, zkv_ref, cosq_ref, sinq_ref, cosk_ref, sink_ref, gq_ref, gk_ref, o_ref, kd_ref, vd_ref):
    tq = zq_ref.shape[1]
    kvw = GQ_KV * HEAD_DIM
    per_slab = LANES // HEAD_DIM
    group = GQ_HEADS // GQ_KV

    @pl.when(pl.program_id(1) == 0)
    def _():
        k = zkv_ref[0, :, :kvw].astype(F32)
        kn = k * lax.rsqrt(_head_ssq(k) * (1.0 / HEAD_DIM) + EPS) * gk_ref[...]
        kd_ref[...] = _dup_heads(_rope(kn, cosk_ref[...], sink_ref[...])).astype(BF16)
        vd_ref[...] = _with_ones(zkv_ref[0, :, kvw:].astype(F32)).astype(BF16)

    q = zq_ref[0].astype(F32)
    qn = q * lax.rsqrt(_head_ssq(q) * (1.0 / HEAD_DIM) + EPS) * gq_ref[...]
    qr = (_rope(qn, cosq_ref[...], sinq_ref[...]) * (HEAD_DIM ** -0.5 * math.log2(math.e))).astype(BF16)
    slabs = []
    for s in range(GQ_WIDTH // LANES):
        q_slab = qr[:, s * LANES:(s + 1) * LANES]
        kv = (s * per_slab) // group
        k_dup = kd_ref[:, kv * LANES:(kv + 1) * LANES]
        o_slab = jnp.zeros((tq, LANES), F32)
        for hh in range(per_slab):
            keep = jnp.where(_lane_mask(hh * HEAD_DIM, HEAD_DIM), 1.0, 0.0).astype(BF16)
            sc = _dot_nt(q_slab * keep, k_dup)
            p = jnp.exp2(sc - jnp.max(sc, axis=1, keepdims=True))
            v_ones = vd_ref[:, (kv * per_slab + hh) * LANES:(kv * per_slab + hh + 1) * LANES]
            o2 = _dot(p.astype(BF16), v_ones)
            sums = o2[:, (1 - hh) * HEAD_DIM:(1 - hh) * HEAD_DIM + 1]
            o_slab = jnp.where(_lane_mask(hh * HEAD_DIM, HEAD_DIM), o2 * (1.0 / sums), o_slab)
        slabs.append(o_slab)
    o_ref[0] = jnp.concatenate(slabs, axis=1).astype(BF16)


def _gqa_attn(z, cos, sin, gq, gk):
    nb, seq, _ = z.shape
    tq = min(512, seq)
    kvw = GQ_KV * HEAD_DIM
    return pl.pallas_call(
        _gqa_kernel,
        grid=(nb, seq // tq),
        in_specs=[
            pl.BlockSpec((1, tq, GQ_WIDTH), lambda b, i: (b, i, COL_GQ // GQ_WIDTH)),
            pl.BlockSpec((1, seq, 2 * kvw), lambda b, i: (b, 0, (COL_GQ + GQ_WIDTH) // (2 * kvw))),
            pl.BlockSpec((tq, GQ_WIDTH), lambda b, i: (i, 0)),
            pl.BlockSpec((tq, GQ_WIDTH), lambda b, i: (i, 0)),
            pl.BlockSpec((seq, kvw), lambda b, i: (0, 0)),
            pl.BlockSpec((seq, kvw), lambda b, i: (0, 0)),
            pl.BlockSpec((1, GQ_WIDTH), lambda b, i: (0, 0)),
            pl.BlockSpec((1, kvw), lambda b, i: (0, 0)),
        ],
        out_specs=pl.BlockSpec((1, tq, GQ_WIDTH), lambda b, i: (b, i, 0)),
        out_shape=jax.ShapeDtypeStruct((nb, seq, GQ_WIDTH), BF16),
        scratch_shapes=[pltpu.VMEM((seq, 2 * kvw), BF16), pltpu.VMEM((seq, 4 * kvw), BF16)],
        compiler_params=_params(("arbitrary", "arbitrary")),
        name="gqa_attn",
    )(z, z, cos, sin, cos, sin, gq, gk)


def _merge_kernel(ya_ref, yb_ref, yc_ref, zg_ref, x_ref, mod_ref, wa_ref, wb_ref, wc_ref, wo_ref,
                  lg_ref, lb_ref, *rest):
    d = D_MODEL
    m = (jnp.tanh(zg_ref[0, :, 0:d].astype(F32)) + 1.0) * _dot(ya_ref[0], wa_ref[...])
    m = m + (jnp.tanh(zg_ref[0, :, d:2 * d].astype(F32)) + 1.0) * _dot(yb_ref[0], wb_ref[...])
    m = m + (jnp.tanh(zg_ref[0, :, 2 * d:3 * d].astype(F32)) + 1.0) * _dot(yc_ref[0], wc_ref[...])
    y = _dot(m.astype(BF16), wo_ref[...])
    gate1 = 1.0 + mod_ref[0, 2:3, :]
    xn = _ln(DEEPNORM_ALPHA * x_ref[0] + gate1 * y) * lg_ref[...] + lb_ref[...]
    h = _ln(xn) * (1.0 + mod_ref[0, 4:5, :]) + mod_ref[0, 3:4, :]
    if len(rest) == 2:
        xo_ref, h_ref = rest
    else:
        wr_ref, br_ref, xo_ref, h_ref, comb_ref = rest
        lane = lax.broadcasted_iota(jnp.int32, (1, LANES), 1).astype(F32)
        neg = -jnp.inf
        logits = jnp.where(lane < N_EXPERTS, _dot3(h, wr_ref[...]) + br_ref[...], neg)
        v1 = jnp.max(logits, axis=1, keepdims=True)
        i1 = jnp.min(jnp.where(logits == v1, lane, float(LANES)), axis=1, keepdims=True)
        others = jnp.where(lane == i1, neg, logits)
        v2 = jnp.max(others, axis=1, keepdims=True)
        i2 = jnp.min(jnp.where(others == v2, lane, float(LANES)), axis=1, keepdims=True)
        e = jnp.exp(v2 - v1)
        w1 = 1.0 / (1.0 + e)
        comb_ref[0] = jnp.where(lane == i1, w1, 0.0) + jnp.where(lane == i2, e * w1, 0.0)
    xo_ref[0] = xn
    h_ref[0] = h.astype(BF16)


def _merge(ya, yb, yc, z, x, mod, wa, wb, wc, wo, lg, lb, router=None):
    nb, seq, d = x.shape
    tm = min(512, seq)
    full = lambda shape: pl.BlockSpec(shape, lambda b, i: (0,) * len(shape))
    row = lambda w, col=0: pl.BlockSpec((1, tm, w), lambda b, i: (b, i, col))
    in_specs = [
        row(HY_WIDTH), row(DA_WIDTH), row(GQ_WIDTH), row(GATE_COLS, COL_GATE // GATE_COLS), row(d),
        pl.BlockSpec((1, 6, d), lambda b, i: (b, 0, 0)),
        full((HY_WIDTH, d)), full((DA_WIDTH, d)), full((GQ_WIDTH, d)), full((d, d)),
        full((1, d)), full((1, d)),
    ]
    out_specs = [row(d), row(d)]
    out_shape = [jax.ShapeDtypeStruct((nb, seq, d), F32), jax.ShapeDtypeStruct((nb, seq, d), BF16)]
    args = [ya, yb, yc, z, x, mod, wa, wb, wc, wo, lg, lb]
    if router is not None:
        in_specs += [full((d, LANES)), full((1, LANES))]
        out_specs.append(row(LANES))
        out_shape.append(jax.ShapeDtypeStruct((nb, seq, LANES), F32))
        args += list(router)
    return pl.pallas_call(
        _merge_kernel,
        grid=(nb, seq // tm),
        in_specs=in_specs,
        out_specs=out_specs,
        out_shape=out_shape,
        compiler_params=_params(("arbitrary", "arbitrary")),
        name="merge_outproj",
    )(*args)


FFN_TF = 256


def _ffn_kernel(h_ref, wup_ref, wd_ref, x_ref, mod_ref, lg_ref, lb_ref, o_ref):
    h = h_ref[0]
    ff = wd_ref.shape[0]
    acc = None
    for c in range(ff // FFN_TF):
        cols = slice(c * FFN_TF, (c + 1) * FFN_TF)
        g = _dot(h, wup_ref[:, cols])
        u = _dot(h, wup_ref[:, ff + c * FFN_TF:ff + (c + 1) * FFN_TF])
        part = _dot((g * _sigmoid(g) * u).astype(BF16), wd_ref[cols, :])
        acc = part if acc is None else acc + part
    gate2 = 1.0 + mod_ref[0, 5:6, :]
    o_ref[0] = _ln(DEEPNORM_ALPHA * x_ref[0] + gate2 * acc) * lg_ref[...] + lb_ref[...]


def _ffn(h, w_up, w_down, x, mod, lg, lb):
    nb, seq, d = x.shape
    ff = w_down.shape[0]
    assert ff % FFN_TF == 0
    tm = min(1024, seq)
    row = lambda w: pl.BlockSpec((1, tm, w), lambda b, i: (b, i, 0))
    vec = pl.BlockSpec((1, d), lambda b, i: (0, 0))
    once = pl.Buffered(1)
    return pl.pallas_call(
        _ffn_kernel,
        grid=(nb, seq // tm),
        in_specs=[
            row(d),
            pl.BlockSpec((d, 2 * ff), lambda b, i: (0, 0), pipeline_mode=once),
            pl.BlockSpec((ff, d), lambda b, i: (0, 0), pipeline_mode=once),
            row(d),
            pl.BlockSpec((1, 6, d), lambda b, i: (b, 0, 0)),
            vec, vec,
        ],
        out_specs=row(d),
        out_shape=jax.ShapeDtypeStruct((nb, seq, d), F32),
        compiler_params=_params(("arbitrary",) * 2),
        name="swiglu_dense",
    )(h, w_up, w_down, x, mod, lg, lb)


MOE_TOKENS = 1024
MOE_SLOT = 384
MOE_CLASSES = (256, 320, MOE_SLOT)


def _moe_kernel(h_ref, comb_ref, wg_ref, wu_ref, wd_ref, x_ref, mod_ref, lg_ref, lb_ref, o_ref,
                posc_ref, posr_ref, cnt_ref, ys_ref):
    e = pl.program_id(2)
    tm = h_ref.shape[1]
    n_exp = ys_ref.shape[0] // MOE_SLOT
    lane = lax.broadcasted_iota(jnp.int32, (1, LANES), 1)

    @pl.when(e == 0)
    def _():
        comb = comb_ref[0]
        sel = jnp.where(comb > 0.0, 1.0, 0.0)
        r = lax.broadcasted_iota(jnp.int32, (tm, tm), 0)
        c = lax.broadcasted_iota(jnp.int32, (tm, tm), 1)
        before = jnp.where(c < r, 1.0, 0.0).astype(BF16)
        rank = _dot(before, sel.astype(BF16))
        posc = jnp.where(comb > 0.0, rank, -1.0)
        posc_ref[...] = posc
        posr_ref[...] = posc.T
        cnt_ref[...] = jnp.sum(sel, axis=0, keepdims=True)
        o_ref[0] = jnp.zeros(o_ref.shape[1:], F32)
        ys_ref[...] = jnp.zeros_like(ys_ref)

    mine = lane == e
    pos_row = posr_ref[pl.ds(e, 1), :]
    count = jnp.sum(jnp.where(mine, cnt_ref[...], 0.0))

    def expert_rows(base, rows):
        slot_r = (base + lax.broadcasted_iota(jnp.int32, (rows, 1), 0)).astype(F32)
        gather = jnp.where(pos_row == slot_r, 1.0, 0.0).astype(BF16)
        xs = _dot(gather, h_ref[0]).astype(BF16)
        g = _dot(xs, wg_ref[...])
        u = _dot(xs, wu_ref[...])
        a = (g * _sigmoid(g) * u).astype(BF16)
        return _dot(a, wd_ref[...]).astype(BF16)

    def fill_slot(rows):
        ys_ref[pl.ds(pl.multiple_of(e * MOE_SLOT, MOE_SLOT), rows), :] = expert_rows(0, rows)

    bounds = (0,) + MOE_CLASSES
    for lo, rows in zip(bounds[:-1], bounds[1:]):
        in_class = (count > lo) & (count <= rows) if rows < MOE_SLOT else count > lo
        pl.when(in_class)(functools.partial(fill_slot, rows))

    for c in range(1, -(-tm // MOE_SLOT)):
        @pl.when(count > c * MOE_SLOT)
        def _():
            pos_col = jnp.sum(jnp.where(mine, posc_ref[...], 0.0), axis=1, keepdims=True)
            w_col = jnp.sum(jnp.where(mine, comb_ref[0], 0.0), axis=1, keepdims=True)
            slot_c = (c * MOE_SLOT + lax.broadcasted_iota(jnp.int32, (1, MOE_SLOT), 1)).astype(F32)
            scatter = jnp.where(pos_col == slot_c, 1.0, 0.0).astype(BF16)
            o_ref[0] += w_col * _dot(scatter, expert_rows(c * MOE_SLOT, MOE_SLOT))

    @pl.when(e == pl.num_programs(2) - 1)
    def _():
        slot_c = lax.broadcasted_iota(jnp.int32, (1, MOE_SLOT), 1).astype(F32)
        f = o_ref[0]
        for e0 in range(0, n_exp, 2):
            weights = jnp.concatenate(
                [jnp.where(posc_ref[:, ee:ee + 1] == slot_c, comb_ref[0, :, ee:ee + 1], 0.0).astype(BF16)
                 for ee in (e0, e0 + 1)], axis=1)
            f = f + _dot(weights, ys_ref[e0 * MOE_SLOT:(e0 + 2) * MOE_SLOT, :])
        gate2 = 1.0 + mod_ref[0, 5:6, :]
        o_ref[0] = _ln(DEEPNORM_ALPHA * x_ref[0] + gate2 * f) * lg_ref[...] + lb_ref[...]


def _moe(h, comb, w_up, w_down, x, mod, lg, lb):
    nb, seq, d = x.shape
    ne, ff, _ = w_down.shape
    tm = min(MOE_TOKENS, seq)
    row = lambda w: pl.BlockSpec((1, tm, w), lambda b, i, e: (b, i, 0))
    vec = pl.BlockSpec((1, d), lambda b, i, e: (0, 0))
    return pl.pallas_call(
        _moe_kernel,
        grid=(nb, seq // tm, ne),
        in_specs=[
            row(d), row(LANES),
            pl.BlockSpec((None, d, ff), lambda b, i, e: (e, 0, 0)),
            pl.BlockSpec((None, d, ff), lambda b, i, e: (e, 0, 1)),
            pl.BlockSpec((None, ff, d), lambda b, i, e: (e, 0, 0)),
            row(d),
            pl.BlockSpec((1, 6, d), lambda b, i, e: (b, 0, 0)),
            vec, vec,
        ],
        out_specs=row(d),
        out_shape=jax.ShapeDtypeStruct((nb, seq, d), F32),
        scratch_shapes=[pltpu.VMEM((tm, LANES), F32), pltpu.VMEM((LANES, tm), F32), pltpu.VMEM((1, LANES), F32),
                        pltpu.VMEM((ne * MOE_SLOT, d), BF16)],
        compiler_params=_params(("arbitrary",) * 3),
        name="swiglu_routed",
    )(h, comb, w_up, w_up, w_down, x, mod, lg, lb)


def _dft_tables(seq):
    n = 2 * seq
    tf = min(CONV_TF, seq)
    nk = seq // tf
    s = jnp.arange(n, dtype=jnp.int32)[None, :]

    def phase(mult):
        ang = ((mult * s) % n).astype(F32) * (2.0 * math.pi / n)
        return jnp.cos(ang), jnp.sin(ang)

    ca, sa = phase(jnp.arange(nk, dtype=jnp.int32)[:, None] * tf)
    cb, sb = phase(jnp.arange(tf, dtype=jnp.int32)[:, None])
    cos = ca[:, None, :] * cb[None] - sa[:, None, :] * sb[None]
    msin = -(sa[:, None, :] * cb[None] + ca[:, None, :] * sb[None])
    k = (jnp.arange(nk, dtype=jnp.int32)[:, None] * tf + jnp.arange(tf, dtype=jnp.int32)[None, :])[:, :, None]
    msin = jnp.where(k == 0, jnp.where(s % 2 == 0, 1.0, -1.0)[None], msin)
    fwd = jnp.concatenate([cos, msin], axis=1)
    scale = jnp.where(k == 0, 1.0 / n, 2.0 / n).astype(F32)
    inv = jnp.concatenate([cos[:, :, :seq] * scale, msin[:, :, :seq] * scale], axis=1).transpose(0, 2, 1)
    return fwd.astype(BF16), inv.astype(BF16)


def _rope_tables(seq):
    t = jnp.arange(seq, dtype=jnp.int32)
    inv = ROPE_THETA ** (-jnp.arange(0, ROPE_AXIS, 2, dtype=F32) / ROPE_AXIS)
    ang_r = (t // GRID_W).astype(F32)[:, None] * inv[None, :]
    ang_c = (t % GRID_W).astype(F32)[:, None] * inv[None, :]
    cos = jnp.concatenate([jnp.cos(ang_r)] * 2 + [jnp.cos(ang_c)] * 2, axis=1)
    sin = jnp.concatenate([-jnp.sin(ang_r), jnp.sin(ang_r), -jnp.sin(ang_c), jnp.sin(ang_c)], axis=1)
    return jnp.tile(cos, (1, GQ_HEADS)), jnp.tile(sin, (1, GQ_HEADS))


def _pad_to(a, rows, cols):
    return jnp.pad(a, ((0, rows - a.shape[0]), (0, cols - a.shape[1])))


def _prepare(seq, p):
    fwd, inv = _dft_tables(seq)
    cos, sin = _rope_tables(seq)
    bands = jnp.linspace(1e-4, N_BANDS - 1, N_BANDS, dtype=F32)
    bands_row = jnp.zeros((1, LANES), F32).at[0, 1:1 + N_BANDS].set(bands).at[0, 1 + N_BANDS:1 + 2 * N_BANDS].set(bands)
    deltas = jnp.abs(jnp.linspace(MIN_DECAY, MAX_DECAY, HY_WIDTH, dtype=F32))
    dec = jnp.tile(deltas, HY_ORDER)[None, :]
    da_scale = DA_HALF ** -0.5 * math.log2(math.e)
    fq, fk = _alibi_tables(seq)
    layers = []
    for l in range(DEPTH):
        w_in, b_in = p['w_in'][l], p['b_in'][l]
        a0, b0, c0, g0 = 0, HY_COLS, HY_COLS + DA_COLS, HY_COLS + DA_COLS + GQ_COLS
        w_cols = [w_in[:, g0:] * 0.5, w_in[:, a0:b0], w_in[:, b0:b0 + DA_WIDTH] * da_scale, w_in[:, b0 + DA_WIDTH:c0], w_in[:, c0:g0]]
        b_cols = [b_in[g0:] * 0.5, b_in[a0:b0], b_in[b0:b0 + DA_WIDTH] * da_scale, b_in[b0 + DA_WIDTH:c0], b_in[c0:g0]]
        ts = _filters_time(
            seq, bands_row,
            _pad_to(p['hy_f_w1'][l], LANES, LANES), _pad_to(p['hy_f_b1'][l][None, :], 1, LANES),
            _pad_to(p['hy_f_w2'][l], LANES, LANES), _pad_to(p['hy_f_b2'][l][None, :], 1, LANES),
            _pad_to(p['hy_f_w3'][l], LANES, 2 * HY_ORDER * HY_WIDTH), p['hy_f_b3'][l][None, :], dec)
        kf = _filter_dft(fwd.reshape(2 * seq, 2 * seq), ts).reshape(fwd.shape[0], -1, HY_ORDER * HY_WIDTH)
        lay = dict(
            w_in=jnp.concatenate(w_cols, axis=1).astype(BF16),
            b_in=jnp.concatenate(b_cols)[None, :],
            conv_w=p['hy_conv_w'][l], conv_b=p['hy_conv_b'][l][None, :],
            kf=kf, hy_bias=p['hy_bias'][l],
            lam=jnp.stack([p['da_lam_q1'][l], p['da_lam_k1'][l], p['da_lam_q2'][l], p['da_lam_k2'][l]]),
            lam_init=0.8 - 0.6 * math.exp(-0.3 * l),
            subln=jnp.tile(p['da_subln_g'][l], DA_HEADS)[None, :],
            gq=jnp.tile(p['gq_qnorm_g'][l], GQ_HEADS)[None, :],
            gk=jnp.tile(p['gq_knorm_g'][l], GQ_KV)[None, :],
            wa=(p['w_br_a'][l] * 0.5).astype(BF16), wb=(p['w_br_b'][l] * 0.5).astype(BF16),
            wc=(p['w_br_c'][l] * 0.5).astype(BF16),
            wo=p['w_out'][l].astype(BF16),
            ln1_g=p['ln1_g'][l][None, :], ln1_b=p['ln1_b'][l][None, :],
            ln2_g=p['ln2_g'][l][None, :], ln2_b=p['ln2_b'][l][None, :],
        )
        if l % 2 == 0:
            lay.update(w_up=p['ffn_w_up'][l // 2].astype(BF16), w_down=p['ffn_w_down'][l // 2].astype(BF16))
        else:
            lay.update(w_up=p['moe_w_up'][l // 2].astype(BF16), w_down=p['moe_w_down'][l // 2].astype(BF16),
                       router=(_pad_to(p['moe_w_router'][l // 2], D_MODEL, LANES),
                               _pad_to(p['moe_b_router'][l // 2][None, :], 1, LANES)))
        layers.append(lay)
    return dict(fwd=fwd[:, :, :seq], inv=inv, cos=cos, sin=sin, fq=fq, fk=fk, layers=layers)


def _trunk(x, c, p, prep):
    mods = _ada(c, p['w_ada'], p['b_ada'])
    nb = x.shape[0]
    for l, lay in enumerate(prep['layers']):
        mod = mods[l].reshape(nb, 6, D_MODEL)
        z = _inproj(x, mod, lay['w_in'], lay['b_in'], lay['conv_w'], lay['conv_b'])
        hy = COL_HY // HY_WIDTH
        u = _longconv(z, hy + 2, z, hy, prep['fwd'], prep['inv'], lay['kf'], 0, lay['hy_bias'][0:1])
        ya = _longconv(u, 0, z, hy + 1, prep['fwd'], prep['inv'], lay['kf'], 1, lay['hy_bias'][1:2])
        yb = _diff_attn(z, prep['fq'], prep['fk'], lay['lam'], lay['subln'], lay['lam_init'])
        yc = _gqa_attn(z, prep['cos'], prep['sin'], lay['gq'], lay['gk'])
        merged = _merge(ya, yb, yc, z, x, mod, lay['wa'], lay['wb'], lay['wc'], lay['wo'],
                        lay['ln1_g'], lay['ln1_b'], lay.get('router'))
        if l % 2 == 0:
            x, h = merged
            x = _ffn(h, lay['w_up'], lay['w_down'], x, mod, lay['ln2_g'], lay['ln2_b'])
        else:
            x, h, comb = merged
            x = _moe(h, comb, lay['w_up'], lay['w_down'], x, mod, lay['ln2_g'], lay['ln2_b'])
    return x


def kernel(x_prompt, x_sample, c_prompt, c_sample, w_ada, b_ada, w_in, b_in, hy_conv_w, hy_conv_b, hy_f_w1, hy_f_b1, hy_f_w2, hy_f_b2, hy_f_w3, hy_f_b3, hy_bias, da_lam_q1, da_lam_k1, da_lam_q2, da_lam_k2, da_subln_g, gq_qnorm_g, gq_knorm_g, w_br_a, w_br_b, w_br_c, w_out, ln1_g, ln1_b, ffn_w_up, ffn_w_down, moe_w_router, moe_b_router, moe_w_up, moe_w_down, ln2_g, ln2_b):
    p = dict(w_ada=w_ada, b_ada=b_ada, w_in=w_in, b_in=b_in,
             hy_conv_w=hy_conv_w, hy_conv_b=hy_conv_b, hy_f_w1=hy_f_w1, hy_f_b1=hy_f_b1,
             hy_f_w2=hy_f_w2, hy_f_b2=hy_f_b2, hy_f_w3=hy_f_w3, hy_f_b3=hy_f_b3, hy_bias=hy_bias,
             da_lam_q1=da_lam_q1, da_lam_k1=da_lam_k1, da_lam_q2=da_lam_q2, da_lam_k2=da_lam_k2,
             da_subln_g=da_subln_g, gq_qnorm_g=gq_qnorm_g, gq_knorm_g=gq_knorm_g,
             w_br_a=w_br_a, w_br_b=w_br_b, w_br_c=w_br_c, w_out=w_out, ln1_g=ln1_g, ln1_b=ln1_b,
             ffn_w_up=ffn_w_up, ffn_w_down=ffn_w_down, moe_w_router=moe_w_router,
             moe_b_router=moe_b_router, moe_w_up=moe_w_up, moe_w_down=moe_w_down,
             ln2_g=ln2_g, ln2_b=ln2_b)
    assert x_prompt.shape[1] == x_sample.shape[1]
    prep = _prepare(x_prompt.shape[1], p)
    return (_trunk(x_prompt, c_prompt, p, prep), _trunk(x_sample, c_sample, p, prep))
```

```python
import functools
import math

import jax
import jax.numpy as jnp
from jax import lax
from jax.experimental import pallas as pl
from jax.experimental.pallas import tpu as pltpu

F32 = jnp.float32
BF16 = jnp.bfloat16

D_MODEL = 1024
DEPTH = 2
GRID_W = 64
HEAD_DIM = 64
HY_WIDTH = D_MODEL // 4
HY_ORDER = 2
SHORT_CONV = 3
N_BANDS = 16
FILTER_HID = 64
DECAY_TARGET = 1e-2
MIN_DECAY = math.log(DECAY_TARGET) / 1.5
MAX_DECAY = math.log(DECAY_TARGET) / 0.3
DECAY_SHIFT = 0.05
DA_HEADS = 4
DA_HALF = HEAD_DIM // 2
DA_WIDTH = DA_HEADS * HEAD_DIM
GQ_HEADS = 8
GQ_KV = 2
GQ_WIDTH = GQ_HEADS * HEAD_DIM
ROPE_AXIS = HEAD_DIM // 2
ROPE_THETA = 10000.0
HY_COLS = (HY_ORDER + 1) * HY_WIDTH
DA_COLS = 3 * DA_WIDTH
GQ_COLS = GQ_WIDTH + 2 * GQ_KV * HEAD_DIM
GATE_COLS = 3 * D_MODEL
IN_COLS = HY_COLS + DA_COLS + GQ_COLS + GATE_COLS
D_FF = 256 * ((8 * D_MODEL // 3 + 255) // 256)
N_EXPERTS = 8
MOE_FF = D_FF // 2
DEEPNORM_ALPHA = (2.0 * DEPTH) ** 0.25
EPS = 1e-5

LANES = 128
VMEM_LIMIT = 56 * 1024 * 1024

COL_GATE = 0
COL_HY = GATE_COLS
COL_DA = COL_HY + HY_COLS
COL_GQ = COL_DA + DA_COLS


def _params(sem):
    return pltpu.CompilerParams(dimension_semantics=sem, vmem_limit_bytes=VMEM_LIMIT)


def _dot(a, b):
    return jnp.dot(a, b, preferred_element_type=F32)


def _dot_nt(a, b):
    return lax.dot_general(a, b, (((1,), (1,)), ((), ())), preferred_element_type=F32)


def _split(x):
    hi = x.astype(BF16)
    lo = (x - hi.astype(F32)).astype(BF16)
    return hi, lo


def _dot3(a, b):
    ah, al = _split(a)
    bh, bl = _split(b)
    return _dot(ah, bh) + (_dot(ah, bl) + _dot(al, bh))


def _sigmoid(x):
    return 0.5 * jnp.tanh(0.5 * x) + 0.5


def _ln(x):
    mu = jnp.mean(x, axis=-1, keepdims=True)
    xc = x - mu
    var = jnp.mean(xc * xc, axis=-1, keepdims=True)
    return xc * lax.rsqrt(var + EPS)


def _head_ssq(x):
    w = x.shape[1]
    r = lax.broadcasted_iota(jnp.int32, (w, w), 0) // HEAD_DIM
    c = lax.broadcasted_iota(jnp.int32, (w, w), 1) // HEAD_DIM
    ones = jnp.where(r == c, 1.0, 0.0).astype(BF16)
    hi, lo = _split(x * x)
    return _dot(hi, ones) + _dot(lo, ones)


def _ada_kernel(c_ref, w_ref, b_ref, o_ref):
    c = c_ref[...]
    o_ref[...] = _dot3(c * _sigmoid(c), w_ref[...]) + b_ref[...]


def _ada(c, w_ada, b_ada):
    nb = c.shape[0]
    tn = 1536
    return pl.pallas_call(
        _ada_kernel,
        grid=(DEPTH, 6 * D_MODEL // tn),
        in_specs=[
            pl.BlockSpec((nb, D_MODEL), lambda l, j: (0, 0)),
            pl.BlockSpec((None, D_MODEL, tn), lambda l, j: (l, 0, j)),
            pl.BlockSpec((None, 1, tn), lambda l, j: (l, 0, j)),
        ],
        out_specs=pl.BlockSpec((None, nb, tn), lambda l, j: (l, 0, j)),
        out_shape=jax.ShapeDtypeStruct((DEPTH, nb, 6 * D_MODEL), F32),
        compiler_params=_params(("arbitrary", "arbitrary")),
        name="ada_mod",
    )(c, w_ada, b_ada.reshape(DEPTH, 1, 6 * D_MODEL))


LN_ROWS = 256
INPROJ_TN = 768
INPROJ_STEPS = IN_COLS // INPROJ_TN
HY_TILE = COL_HY // INPROJ_TN


def _inproj_kernel(x_ref, mod_ref, w_ref, b_ref, cw_ref, cb_ref, o_ref, ha_ref, hb_ref):
    b = pl.program_id(0)
    j = pl.program_id(1)
    seq = x_ref.shape[1]
    chunks = seq // LN_ROWS
    assert chunks <= INPROJ_STEPS + 1

    @pl.when((b == 0) & (j == 0))
    def _():
        hb_ref[...] = jnp.zeros_like(hb_ref)

    def step(h_ln, h_mm, ln_chunks, conv, col):
        shift = mod_ref[0, 0:1, :]
        scale = 1.0 + mod_ref[0, 1:2, :]
        for c in ln_chunks:
            r = pl.ds(pl.multiple_of(c * LN_ROWS, LN_ROWS), LN_ROWS)
            h_ln[r, :] = (_ln(x_ref[0, r, :]) * scale + shift).astype(BF16)
        cols = pl.ds(pl.multiple_of(col * INPROJ_TN, INPROJ_TN), INPROJ_TN)
        z = _dot(h_mm[...], w_ref[:, cols]) + b_ref[:, cols]
        if conv:
            row = lax.broadcasted_iota(jnp.int32, z.shape, 0)
            prev = jnp.where(row == 0, 0.0, pltpu.roll(z, 1, 0))
            nxt = jnp.where(row == seq - 1, 0.0, pltpu.roll(z, seq - 1, 0))
            z = cb_ref[...] + prev * cw_ref[0:1, :] + z * cw_ref[1:2, :] + nxt * cw_ref[2:3, :]
        o_ref[0] = z.astype(BF16)

    first = [0] + ([INPROJ_STEPS] if chunks > INPROJ_STEPS else [])
    later = [jnp.minimum(j, chunks - 1)]
    for parity, (h_ln, h_mm) in enumerate(((ha_ref, hb_ref), (hb_ref, ha_ref))):
        even = b % 2 == parity
        pl.when(even & (j == 0))(functools.partial(step, h_ln, h_mm, first, False, 0))
        pl.when(even & (j == HY_TILE))(functools.partial(step, h_ln, h_mm, later, True, HY_TILE))
        pl.when(even & (j != 0) & (j != HY_TILE))(functools.partial(step, h_ln, h_mm, later, False, j))


def _inproj(x, mod, w, b, conv_w, conv_b):
    nb, seq, _ = x.shape
    tn = INPROJ_TN
    assert HY_COLS == tn and COL_HY % tn == 0 and HY_TILE != 0 and seq % LN_ROWS == 0
    cur = lambda i, j: (jnp.minimum(i, nb - 1), 0, 0)
    return pl.pallas_call(
        _inproj_kernel,
        grid=(nb + 1, INPROJ_STEPS),
        in_specs=[
            pl.BlockSpec((1, seq, D_MODEL), cur),
            pl.BlockSpec((1, 6, D_MODEL), cur),
            pl.BlockSpec((D_MODEL, IN_COLS), lambda i, j: (0, 0), pipeline_mode=pl.Buffered(1)),
            pl.BlockSpec((1, IN_COLS), lambda i, j: (0, 0)),
            pl.BlockSpec((SHORT_CONV, tn), lambda i, j: (0, 0)),
            pl.BlockSpec((1, tn), lambda i, j: (0, 0)),
        ],
        out_specs=pl.BlockSpec((1, seq, tn), lambda i, j: (jnp.maximum(i - 1, 0), 0, jnp.where(i == 0, 0, j))),
        out_shape=jax.ShapeDtypeStruct((nb, seq, IN_COLS), BF16),
        scratch_shapes=[pltpu.VMEM((seq, D_MODEL), BF16), pltpu.VMEM((seq, D_MODEL), BF16)],
        compiler_params=_params(("arbitrary", "arbitrary")),
        name="ln_inproj",
    )(x, mod, w, b, conv_w, conv_b)


FILT_ROWS = 512


def _filter_kernel(bands_ref, w1_ref, b1_ref, w2_ref, b2_ref, w3_ref, b3_ref, dec_ref, o_ref, ts_ref, *, seq):
    n = 2 * seq
    rows = min(FILT_ROWS, n)
    hw = HY_ORDER * HY_WIDTH
    lane = lax.broadcasted_iota(jnp.int32, (1, LANES), 1)

    def fill(i, asum):
        j0 = pl.multiple_of(i * rows, rows)
        j = j0 + lax.broadcasted_iota(jnp.int32, (rows, 1), 0)
        t = jnp.where(j < seq, j, n - j).astype(F32)
        t_norm = t / max(seq - 1, 1)
        ang = (2.0 * math.pi / seq) * t * bands_ref[...]
        feats = jnp.where(lane == 0, t_norm,
                          jnp.where(lane <= N_BANDS, jnp.cos(ang),
                                    jnp.where(lane <= 2 * N_BANDS, -jnp.sin(ang), 0.0)))
        h = jnp.sin(_dot3(feats, w1_ref[...]) + b1_ref[...])
        h = jnp.sin(_dot3(h, w2_ref[...]) + b2_ref[...])
        h = _dot3(h, w3_ref[...]) + b3_ref[...]
        window = jnp.exp(-t_norm * dec_ref[...]) + DECAY_SHIFT
        sel = jnp.where(j < seq, h[:, :hw], h[:, hw:]) * window
        sel = jnp.where(j == seq, 0.0, sel)
        ts_ref[pl.ds(j0, rows), :] = sel
        return asum + jnp.sum(jnp.abs(sel), axis=0, keepdims=True)

    asum = lax.fori_loop(0, n // rows, fill, jnp.zeros((1, hw), F32))
    inv = 1.0 / (asum + EPS)

    def norm(i, carry):
        r = pl.ds(pl.multiple_of(i * rows, rows), rows)
        o_ref[r, :] = (ts_ref[r, :] * inv).astype(BF16)
        return carry

    lax.fori_loop(0, n // rows, norm, 0)


def _filters_time(seq, bands, w1, b1, w2, b2, w3, b3, dec):
    hw = HY_ORDER * HY_WIDTH
    return pl.pallas_call(
        functools.partial(_filter_kernel, seq=seq),
        out_shape=jax.ShapeDtypeStruct((2 * seq, hw), BF16),
        scratch_shapes=[pltpu.VMEM((2 * seq, hw), F32)],
        compiler_params=pltpu.CompilerParams(vmem_limit_bytes=VMEM_LIMIT),
        name="hy_filter_time",
    )(bands, w1, b1, w2, b2, w3, b3, dec)


def _matmul_kernel(a_ref, b_ref, o_ref):
    o_ref[...] = _dot(a_ref[...], b_ref[...])


def _filter_dft(wf, ts):
    n, hw = ts.shape
    tm = min(512, n)
    return pl.pallas_call(
        _matmul_kernel,
        grid=(n // tm,),
        in_specs=[pl.BlockSpec((tm, n), lambda i: (i, 0)), pl.BlockSpec((n, hw), lambda i: (0, 0))],
        out_specs=pl.BlockSpec((tm, hw), lambda i: (i, 0)),
        out_shape=jax.ShapeDtypeStruct((n, hw), F32),
        compiler_params=_params(("arbitrary",)),
        name="hy_filter_dft",
    )(wf, ts)


def _longconv_kernel(v_ref, g_ref, w_ref, wi_ref, kf_ref, bias_ref, o_ref):
    nb = v_ref.shape[0]
    nk = w_ref.shape[0]
    tf = w_ref.shape[1] // 2
    first = lax.broadcasted_iota(jnp.int32, (tf, 1), 0) == 0
    for n in range(nb):
        v = v_ref[n]
        acc = None
        for kt in range(nk):
            u = _dot(w_ref[kt], v)
            ur, ui = u[:tf], u[tf:]
            kr = kf_ref[kt, :tf, :]
            ki = kf_ref[kt, tf:, :]
            if kt == 0:
                gr = ur * kr - jnp.where(first, 0.0, ui * ki)
                gi = jnp.where(first, ui * ki, ur * ki + ui * kr)
            else:
                gr = ur * kr - ui * ki
                gi = ur * ki + ui * kr
            part = _dot(wi_ref[kt], jnp.concatenate([gr, gi], axis=0).astype(BF16))
            acc = part if acc is None else acc + part
        y = (acc + v.astype(F32) * bias_ref[...]).astype(BF16)
        o_ref[n] = (g_ref[n] * y).astype(BF16)


CONV_TF = 256
CONV_NB = 2


def _longconv(v_arr, v_col, g_arr, g_col, w, wi, kf, order, bias):
    nbatch, seq, _ = v_arr.shape
    nb = math.gcd(nbatch, CONV_NB)
    nk, tf2, _ = w.shape
    c = HY_WIDTH
    once = pl.Buffered(1)
    return pl.pallas_call(
        _longconv_kernel,
        grid=(nbatch // nb,),
        in_specs=[
            pl.BlockSpec((nb, seq, c), lambda i: (i, 0, v_col)),
            pl.BlockSpec((nb, seq, c), lambda i: (i, 0, g_col)),
            pl.BlockSpec((nk, tf2, seq), lambda i: (0, 0, 0), pipeline_mode=once),
            pl.BlockSpec((nk, seq, tf2), lambda i: (0, 0, 0), pipeline_mode=once),
            pl.BlockSpec((nk, tf2, c), lambda i: (0, 0, order), pipeline_mode=once),
            pl.BlockSpec((1, c), lambda i: (0, 0)),
        ],
        out_specs=pl.BlockSpec((nb, seq, c), lambda i: (i, 0, 0)),
        out_shape=jax.ShapeDtypeStruct((nbatch, seq, c), BF16),
        compiler_params=_params(("arbitrary",)),
        name="hy_longconv",
    )(v_arr, g_arr, w, wi, kf, bias)


def _lane_mask(lo, width, n=LANES):
    lane = lax.broadcasted_iota(jnp.int32, (1, n), 1)
    return (lane >= lo) & (lane < lo + width)


ALIBI_SPLIT = 3
POS_RADIX = 256


def _da_coef(head):
    return 2.0 ** (-8.0 * (head + 1) / DA_HEADS) * math.log2(math.e)


def _alibi_tables(seq):
    pos = jnp.arange(seq, dtype=jnp.int32)
    hi = ((pos // POS_RADIX) * POS_RADIX).astype(F32)[:, None]
    lo = (pos % POS_RADIX).astype(F32)[:, None]
    ones = jnp.ones((seq, 1), F32)
    fq, fk = [], []
    for head in range(DA_HEADS):
        rest = jnp.float32(_da_coef(head))
        pieces = []
        for _ in range(ALIBI_SPLIT):
            piece = rest.astype(BF16).astype(F32)
            pieces.append(piece)
            rest = rest - piece
        cq = jnp.concatenate([ones * c for c in pieces], axis=1)
        fq.append(jnp.concatenate([hi] * ALIBI_SPLIT + [lo] * ALIBI_SPLIT + [cq, cq], axis=1))
        fk.append(jnp.concatenate([-cq, -cq] + [hi] * ALIBI_SPLIT + [lo] * ALIBI_SPLIT, axis=1))
    pad = lambda t: jnp.pad(t, ((0, 0), (0, 0), (0, LANES - 4 * ALIBI_SPLIT))).astype(BF16)
    fk = jnp.stack(fk)
    tq = min(DA_TQ, seq)
    key_tile = (pos // tq)[None, :, None]
    signed = [pad(fk * jnp.where(key_tile < qi, 1.0, jnp.where(key_tile > qi, -1.0, 0.0))) for qi in range(seq // tq)]
    return pad(jnp.stack(fq)), signed


def _da_kernel(q_ref, k_ref, v_ref, fq_ref, fk_ref, lam_ref, g_ref, o_ref, *, qi, lam_init):
    tq = q_ref.shape[1]
    seq = k_ref.shape[1]
    lv = lam_ref[...]
    lam = (jnp.exp(jnp.sum(lv[0:1] * lv[1:2], axis=1, keepdims=True))
           - jnp.exp(jnp.sum(lv[2:3] * lv[3:4], axis=1, keepdims=True)) + lam_init)
    local = jnp.abs(lax.broadcasted_iota(jnp.int32, (tq, tq), 0)
                    - lax.broadcasted_iota(jnp.int32, (tq, tq), 1)).astype(F32)
    slabs = []
    for s in range(DA_WIDTH // LANES):
        cols = slice(s * LANES, (s + 1) * LANES)
        q_slab = q_ref[0, :, cols]
        k_slab = k_ref[0, :, cols]
        v_slab = v_ref[0, :, cols]
        o_slab = jnp.zeros((tq, LANES), F32)
        for hh in range(LANES // HEAD_DIM):
            head = s * (LANES // HEAD_DIM) + hh
            k_aug = jnp.concatenate([k_slab, fk_ref[head]], axis=1)
            mine = _lane_mask(hh * HEAD_DIM, HEAD_DIM)
            v_ones = jnp.where(mine, v_slab, jnp.ones_like(v_slab))
            maps = []
            for c in range(2):
                keep = jnp.where(_lane_mask(hh * HEAD_DIM + c * DA_HALF, DA_HALF), 1.0, 0.0).astype(BF16)
                q_aug = jnp.concatenate([q_slab * keep, fq_ref[head]], axis=1)
                sc = _dot_nt(q_aug, k_aug)
                pieces = [sc[:, qi * tq:(qi + 1) * tq] - _da_coef(head) * local]
                if qi > 0:
                    pieces.insert(0, sc[:, :qi * tq])
                if (qi + 1) * tq < seq:
                    pieces.append(sc[:, (qi + 1) * tq:])
                sc = jnp.concatenate(pieces, axis=1)
                p = jnp.exp2(sc - jnp.max(sc, axis=1, keepdims=True))
                o2 = _dot(p.astype(BF16), v_ones)
                maps.append((o2, o2[:, (1 - hh) * HEAD_DIM:(1 - hh) * HEAD_DIM + 1]))
            diff = maps[0][0] * (1.0 / maps[0][1]) - maps[1][0] * (lam / maps[1][1])
            o_slab = jnp.where(mine, diff, o_slab)
        slabs.append(o_slab)
    o = jnp.concatenate(slabs, axis=1)
    y = o * lax.rsqrt(_head_ssq(o) * (1.0 / HEAD_DIM) + EPS) * g_ref[...]
    o_ref[0] = (y * (1.0 - lam_init)).astype(BF16)


DA_TQ = 512


def _diff_attn(z, fq, fk_signed, lam_vecs, g_tiled, lam_init):
    nb, seq, _ = z.shape
    tq = min(DA_TQ, seq)
    cb = COL_DA // DA_WIDTH
    tiles = []
    for qi in range(seq // tq):
        tiles.append(pl.pallas_call(
            functools.partial(_da_kernel, qi=qi, lam_init=lam_init),
            grid=(nb,),
            in_specs=[
                pl.BlockSpec((1, tq, DA_WIDTH), lambda b, qi=qi: (b, qi, cb)),
                pl.BlockSpec((1, seq, DA_WIDTH), lambda b: (b, 0, cb + 1)),
                pl.BlockSpec((1, seq, DA_WIDTH), lambda b: (b, 0, cb + 2)),
                pl.BlockSpec((DA_HEADS, tq, LANES), lambda b, qi=qi: (0, qi, 0)),
                pl.BlockSpec((DA_HEADS, seq, LANES), lambda b: (0, 0, 0)),
                pl.BlockSpec((4, DA_HALF), lambda b: (0, 0)),
                pl.BlockSpec((1, DA_WIDTH), lambda b: (0, 0)),
            ],
            out_specs=pl.BlockSpec((1, tq, DA_WIDTH), lambda b: (b, 0, 0)),
            out_shape=jax.ShapeDtypeStruct((nb, tq, DA_WIDTH), BF16),
            compiler_params=_params(("arbitrary",)),
            name="diff_attn",
        )(z, z, z, fq, fk_signed[qi], lam_vecs, g_tiled))
    return jnp.concatenate(tiles, axis=1)


def _rope(x, cos, sin):
    w = x.shape[1]
    lane = lax.broadcasted_iota(jnp.int32, (1, w), 1)
    half = ROPE_AXIS // 2
    partner = jnp.where(lane % ROPE_AXIS < half, pltpu.roll(x, w - half, 1), pltpu.roll(x, half, 1))
    return x * cos + partner * sin


def _dup_heads(x):
    first = _lane_mask(0, HEAD_DIM)
    a = jnp.where(first, x, 0.0)
    b = jnp.where(first, 0.0, x)
    return jnp.concatenate([a + pltpu.roll(a, HEAD_DIM, 1), b + pltpu.roll(b, HEAD_DIM, 1)], axis=1)


def _with_ones(v):
    first = _lane_mask(0, HEAD_DIM)
    swapped = pltpu.roll(v, HEAD_DIM, 1)
    return jnp.concatenate([jnp.where(first, v, 1.0), jnp.where(first, 1.0, swapped),
                            jnp.where(first, swapped, 1.0), jnp.where(first, 1.0, v)], axis=1)


def _gqa_kernel(zq_ref, zkv_ref, cosq_ref, sinq_ref, cosk_ref, sink_ref, gq_ref, gk_ref, o_ref, kd_ref, vd_ref):
    tq = zq_ref.shape[1]
    kvw = GQ_KV * HEAD_DIM
    per_slab = LANES // HEAD_DIM
    group = GQ_HEADS // GQ_KV

    @pl.when(pl.program_id(1) == 0)
    def _():
        k = zkv_ref[0, :, :kvw].astype(F32)
        kn = k * lax.rsqrt(_head_ssq(k) * (1.0 / HEAD_DIM) + EPS) * gk_ref[...]
        kd_ref[...] = _dup_heads(_rope(kn, cosk_ref[...], sink_ref[...])).astype(BF16)
        vd_ref[...] = _with_ones(zkv_ref[0, :, kvw:].astype(F32)).astype(BF16)

    q = zq_ref[0].astype(F32)
    qn = q * lax.rsqrt(_head_ssq(q) * (1.0 / HEAD_DIM) + EPS) * gq_ref[...]
    qr = (_rope(qn, cosq_ref[...], sinq_ref[...]) * (HEAD_DIM ** -0.5 * math.log2(math.e))).astype(BF16)
    slabs = []
    for s in range(GQ_WIDTH // LANES):
        q_slab = qr[:, s * LANES:(s + 1) * LANES]
        kv = (s * per_slab) // group
        k_dup = kd_ref[:, kv * LANES:(kv + 1) * LANES]
        o_slab = jnp.zeros((tq, LANES), F32)
        for hh in range(per_slab):
            keep = jnp.where(_lane_mask(hh * HEAD_DIM, HEAD_DIM), 1.0, 0.0).astype(BF16)
            sc = _dot_nt(q_slab * keep, k_dup)
            p = jnp.exp2(sc - jnp.max(sc, axis=1, keepdims=True))
            v_ones = vd_ref[:, (kv * per_slab + hh) * LANES:(kv * per_slab + hh + 1) * LANES]
            o2 = _dot(p.astype(BF16), v_ones)
            sums = o2[:, (1 - hh) * HEAD_DIM:(1 - hh) * HEAD_DIM + 1]
            o_slab = jnp.where(_lane_mask(hh * HEAD_DIM, HEAD_DIM), o2 * (1.0 / sums), o_slab)
        slabs.append(o_slab)
    o_ref[0] = jnp.concatenate(slabs, axis=1).astype(BF16)


def _gqa_attn(z, cos, sin, gq, gk):
    nb, seq, _ = z.shape
    tq = min(512, seq)
    kvw = GQ_KV * HEAD_DIM
    return pl.pallas_call(
        _gqa_kernel,
        grid=(nb, seq // tq),
        in_specs=[
            pl.BlockSpec((1, tq, GQ_WIDTH), lambda b, i: (b, i, COL_GQ // GQ_WIDTH)),
            pl.BlockSpec((1, seq, 2 * kvw), lambda b, i: (b, 0, (COL_GQ + GQ_WIDTH) // (2 * kvw))),
            pl.BlockSpec((tq, GQ_WIDTH), lambda b, i: (i, 0)),
            pl.BlockSpec((tq, GQ_WIDTH), lambda b, i: (i, 0)),
            pl.BlockSpec((seq, kvw), lambda b, i: (0, 0)),
            pl.BlockSpec((seq, kvw), lambda b, i: (0, 0)),
            pl.BlockSpec((1, GQ_WIDTH), lambda b, i: (0, 0)),
            pl.BlockSpec((1, kvw), lambda b, i: (0, 0)),
        ],
        out_specs=pl.BlockSpec((1, tq, GQ_WIDTH), lambda b, i: (b, i, 0)),
        out_shape=jax.ShapeDtypeStruct((nb, seq, GQ_WIDTH), BF16),
        scratch_shapes=[pltpu.VMEM((seq, 2 * kvw), BF16), pltpu.VMEM((seq, 4 * kvw), BF16)],
        compiler_params=_params(("arbitrary", "arbitrary")),
        name="gqa_attn",
    )(z, z, cos, sin, cos, sin, gq, gk)


def _merge_kernel(ya_ref, yb_ref, yc_ref, zg_ref, x_ref, mod_ref, wa_ref, wb_ref, wc_ref, wo_ref,
                  lg_ref, lb_ref, *rest):
    d = D_MODEL
    m = (jnp.tanh(zg_ref[0, :, 0:d].astype(F32)) + 1.0) * _dot(ya_ref[0], wa_ref[...])
    m = m + (jnp.tanh(zg_ref[0, :, d:2 * d].astype(F32)) + 1.0) * _dot(yb_ref[0], wb_ref[...])
    m = m + (jnp.tanh(zg_ref[0, :, 2 * d:3 * d].astype(F32)) + 1.0) * _dot(yc_ref[0], wc_ref[...])
    y = _dot(m.astype(BF16), wo_ref[...])
    gate1 = 1.0 + mod_ref[0, 2:3, :]
    xn = _ln(DEEPNORM_ALPHA * x_ref[0] + gate1 * y) * lg_ref[...] + lb_ref[...]
    h = _ln(xn) * (1.0 + mod_ref[0, 4:5, :]) + mod_ref[0, 3:4, :]
    if len(rest) == 2:
        xo_ref, h_ref = rest
    else:
        wr_ref, br_ref, xo_ref, h_ref, comb_ref = rest
        lane = lax.broadcasted_iota(jnp.int32, (1, LANES), 1).astype(F32)
        neg = -jnp.inf
        logits = jnp.where(lane < N_EXPERTS, _dot3(h, wr_ref[...]) + br_ref[...], neg)
        v1 = jnp.max(logits, axis=1, keepdims=True)
        i1 = jnp.min(jnp.where(logits == v1, lane, float(LANES)), axis=1, keepdims=True)
        others = jnp.where(lane == i1, neg, logits)
        v2 = jnp.max(others, axis=1, keepdims=True)
        i2 = jnp.min(jnp.where(others == v2, lane, float(LANES)), axis=1, keepdims=True)
        e = jnp.exp(v2 - v1)
        w1 = 1.0 / (1.0 + e)
        comb_ref[0] = jnp.where(lane == i1, w1, 0.0) + jnp.where(lane == i2, e * w1, 0.0)
    xo_ref[0] = xn
    h_ref[0] = h.astype(BF16)


def _merge(ya, yb, yc, z, x, mod, wa, wb, wc, wo, lg, lb, router=None):
    nb, seq, d = x.shape
    tm = min(512, seq)
    full = lambda shape: pl.BlockSpec(shape, lambda b, i: (0,) * len(shape))
    row = lambda w, col=0: pl.BlockSpec((1, tm, w), lambda b, i: (b, i, col))
    in_specs = [
        row(HY_WIDTH), row(DA_WIDTH), row(GQ_WIDTH), row(GATE_COLS, COL_GATE // GATE_COLS), row(d),
        pl.BlockSpec((1, 6, d), lambda b, i: (b, 0, 0)),
        full((HY_WIDTH, d)), full((DA_WIDTH, d)), full((GQ_WIDTH, d)), full((d, d)),
        full((1, d)), full((1, d)),
    ]
    out_specs = [row(d), row(d)]
    out_shape = [jax.ShapeDtypeStruct((nb, seq, d), F32), jax.ShapeDtypeStruct((nb, seq, d), BF16)]
    args = [ya, yb, yc, z, x, mod, wa, wb, wc, wo, lg, lb]
    if router is not None:
        in_specs += [full((d, LANES)), full((1, LANES))]
        out_specs.append(row(LANES))
        out_shape.append(jax.ShapeDtypeStruct((nb, seq, LANES), F32))
        args += list(router)
    return pl.pallas_call(
        _merge_kernel,
        grid=(nb, seq // tm),
        in_specs=in_specs,
        out_specs=out_specs,
        out_shape=out_shape,
        compiler_params=_params(("arbitrary", "arbitrary")),
        name="merge_outproj",
    )(*args)


FFN_TF = 256


def _ffn_kernel(h_ref, wup_ref, wd_ref, x_ref, mod_ref, lg_ref, lb_ref, o_ref):
    h = h_ref[0]
    ff = wd_ref.shape[0]
    acc = None
    for c in range(ff // FFN_TF):
        cols = slice(c * FFN_TF, (c + 1) * FFN_TF)
        g = _dot(h, wup_ref[:, cols])
        u = _dot(h, wup_ref[:, ff + c * FFN_TF:ff + (c + 1) * FFN_TF])
        part = _dot((g * _sigmoid(g) * u).astype(BF16), wd_ref[cols, :])
        acc = part if acc is None else acc + part
    gate2 = 1.0 + mod_ref[0, 5:6, :]
    o_ref[0] = _ln(DEEPNORM_ALPHA * x_ref[0] + gate2 * acc) * lg_ref[...] + lb_ref[...]


def _ffn(h, w_up, w_down, x, mod, lg, lb):
    nb, seq, d = x.shape
    ff = w_down.shape[0]
    assert ff % FFN_TF == 0
    tm = min(1024, seq)
    row = lambda w: pl.BlockSpec((1, tm, w), lambda b, i: (b, i, 0))
    vec = pl.BlockSpec((1, d), lambda b, i: (0, 0))
    once = pl.Buffered(1)
    return pl.pallas_call(
        _ffn_kernel,
        grid=(nb, seq // tm),
        in_specs=[
            row(d),
            pl.BlockSpec((d, 2 * ff), lambda b, i: (0, 0), pipeline_mode=once),
            pl.BlockSpec((ff, d), lambda b, i: (0, 0), pipeline_mode=once),
            row(d),
            pl.BlockSpec((1, 6, d), lambda b, i: (b, 0, 0)),
            vec, vec,
        ],
        out_specs=row(d),
        out_shape=jax.ShapeDtypeStruct((nb, seq, d), F32),
        compiler_params=_params(("arbitrary",) * 2),
        name="swiglu_dense",
    )(h, w_up, w_down, x, mod, lg, lb)


MOE_TOKENS = 1024
MOE_SLOT = 384
MOE_CLASSES = (256, 320, MOE_SLOT)
MOE_EXTRA = (128, 256, MOE_SLOT)


def _moe_kernel(h_ref, comb_ref, wg_ref, wu_ref, wd_ref, x_ref, mod_ref, lg_ref, lb_ref, o_ref,
                posc_ref, posr_ref, cnt_ref, ys_ref):
    e = pl.program_id(2)
    tm = h_ref.shape[1]
    n_exp = ys_ref.shape[0] // MOE_SLOT
    lane = lax.broadcasted_iota(jnp.int32, (1, LANES), 1)

    @pl.when(e == 0)
    def _():
        comb = comb_ref[0]
        sel = jnp.where(comb > 0.0, 1.0, 0.0)
        r = lax.broadcasted_iota(jnp.int32, (tm, tm), 0)
        c = lax.broadcasted_iota(jnp.int32, (tm, tm), 1)
        before = jnp.where(c < r, 1.0, 0.0).astype(BF16)
        rank = _dot(before, sel.astype(BF16))
        posc = jnp.where(comb > 0.0, rank, -1.0)
        posc_ref[...] = posc
        posr_ref[...] = posc.T
        cnt_ref[...] = jnp.sum(sel, axis=0, keepdims=True)
        o_ref[0] = jnp.zeros(o_ref.shape[1:], F32)
        ys_ref[...] = jnp.zeros_like(ys_ref)

    mine = lane == e
    pos_row = posr_ref[pl.ds(e, 1), :]
    count = jnp.sum(jnp.where(mine, cnt_ref[...], 0.0))

    def expert_rows(base, rows):
        slot_r = (base + lax.broadcasted_iota(jnp.int32, (rows, 1), 0)).astype(F32)
        gather = jnp.where(pos_row == slot_r, 1.0, 0.0).astype(BF16)
        xs = _dot(gather, h_ref[0]).astype(BF16)
        g = _dot(xs, wg_ref[...])
        u = _dot(xs, wu_ref[...])
        a = (g * _sigmoid(g) * u).astype(BF16)
        return _dot(a, wd_ref[...]).astype(BF16)

    def fill_slot(rows):
        ys_ref[pl.ds(pl.multiple_of(e * MOE_SLOT, MOE_SLOT), rows), :] = expert_rows(0, rows)

    bounds = (0,) + MOE_CLASSES
    for lo, rows in zip(bounds[:-1], bounds[1:]):
        in_class = (count > lo) & (count <= rows) if rows < MOE_SLOT else count > lo
        pl.when(in_class)(functools.partial(fill_slot, rows))

    def scatter_rows(base, rows):
        pos_col = jnp.sum(jnp.where(mine, posc_ref[...], 0.0), axis=1, keepdims=True)
        w_col = jnp.sum(jnp.where(mine, comb_ref[0], 0.0), axis=1, keepdims=True)
        slot_c = (base + lax.broadcasted_iota(jnp.int32, (1, rows), 1)).astype(F32)
        scatter = jnp.where(pos_col == slot_c, 1.0, 0.0).astype(BF16)
        o_ref[0] += w_col * _dot(scatter, expert_rows(base, rows))

    extra = (0,) + MOE_EXTRA
    for lo, rows in zip(extra[:-1], extra[1:]):
        over = count - MOE_SLOT
        in_class = (over > lo) & (over <= rows) if rows < MOE_SLOT else over > lo
        pl.when(in_class)(functools.partial(scatter_rows, MOE_SLOT, rows))
    for c in range(2, -(-tm // MOE_SLOT)):
        pl.when(count > c * MOE_SLOT)(functools.partial(scatter_rows, c * MOE_SLOT, MOE_SLOT))

    @pl.when(e == pl.num_programs(2) - 1)
    def _():
        slot_c = lax.broadcasted_iota(jnp.int32, (1, MOE_SLOT), 1).astype(F32)
        f = o_ref[0]
        for e0 in range(0, n_exp, 2):
            weights = jnp.concatenate(
                [jnp.where(posc_ref[:, ee:ee + 1] == slot_c, comb_ref[0, :, ee:ee + 1], 0.0).astype(BF16)
                 for ee in (e0, e0 + 1)], axis=1)
            f = f + _dot(weights, ys_ref[e0 * MOE_SLOT:(e0 + 2) * MOE_SLOT, :])
        gate2 = 1.0 + mod_ref[0, 5:6, :]
        o_ref[0] = _ln(DEEPNORM_ALPHA * x_ref[0] + gate2 * f) * lg_ref[...] + lb_ref[...]


def _moe(h, comb, w_up, w_down, x, mod, lg, lb):
    nb, seq, d = x.shape
    ne, ff, _ = w_down.shape
    tm = min(MOE_TOKENS, seq)
    row = lambda w: pl.BlockSpec((1, tm, w), lambda b, i, e: (b, i, 0))
    vec = pl.BlockSpec((1, d), lambda b, i, e: (0, 0))
    return pl.pallas_call(
        _moe_kernel,
        grid=(nb, seq // tm, ne),
        in_specs=[
            row(d), row(LANES),
            pl.BlockSpec((None, d, ff), lambda b, i, e: (e, 0, 0)),
            pl.BlockSpec((None, d, ff), lambda b, i, e: (e, 0, 1)),
            pl.BlockSpec((None, ff, d), lambda b, i, e: (e, 0, 0)),
            row(d),
            pl.BlockSpec((1, 6, d), lambda b, i, e: (b, 0, 0)),
            vec, vec,
        ],
        out_specs=row(d),
        out_shape=jax.ShapeDtypeStruct((nb, seq, d), F32),
        scratch_shapes=[pltpu.VMEM((tm, LANES), F32), pltpu.VMEM((LANES, tm), F32), pltpu.VMEM((1, LANES), F32),
                        pltpu.VMEM((ne * MOE_SLOT, d), BF16)],
        compiler_params=_params(("arbitrary",) * 3),
        name="swiglu_routed",
    )(h, comb, w_up, w_up, w_down, x, mod, lg, lb)


def _dft_tables(seq):
    n = 2 * seq
    tf = min(CONV_TF, seq)
    nk = seq // tf
    s = jnp.arange(n, dtype=jnp.int32)[None, :]

    def phase(mult):
        ang = ((mult * s) % n).astype(F32) * (2.0 * math.pi / n)
        return jnp.cos(ang), jnp.sin(ang)

    ca, sa = phase(jnp.arange(nk, dtype=jnp.int32)[:, None] * tf)
    cb, sb = phase(jnp.arange(tf, dtype=jnp.int32)[:, None])
    cos = ca[:, None, :] * cb[None] - sa[:, None, :] * sb[None]
    msin = -(sa[:, None, :] * cb[None] + ca[:, None, :] * sb[None])
    k = (jnp.arange(nk, dtype=jnp.int32)[:, None] * tf + jnp.arange(tf, dtype=jnp.int32)[None, :])[:, :, None]
    msin = jnp.where(k == 0, jnp.where(s % 2 == 0, 1.0, -1.0)[None], msin)
    fwd = jnp.concatenate([cos, msin], axis=1)
    scale = jnp.where(k == 0, 1.0 / n, 2.0 / n).astype(F32)
    inv = jnp.concatenate([cos[:, :, :seq] * scale, msin[:, :, :seq] * scale], axis=1).transpose(0, 2, 1)
    return fwd.astype(BF16), inv.astype(BF16)


def _rope_tables(seq):
    t = jnp.arange(seq, dtype=jnp.int32)
    inv = ROPE_THETA ** (-jnp.arange(0, ROPE_AXIS, 2, dtype=F32) / ROPE_AXIS)
    ang_r = (t // GRID_W).astype(F32)[:, None] * inv[None, :]
    ang_c = (t % GRID_W).astype(F32)[:, None] * inv[None, :]
    cos = jnp.concatenate([jnp.cos(ang_r)] * 2 + [jnp.cos(ang_c)] * 2, axis=1)
    sin = jnp.concatenate([-jnp.sin(ang_r), jnp.sin(ang_r), -jnp.sin(ang_c), jnp.sin(ang_c)], axis=1)
    return jnp.tile(cos, (1, GQ_HEADS)), jnp.tile(sin, (1, GQ_HEADS))


def _pad_to(a, rows, cols):
    return jnp.pad(a, ((0, rows - a.shape[0]), (0, cols - a.shape[1])))


def _prepare(seq, p):
    fwd, inv = _dft_tables(seq)
    cos, sin = _rope_tables(seq)
    bands = jnp.linspace(1e-4, N_BANDS - 1, N_BANDS, dtype=F32)
    bands_row = jnp.zeros((1, LANES), F32).at[0, 1:1 + N_BANDS].set(bands).at[0, 1 + N_BANDS:1 + 2 * N_BANDS].set(bands)
    deltas = jnp.abs(jnp.linspace(MIN_DECAY, MAX_DECAY, HY_WIDTH, dtype=F32))
    dec = jnp.tile(deltas, HY_ORDER)[None, :]
    da_scale = DA_HALF ** -0.5 * math.log2(math.e)
    fq, fk = _alibi_tables(seq)
    layers = []
    for l in range(DEPTH):
        w_in, b_in = p['w_in'][l], p['b_in'][l]
        a0, b0, c0, g0 = 0, HY_COLS, HY_COLS + DA_COLS, HY_COLS + DA_COLS + GQ_COLS
        w_cols = [w_in[:, g0:] * 0.5, w_in[:, a0:b0], w_in[:, b0:b0 + DA_WIDTH] * da_scale, w_in[:, b0 + DA_WIDTH:c0], w_in[:, c0:g0]]
        b_cols = [b_in[g0:] * 0.5, b_in[a0:b0], b_in[b0:b0 + DA_WIDTH] * da_scale, b_in[b0 + DA_WIDTH:c0], b_in[c0:g0]]
        ts = _filters_time(
            seq, bands_row,
            _pad_to(p['hy_f_w1'][l], LANES, LANES), _pad_to(p['hy_f_b1'][l][None, :], 1, LANES),
            _pad_to(p['hy_f_w2'][l], LANES, LANES), _pad_to(p['hy_f_b2'][l][None, :], 1, LANES),
            _pad_to(p['hy_f_w3'][l], LANES, 2 * HY_ORDER * HY_WIDTH), p['hy_f_b3'][l][None, :], dec)
        kf = _filter_dft(fwd.reshape(2 * seq, 2 * seq), ts).reshape(fwd.shape[0], -1, HY_ORDER * HY_WIDTH)
        lay = dict(
            w_in=jnp.concatenate(w_cols, axis=1).astype(BF16),
            b_in=jnp.concatenate(b_cols)[None, :],
            conv_w=p['hy_conv_w'][l], conv_b=p['hy_conv_b'][l][None, :],
            kf=kf, hy_bias=p['hy_bias'][l],
            lam=jnp.stack([p['da_lam_q1'][l], p['da_lam_k1'][l], p['da_lam_q2'][l], p['da_lam_k2'][l]]),
            lam_init=0.8 - 0.6 * math.exp(-0.3 * l),
            subln=jnp.tile(p['da_subln_g'][l], DA_HEADS)[None, :],
            gq=jnp.tile(p['gq_qnorm_g'][l], GQ_HEADS)[None, :],
            gk=jnp.tile(p['gq_knorm_g'][l], GQ_KV)[None, :],
            wa=(p['w_br_a'][l] * 0.5).astype(BF16), wb=(p['w_br_b'][l] * 0.5).astype(BF16),
            wc=(p['w_br_c'][l] * 0.5).astype(BF16),
            wo=p['w_out'][l].astype(BF16),
            ln1_g=p['ln1_g'][l][None, :], ln1_b=p['ln1_b'][l][None, :],
            ln2_g=p['ln2_g'][l][None, :], ln2_b=p['ln2_b'][l][None, :],
        )
        if l % 2 == 0:
            lay.update(w_up=p['ffn_w_up'][l // 2].astype(BF16), w_down=p['ffn_w_down'][l // 2].astype(BF16))
        else:
            lay.update(w_up=p['moe_w_up'][l // 2].astype(BF16), w_down=p['moe_w_down'][l // 2].astype(BF16),
                       router=(_pad_to(p['moe_w_router'][l // 2], D_MODEL, LANES),
                               _pad_to(p['moe_b_router'][l // 2][None, :], 1, LANES)))
        layers.append(lay)
    return dict(fwd=fwd[:, :, :seq], inv=inv, cos=cos, sin=sin, fq=fq, fk=fk, layers=layers)


def _trunk(x, c, p, prep):
    mods = _ada(c, p['w_ada'], p['b_ada'])
    nb = x.shape[0]
    for l, lay in enumerate(prep['layers']):
        mod = mods[l].reshape(nb, 6, D_MODEL)
        z = _inproj(x, mod, lay['w_in'], lay['b_in'], lay['conv_w'], lay['conv_b'])
        hy = COL_HY // HY_WIDTH
        u = _longconv(z, hy + 2, z, hy, prep['fwd'], prep['inv'], lay['kf'], 0, lay['hy_bias'][0:1])
        ya = _longconv(u, 0, z, hy + 1, prep['fwd'], prep['inv'], lay['kf'], 1, lay['hy_bias'][1:2])
        yb = _diff_attn(z, prep['fq'], prep['fk'], lay['lam'], lay['subln'], lay['lam_init'])
        yc = _gqa_attn(z, prep['cos'], prep['sin'], lay['gq'], lay['gk'])
        merged = _merge(ya, yb, yc, z, x, mod, lay['wa'], lay['wb'], lay['wc'], lay['wo'],
                        lay['ln1_g'], lay['ln1_b'], lay.get('router'))
        if l % 2 == 0:
            x, h = merged
            x = _ffn(h, lay['w_up'], lay['w_down'], x, mod, lay['ln2_g'], lay['ln2_b'])
        else:
            x, h, comb = merged
            x = _moe(h, comb, lay['w_up'], lay['w_down'], x, mod, lay['ln2_g'], lay['ln2_b'])
    return x


def kernel(x_prompt, x_sample, c_prompt, c_sample, w_ada, b_ada, w_in, b_in, hy_conv_w, hy_conv_b, hy_f_w1, hy_f_b1, hy_f_w2, hy_f_b2, hy_f_w3, hy_f_b3, hy_bias, da_lam_q1, da_lam_k1, da_lam_q2, da_lam_k2, da_subln_g, gq_qnorm_g, gq_knorm_g, w_br_a, w_br_b, w_br_c, w_out, ln1_g, ln1_b, ffn_w_up, ffn_w_down, moe_w_router, moe_b_router, moe_w_up, moe_w_down, ln2_g, ln2_b):
    p = dict(w_ada=w_ada, b_ada=b_ada, w_in=w_in, b_in=b_in,
             hy_conv_w=hy_conv_w, hy_conv_b=hy_conv_b, hy_f_w1=hy_f_w1, hy_f_b1=hy_f_b1,
             hy_f_w2=hy_f_w2, hy_f_b2=hy_f_b2, hy_f_w3=hy_f_w3, hy_f_b3=hy_f_b3, hy_bias=hy_bias,
             da_lam_q1=da_lam_q1, da_lam_k1=da_lam_k1, da_lam_q2=da_lam_q2, da_lam_k2=da_lam_k2,
             da_subln_g=da_subln_g, gq_qnorm_g=gq_qnorm_g, gq_knorm_g=gq_knorm_g,
             w_br_a=w_br_a, w_br_b=w_br_b, w_br_c=w_br_c, w_out=w_out, ln1_g=ln1_g, ln1_b=ln1_b,
             ffn_w_up=ffn_w_up, ffn_w_down=ffn_w_down, moe_w_router=moe_w_router,
             moe_b_router=moe_b_router, moe_w_up=moe_w_up, moe_w_down=moe_w_down,
             ln2_g=ln2_g, ln2_b=ln2_b)
    assert x_prompt.shape[1] == x_sample.shape[1]
    prep = _prepare(x_prompt.shape[1], p)
    return (_trunk(x_prompt, c_prompt, p, prep), _trunk(x_sample, c_sample, p, prep))
```

```python
import functools
import math

import jax
import jax.numpy as jnp
from jax import lax
from jax.experimental import pallas as pl
from jax.experimental.pallas import tpu as pltpu

F32 = jnp.float32
BF16 = jnp.bfloat16

D_MODEL = 1024
DEPTH = 2
GRID_W = 64
HEAD_DIM = 64
HY_WIDTH = D_MODEL // 4
HY_ORDER = 2
SHORT_CONV = 3
N_BANDS = 16
FILTER_HID = 64
DECAY_TARGET = 1e-2
MIN_DECAY = math.log(DECAY_TARGET) / 1.5
MAX_DECAY = math.log(DECAY_TARGET) / 0.3
DECAY_SHIFT = 0.05
DA_HEADS = 4
DA_HALF = HEAD_DIM // 2
DA_WIDTH = DA_HEADS * HEAD_DIM
GQ_HEADS = 8
GQ_KV = 2
GQ_WIDTH = GQ_HEADS * HEAD_DIM
ROPE_AXIS = HEAD_DIM // 2
ROPE_THETA = 10000.0
HY_COLS = (HY_ORDER + 1) * HY_WIDTH
DA_COLS = 3 * DA_WIDTH
GQ_COLS = GQ_WIDTH + 2 * GQ_KV * HEAD_DIM
GATE_COLS = 3 * D_MODEL
IN_COLS = HY_COLS + DA_COLS + GQ_COLS + GATE_COLS
D_FF = 256 * ((8 * D_MODEL // 3 + 255) // 256)
N_EXPERTS = 8
MOE_FF = D_FF // 2
DEEPNORM_ALPHA = (2.0 * DEPTH) ** 0.25
EPS = 1e-5

LANES = 128
VMEM_LIMIT = 56 * 1024 * 1024

COL_GATE = 0
COL_HY = GATE_COLS
COL_DA = COL_HY + HY_COLS
COL_GQ = COL_DA + DA_COLS


def _params(sem):
    return pltpu.CompilerParams(dimension_semantics=sem, vmem_limit_bytes=VMEM_LIMIT)


def _dot(a, b):
    return jnp.dot(a, b, preferred_element_type=F32)


def _dot_nt(a, b):
    return lax.dot_general(a, b, (((1,), (1,)), ((), ())), preferred_element_type=F32)


def _split(x):
    hi = x.astype(BF16)
    lo = (x - hi.astype(F32)).astype(BF16)
    return hi, lo


def _dot3(a, b):
    ah, al = _split(a)
    bh, bl = _split(b)
    return _dot(ah, bh) + (_dot(ah, bl) + _dot(al, bh))


def _sigmoid(x):
    return 0.5 * jnp.tanh(0.5 * x) + 0.5


def _ln(x):
    mu = jnp.mean(x, axis=-1, keepdims=True)
    xc = x - mu
    var = jnp.mean(xc * xc, axis=-1, keepdims=True)
    return xc * lax.rsqrt(var + EPS)


def _head_ssq(x):
    w = x.shape[1]
    r = lax.broadcasted_iota(jnp.int32, (w, w), 0) // HEAD_DIM
    c = lax.broadcasted_iota(jnp.int32, (w, w), 1) // HEAD_DIM
    ones = jnp.where(r == c, 1.0, 0.0).astype(BF16)
    hi, lo = _split(x * x)
    return _dot(hi, ones) + _dot(lo, ones)


def _ada_kernel(c_ref, w_ref, b_ref, o_ref):
    c = c_ref[...]
    o_ref[...] = _dot3(c * _sigmoid(c), w_ref[...]) + b_ref[...]


def _ada(c, w_ada, b_ada):
    nb = c.shape[0]
    tn = 1536
    return pl.pallas_call(
        _ada_kernel,
        grid=(DEPTH, 6 * D_MODEL // tn),
        in_specs=[
            pl.BlockSpec((nb, D_MODEL), lambda l, j: (0, 0)),
            pl.BlockSpec((None, D_MODEL, tn), lambda l, j: (l, 0, j)),
            pl.BlockSpec((None, 1, tn), lambda l, j: (l, 0, j)),
        ],
        out_specs=pl.BlockSpec((None, nb, tn), lambda l, j: (l, 0, j)),
        out_shape=jax.ShapeDtypeStruct((DEPTH, nb, 6 * D_MODEL), F32),
        compiler_params=_params(("arbitrary", "arbitrary")),
        name="ada_mod",
    )(c, w_ada, b_ada.reshape(DEPTH, 1, 6 * D_MODEL))


LN_ROWS = 256
INPROJ_TN = 768
INPROJ_STEPS = IN_COLS // INPROJ_TN
HY_TILE = COL_HY // INPROJ_TN


def _inproj_kernel(x_ref, mod_ref, w_ref, b_ref, cw_ref, cb_ref, o_ref, ha_ref, hb_ref):
    b = pl.program_id(0)
    j = pl.program_id(1)
    seq = x_ref.shape[1]
    chunks = seq // LN_ROWS
    assert chunks <= INPROJ_STEPS + 1

    @pl.when((b == 0) & (j == 0))
    def _():
        hb_ref[...] = jnp.zeros_like(hb_ref)

    def step(h_ln, h_mm, ln_chunks, conv, col):
        shift = mod_ref[0, 0:1, :]
        scale = 1.0 + mod_ref[0, 1:2, :]
        for c in ln_chunks:
            r = pl.ds(pl.multiple_of(c * LN_ROWS, LN_ROWS), LN_ROWS)
            h_ln[r, :] = (_ln(x_ref[0, r, :]) * scale + shift).astype(BF16)
        cols = pl.ds(pl.multiple_of(col * INPROJ_TN, INPROJ_TN), INPROJ_TN)
        z = _dot(h_mm[...], w_ref[:, cols]) + b_ref[:, cols]
        if conv:
            row = lax.broadcasted_iota(jnp.int32, z.shape, 0)
            prev = jnp.where(row == 0, 0.0, pltpu.roll(z, 1, 0))
            nxt = jnp.where(row == seq - 1, 0.0, pltpu.roll(z, seq - 1, 0))
            z = cb_ref[...] + prev * cw_ref[0:1, :] + z * cw_ref[1:2, :] + nxt * cw_ref[2:3, :]
        o_ref[0] = z.astype(BF16)

    first = [0] + ([INPROJ_STEPS] if chunks > INPROJ_STEPS else [])
    later = [jnp.minimum(j, chunks - 1)]
    for parity, (h_ln, h_mm) in enumerate(((ha_ref, hb_ref), (hb_ref, ha_ref))):
        even = b % 2 == parity
        pl.when(even & (j == 0))(functools.partial(step, h_ln, h_mm, first, False, 0))
        pl.when(even & (j == HY_TILE))(functools.partial(step, h_ln, h_mm, later, True, HY_TILE))
        pl.when(even & (j != 0) & (j != HY_TILE))(functools.partial(step, h_ln, h_mm, later, False, j))


def _inproj(x, mod, w, b, conv_w, conv_b):
    nb, seq, _ = x.shape
    tn = INPROJ_TN
    assert HY_COLS == tn and COL_HY % tn == 0 and HY_TILE != 0 and seq % LN_ROWS == 0
    cur = lambda i, j: (jnp.minimum(i, nb - 1), 0, 0)
    return pl.pallas_call(
        _inproj_kernel,
        grid=(nb + 1, INPROJ_STEPS),
        in_specs=[
            pl.BlockSpec((1, seq, D_MODEL), cur),
            pl.BlockSpec((1, 6, D_MODEL), cur),
            pl.BlockSpec((D_MODEL, IN_COLS), lambda i, j: (0, 0), pipeline_mode=pl.Buffered(1)),
            pl.BlockSpec((1, IN_COLS), lambda i, j: (0, 0)),
            pl.BlockSpec((SHORT_CONV, tn), lambda i, j: (0, 0)),
            pl.BlockSpec((1, tn), lambda i, j: (0, 0)),
        ],
        out_specs=pl.BlockSpec((1, seq, tn), lambda i, j: (jnp.maximum(i - 1, 0), 0, jnp.where(i == 0, 0, j))),
        out_shape=jax.ShapeDtypeStruct((nb, seq, IN_COLS), BF16),
        scratch_shapes=[pltpu.VMEM((seq, D_MODEL), BF16), pltpu.VMEM((seq, D_MODEL), BF16)],
        compiler_params=_params(("arbitrary", "arbitrary")),
        name="ln_inproj",
    )(x, mod, w, b, conv_w, conv_b)


FILT_ROWS = 512


def _filter_kernel(bands_ref, w1_ref, b1_ref, w2_ref, b2_ref, w3_ref, b3_ref, dec_ref, o_ref, ts_ref, *, seq):
    n = 2 * seq
    rows = min(FILT_ROWS, n)
    hw = HY_ORDER * HY_WIDTH
    lane = lax.broadcasted_iota(jnp.int32, (1, LANES), 1)

    def fill(i, asum):
        j0 = pl.multiple_of(i * rows, rows)
        j = j0 + lax.broadcasted_iota(jnp.int32, (rows, 1), 0)
        t = jnp.where(j < seq, j, n - j).astype(F32)
        t_norm = t / max(seq - 1, 1)
        ang = (2.0 * math.pi / seq) * t * bands_ref[...]
        feats = jnp.where(lane == 0, t_norm,
                          jnp.where(lane <= N_BANDS, jnp.cos(ang),
                                    jnp.where(lane <= 2 * N_BANDS, -jnp.sin(ang), 0.0)))
        h = jnp.sin(_dot3(feats, w1_ref[...]) + b1_ref[...])
        h = jnp.sin(_dot3(h, w2_ref[...]) + b2_ref[...])
        h = _dot3(h, w3_ref[...]) + b3_ref[...]
        window = jnp.exp(-t_norm * dec_ref[...]) + DECAY_SHIFT
        sel = jnp.where(j < seq, h[:, :hw], h[:, hw:]) * window
        sel = jnp.where(j == seq, 0.0, sel)
        ts_ref[pl.ds(j0, rows), :] = sel
        return asum + jnp.sum(jnp.abs(sel), axis=0, keepdims=True)

    asum = lax.fori_loop(0, n // rows, fill, jnp.zeros((1, hw), F32))
    inv = 1.0 / (asum + EPS)

    def norm(i, carry):
        r = pl.ds(pl.multiple_of(i * rows, rows), rows)
        o_ref[r, :] = (ts_ref[r, :] * inv).astype(BF16)
        return carry

    lax.fori_loop(0, n // rows, norm, 0)


def _filters_time(seq, bands, w1, b1, w2, b2, w3, b3, dec):
    hw = HY_ORDER * HY_WIDTH
    return pl.pallas_call(
        functools.partial(_filter_kernel, seq=seq),
        out_shape=jax.ShapeDtypeStruct((2 * seq, hw), BF16),
        scratch_shapes=[pltpu.VMEM((2 * seq, hw), F32)],
        compiler_params=pltpu.CompilerParams(vmem_limit_bytes=VMEM_LIMIT),
        name="hy_filter_time",
    )(bands, w1, b1, w2, b2, w3, b3, dec)


def _matmul_kernel(a_ref, b_ref, o_ref):
    o_ref[...] = _dot(a_ref[...], b_ref[...])


def _filter_dft(wf, ts):
    n, hw = ts.shape
    tm = min(512, n)
    return pl.pallas_call(
        _matmul_kernel,
        grid=(n // tm,),
        in_specs=[pl.BlockSpec((tm, n), lambda i: (i, 0)), pl.BlockSpec((n, hw), lambda i: (0, 0))],
        out_specs=pl.BlockSpec((tm, hw), lambda i: (i, 0)),
        out_shape=jax.ShapeDtypeStruct((n, hw), F32),
        compiler_params=_params(("arbitrary",)),
        name="hy_filter_dft",
    )(wf, ts)


def _longconv_kernel(v_ref, g_ref, w_ref, wi_ref, kf_ref, bias_ref, o_ref):
    nb = v_ref.shape[0]
    nk = w_ref.shape[0]
    tf = w_ref.shape[1] // 2
    first = lax.broadcasted_iota(jnp.int32, (tf, 1), 0) == 0
    for n in range(nb):
        v = v_ref[n]
        acc = None
        for kt in range(nk):
            u = _dot(w_ref[kt], v)
            ur, ui = u[:tf], u[tf:]
            kr = kf_ref[kt, :tf, :]
            ki = kf_ref[kt, tf:, :]
            if kt == 0:
                gr = ur * kr - jnp.where(first, 0.0, ui * ki)
                gi = jnp.where(first, ui * ki, ur * ki + ui * kr)
            else:
                gr = ur * kr - ui * ki
                gi = ur * ki + ui * kr
            part = _dot(wi_ref[kt], jnp.concatenate([gr, gi], axis=0).astype(BF16))
            acc = part if acc is None else acc + part
        y = (acc + v.astype(F32) * bias_ref[...]).astype(BF16)
        o_ref[n] = (g_ref[n] * y).astype(BF16)


CONV_TF = 256
CONV_NB = 2


def _longconv(v_arr, v_col, g_arr, g_col, w, wi, kf, order, bias):
    nbatch, seq, _ = v_arr.shape
    nb = math.gcd(nbatch, CONV_NB)
    nk, tf2, _ = w.shape
    c = HY_WIDTH
    once = pl.Buffered(1)
    return pl.pallas_call(
        _longconv_kernel,
        grid=(nbatch // nb,),
        in_specs=[
            pl.BlockSpec((nb, seq, c), lambda i: (i, 0, v_col)),
            pl.BlockSpec((nb, seq, c), lambda i: (i, 0, g_col)),
            pl.BlockSpec((nk, tf2, seq), lambda i: (0, 0, 0), pipeline_mode=once),
            pl.BlockSpec((nk, seq, tf2), lambda i: (0, 0, 0), pipeline_mode=once),
            pl.BlockSpec((nk, tf2, c), lambda i: (0, 0, order), pipeline_mode=once),
            pl.BlockSpec((1, c), lambda i: (0, 0)),
        ],
        out_specs=pl.BlockSpec((nb, seq, c), lambda i: (i, 0, 0)),
        out_shape=jax.ShapeDtypeStruct((nbatch, seq, c), BF16),
        compiler_params=_params(("arbitrary",)),
        name="hy_longconv",
    )(v_arr, g_arr, w, wi, kf, bias)


def _lane_mask(lo, width, n=LANES):
    lane = lax.broadcasted_iota(jnp.int32, (1, n), 1)
    return (lane >= lo) & (lane < lo + width)


ALIBI_SPLIT = 3
POS_RADIX = 256


def _da_coef(head):
    return 2.0 ** (-8.0 * (head + 1) / DA_HEADS) * math.log2(math.e)


def _alibi_tables(seq):
    pos = jnp.arange(seq, dtype=jnp.int32)
    hi = ((pos // POS_RADIX) * POS_RADIX).astype(F32)[:, None]
    lo = (pos % POS_RADIX).astype(F32)[:, None]
    ones = jnp.ones((seq, 1), F32)
    fq, fk = [], []
    for head in range(DA_HEADS):
        rest = jnp.float32(_da_coef(head))
        pieces = []
        for _ in range(ALIBI_SPLIT):
            piece = rest.astype(BF16).astype(F32)
            pieces.append(piece)
            rest = rest - piece
        cq = jnp.concatenate([ones * c for c in pieces], axis=1)
        fq.append(jnp.concatenate([hi] * ALIBI_SPLIT + [lo] * ALIBI_SPLIT + [cq, cq], axis=1))
        fk.append(jnp.concatenate([-cq, -cq] + [hi] * ALIBI_SPLIT + [lo] * ALIBI_SPLIT, axis=1))
    pad = lambda t: jnp.pad(t, ((0, 0), (0, 0), (0, LANES - 4 * ALIBI_SPLIT))).astype(BF16)
    fk = jnp.stack(fk)
    tq = min(DA_TQ, seq)
    key_tile = (pos // tq)[None, :, None]
    signed = [pad(fk * jnp.where(key_tile < qi, 1.0, jnp.where(key_tile > qi, -1.0, 0.0))) for qi in range(seq // tq)]
    return pad(jnp.stack(fq)), signed


def _da_kernel(q_ref, k_ref, v_ref, fq_ref, fk_ref, lam_ref, g_ref, o_ref, *, qi, lam_init):
    tq = q_ref.shape[1]
    seq = k_ref.shape[1]
    lv = lam_ref[...]
    lam = (jnp.exp(jnp.sum(lv[0:1] * lv[1:2], axis=1, keepdims=True))
           - jnp.exp(jnp.sum(lv[2:3] * lv[3:4], axis=1, keepdims=True)) + lam_init)
    local = jnp.abs(lax.broadcasted_iota(jnp.int32, (tq, tq), 0)
                    - lax.broadcasted_iota(jnp.int32, (tq, tq), 1)).astype(F32)
    slabs = []
    for s in range(DA_WIDTH // LANES):
        cols = slice(s * LANES, (s + 1) * LANES)
        q_slab = q_ref[0, :, cols]
        k_slab = k_ref[0, :, cols]
        v_slab = v_ref[0, :, cols]
        o_slab = jnp.zeros((tq, LANES), F32)
        for hh in range(LANES // HEAD_DIM):
            head = s * (LANES // HEAD_DIM) + hh
            k_aug = jnp.concatenate([k_slab, fk_ref[head]], axis=1)
            mine = _lane_mask(hh * HEAD_DIM, HEAD_DIM)
            v_ones = jnp.where(mine, v_slab, jnp.ones_like(v_slab))
            maps = []
            for c in range(2):
                keep = jnp.where(_lane_mask(hh * HEAD_DIM + c * DA_HALF, DA_HALF), 1.0, 0.0).astype(BF16)
                q_aug = jnp.concatenate([q_slab * keep, fq_ref[head]], axis=1)
                sc = _dot_nt(q_aug, k_aug)
                pieces = [sc[:, qi * tq:(qi + 1) * tq] - _da_coef(head) * local]
                if qi > 0:
                    pieces.insert(0, sc[:, :qi * tq])
                if (qi + 1) * tq < seq:
                    pieces.append(sc[:, (qi + 1) * tq:])
                sc = jnp.concatenate(pieces, axis=1)
                p = jnp.exp2(sc - jnp.max(sc, axis=1, keepdims=True))
                o2 = _dot(p.astype(BF16), v_ones)
                maps.append((o2, o2[:, (1 - hh) * HEAD_DIM:(1 - hh) * HEAD_DIM + 1]))
            diff = maps[0][0] * (1.0 / maps[0][1]) - maps[1][0] * (lam / maps[1][1])
            o_slab = jnp.where(mine, diff, o_slab)
        slabs.append(o_slab)
    o = jnp.concatenate(slabs, axis=1)
    y = o * lax.rsqrt(_head_ssq(o) * (1.0 / HEAD_DIM) + EPS) * g_ref[...]
    o_ref[0] = (y * (1.0 - lam_init)).astype(BF16)


DA_TQ = 512


def _diff_attn(z, fq, fk_signed, lam_vecs, g_tiled, lam_init):
    nb, seq, _ = z.shape
    tq = min(DA_TQ, seq)
    cb = COL_DA // DA_WIDTH
    tiles = []
    for qi in range(seq // tq):
        tiles.append(pl.pallas_call(
            functools.partial(_da_kernel, qi=qi, lam_init=lam_init),
            grid=(nb,),
            in_specs=[
                pl.BlockSpec((1, tq, DA_WIDTH), lambda b, qi=qi: (b, qi, cb)),
                pl.BlockSpec((1, seq, DA_WIDTH), lambda b: (b, 0, cb + 1)),
                pl.BlockSpec((1, seq, DA_WIDTH), lambda b: (b, 0, cb + 2)),
                pl.BlockSpec((DA_HEADS, tq, LANES), lambda b, qi=qi: (0, qi, 0)),
                pl.BlockSpec((DA_HEADS, seq, LANES), lambda b: (0, 0, 0)),
                pl.BlockSpec((4, DA_HALF), lambda b: (0, 0)),
                pl.BlockSpec((1, DA_WIDTH), lambda b: (0, 0)),
            ],
            out_specs=pl.BlockSpec((1, tq, DA_WIDTH), lambda b: (b, 0, 0)),
            out_shape=jax.ShapeDtypeStruct((nb, tq, DA_WIDTH), BF16),
            compiler_params=_params(("arbitrary",)),
            name="diff_attn",
        )(z, z, z, fq, fk_signed[qi], lam_vecs, g_tiled))
    return jnp.concatenate(tiles, axis=1)


def _rope(x, cos, sin):
    w = x.shape[1]
    lane = lax.broadcasted_iota(jnp.int32, (1, w), 1)
    half = ROPE_AXIS // 2
    partner = jnp.where(lane % ROPE_AXIS < half, pltpu.roll(x, w - half, 1), pltpu.roll(x, half, 1))
    return x * cos + partner * sin


def _dup_heads(x):
    first = _lane_mask(0, HEAD_DIM)
    a = jnp.where(first, x, 0.0)
    b = jnp.where(first, 0.0, x)
    return jnp.concatenate([a + pltpu.roll(a, HEAD_DIM, 1), b + pltpu.roll(b, HEAD_DIM, 1)], axis=1)


def _with_ones(v):
    first = _lane_mask(0, HEAD_DIM)
    swapped = pltpu.roll(v, HEAD_DIM, 1)
    return jnp.concatenate([jnp.where(first, v, 1.0), jnp.where(first, 1.0, swapped),
                            jnp.where(first, swapped, 1.0), jnp.where(first, 1.0, v)], axis=1)


def _gqa_kernel(zq_ref, zkv_ref, cosq_ref, sinq_ref, cosk_ref, sink_ref, gq_ref, gk_ref, o_ref, kd_ref, vd_ref):
    tq = zq_ref.shape[1]
    kvw = GQ_KV * HEAD_DIM
    per_slab = LANES // HEAD_DIM
    group = GQ_HEADS // GQ_KV

    @pl.when(pl.program_id(1) == 0)
    def _():
        k = zkv_ref[0, :, :kvw].astype(F32)
        kn = k * lax.rsqrt(_head_ssq(k) * (1.0 / HEAD_DIM) + EPS) * gk_ref[...]
        kd_ref[...] = _dup_heads(_rope(kn, cosk_ref[...], sink_ref[...])).astype(BF16)
        vd_ref[...] = _with_ones(zkv_ref[0, :, kvw:].astype(F32)).astype(BF16)

    q = zq_ref[0].astype(F32)
    qn = q * lax.rsqrt(_head_ssq(q) * (1.0 / HEAD_DIM) + EPS) * gq_ref[...]
    qr = (_rope(qn, cosq_ref[...], sinq_ref[...]) * (HEAD_DIM ** -0.5 * math.log2(math.e))).astype(BF16)
    slabs = []
    for s in range(GQ_WIDTH // LANES):
        q_slab = qr[:, s * LANES:(s + 1) * LANES]
        kv = (s * per_slab) // group
        k_dup = kd_ref[:, kv * LANES:(kv + 1) * LANES]
        o_slab = jnp.zeros((tq, LANES), F32)
        for hh in range(per_slab):
            keep = jnp.where(_lane_mask(hh * HEAD_DIM, HEAD_DIM), 1.0, 0.0).astype(BF16)
            sc = _dot_nt(q_slab * keep, k_dup)
            p = jnp.exp2(sc - jnp.max(sc, axis=1, keepdims=True))
            v_ones = vd_ref[:, (kv * per_slab + hh) * LANES:(kv * per_slab + hh + 1) * LANES]
            o2 = _dot(p.astype(BF16), v_ones)
            sums = o2[:, (1 - hh) * HEAD_DIM:(1 - hh) * HEAD_DIM + 1]
            o_slab = jnp.where(_lane_mask(hh * HEAD_DIM, HEAD_DIM), o2 * (1.0 / sums), o_slab)
        slabs.append(o_slab)
    o_ref[0] = jnp.concatenate(slabs, axis=1).astype(BF16)


def _gqa_attn(z, cos, sin, gq, gk):
    nb, seq, _ = z.shape
    tq = min(512, seq)
    kvw = GQ_KV * HEAD_DIM
    return pl.pallas_call(
        _gqa_kernel,
        grid=(nb, seq // tq),
        in_specs=[
            pl.BlockSpec((1, tq, GQ_WIDTH), lambda b, i: (b, i, COL_GQ // GQ_WIDTH)),
            pl.BlockSpec((1, seq, 2 * kvw), lambda b, i: (b, 0, (COL_GQ + GQ_WIDTH) // (2 * kvw))),
            pl.BlockSpec((tq, GQ_WIDTH), lambda b, i: (i, 0)),
            pl.BlockSpec((tq, GQ_WIDTH), lambda b, i: (i, 0)),
            pl.BlockSpec((seq, kvw), lambda b, i: (0, 0)),
            pl.BlockSpec((seq, kvw), lambda b, i: (0, 0)),
            pl.BlockSpec((1, GQ_WIDTH), lambda b, i: (0, 0)),
            pl.BlockSpec((1, kvw), lambda b, i: (0, 0)),
        ],
        out_specs=pl.BlockSpec((1, tq, GQ_WIDTH), lambda b, i: (b, i, 0)),
        out_shape=jax.ShapeDtypeStruct((nb, seq, GQ_WIDTH), BF16),
        scratch_shapes=[pltpu.VMEM((seq, 2 * kvw), BF16), pltpu.VMEM((seq, 4 * kvw), BF16)],
        compiler_params=_params(("arbitrary", "arbitrary")),
        name="gqa_attn",
    )(z, z, cos, sin, cos, sin, gq, gk)


def _merge_kernel(ya_ref, yb_ref, yc_ref, zg_ref, x_ref, mod_ref, wa_ref, wb_ref, wc_ref, wo_ref,
                  lg_ref, lb_ref, *rest):
    d = D_MODEL
    m = (jnp.tanh(zg_ref[0, :, 0:d].astype(F32)) + 1.0) * _dot(ya_ref[0], wa_ref[...])
    m = m + (jnp.tanh(zg_ref[0, :, d:2 * d].astype(F32)) + 1.0) * _dot(yb_ref[0], wb_ref[...])
    m = m + (jnp.tanh(zg_ref[0, :, 2 * d:3 * d].astype(F32)) + 1.0) * _dot(yc_ref[0], wc_ref[...])
    y = _dot(m.astype(BF16), wo_ref[...])
    gate1 = 1.0 + mod_ref[0, 2:3, :]
    xn = _ln(DEEPNORM_ALPHA * x_ref[0] + gate1 * y) * lg_ref[...] + lb_ref[...]
    h = _ln(xn) * (1.0 + mod_ref[0, 4:5, :]) + mod_ref[0, 3:4, :]
    if len(rest) == 2:
        xo_ref, h_ref = rest
    else:
        wr_ref, br_ref, xo_ref, h_ref, comb_ref = rest
        lane = lax.broadcasted_iota(jnp.int32, (1, LANES), 1).astype(F32)
        neg = -jnp.inf
        logits = jnp.where(lane < N_EXPERTS, _dot3(h, wr_ref[...]) + br_ref[...], neg)
        v1 = jnp.max(logits, axis=1, keepdims=True)
        i1 = jnp.min(jnp.where(logits == v1, lane, float(LANES)), axis=1, keepdims=True)
        others = jnp.where(lane == i1, neg, logits)
        v2 = jnp.max(others, axis=1, keepdims=True)
        i2 = jnp.min(jnp.where(others == v2, lane, float(LANES)), axis=1, keepdims=True)
        e = jnp.exp(v2 - v1)
        w1 = 1.0 / (1.0 + e)
        comb_ref[0] = jnp.where(lane == i1, w1, 0.0) + jnp.where(lane == i2, e * w1, 0.0)
    xo_ref[0] = xn
    h_ref[0] = h.astype(BF16)


def _merge(ya, yb, yc, z, x, mod, wa, wb, wc, wo, lg, lb, router=None):
    nb, seq, d = x.shape
    tm = min(512, seq)
    full = lambda shape: pl.BlockSpec(shape, lambda b, i: (0,) * len(shape))
    row = lambda w, col=0: pl.BlockSpec((1, tm, w), lambda b, i: (b, i, col))
    in_specs = [
        row(HY_WIDTH), row(DA_WIDTH), row(GQ_WIDTH), row(GATE_COLS, COL_GATE // GATE_COLS), row(d),
        pl.BlockSpec((1, 6, d), lambda b, i: (b, 0, 0)),
        full((HY_WIDTH, d)), full((DA_WIDTH, d)), full((GQ_WIDTH, d)), full((d, d)),
        full((1, d)), full((1, d)),
    ]
    out_specs = [row(d), row(d)]
    out_shape = [jax.ShapeDtypeStruct((nb, seq, d), F32), jax.ShapeDtypeStruct((nb, seq, d), BF16)]
    args = [ya, yb, yc, z, x, mod, wa, wb, wc, wo, lg, lb]
    if router is not None:
        in_specs += [full((d, LANES)), full((1, LANES))]
        out_specs.append(row(LANES))
        out_shape.append(jax.ShapeDtypeStruct((nb, seq, LANES), F32))
        args += list(router)
    return pl.pallas_call(
        _merge_kernel,
        grid=(nb, seq // tm),
        in_specs=in_specs,
        out_specs=out_specs,
        out_shape=out_shape,
        compiler_params=_params(("arbitrary", "arbitrary")),
        name="merge_outproj",
    )(*args)


FFN_TF = 256


def _ffn_kernel(h_ref, wup_ref, wd_ref, x_ref, mod_ref, lg_ref, lb_ref, o_ref):
    h = h_ref[0]
    ff = wd_ref.shape[0]
    acc = None
    for c in range(ff // FFN_TF):
        cols = slice(c * FFN_TF, (c + 1) * FFN_TF)
        g = _dot(h, wup_ref[:, cols])
        u = _dot(h, wup_ref[:, ff + c * FFN_TF:ff + (c + 1) * FFN_TF])
        part = _dot((g * _sigmoid(g) * u).astype(BF16), wd_ref[cols, :])
        acc = part if acc is None else acc + part
    gate2 = 1.0 + mod_ref[0, 5:6, :]
    o_ref[0] = _ln(DEEPNORM_ALPHA * x_ref[0] + gate2 * acc) * lg_ref[...] + lb_ref[...]


def _ffn(h, w_up, w_down, x, mod, lg, lb):
    nb, seq, d = x.shape
    ff = w_down.shape[0]
    assert ff % FFN_TF == 0
    tm = min(1024, seq)
    row = lambda w: pl.BlockSpec((1, tm, w), lambda b, i: (b, i, 0))
    vec = pl.BlockSpec((1, d), lambda b, i: (0, 0))
    once = pl.Buffered(1)
    return pl.pallas_call(
        _ffn_kernel,
        grid=(nb, seq // tm),
        in_specs=[
            row(d),
            pl.BlockSpec((d, 2 * ff), lambda b, i: (0, 0), pipeline_mode=once),
            pl.BlockSpec((ff, d), lambda b, i: (0, 0), pipeline_mode=once),
            row(d),
            pl.BlockSpec((1, 6, d), lambda b, i: (b, 0, 0)),
            vec, vec,
        ],
        out_specs=row(d),
        out_shape=jax.ShapeDtypeStruct((nb, seq, d), F32),
        compiler_params=_params(("arbitrary",) * 2),
        name="swiglu_dense",
    )(h, w_up, w_down, x, mod, lg, lb)


MOE_TOKENS = 1024
MOE_SLOT = 384
MOE_CLASSES = (128, 192, 256, 320, MOE_SLOT)
MOE_EXTRA = (128, 256, MOE_SLOT)


def _moe_kernel(h_ref, comb_ref, wg_ref, wu_ref, wd_ref, x_ref, mod_ref, lg_ref, lb_ref, o_ref,
                posc_ref, posr_ref, cnt_ref, ys_ref):
    e = pl.program_id(2)
    tm = h_ref.shape[1]
    n_exp = ys_ref.shape[0] // MOE_SLOT
    lane = lax.broadcasted_iota(jnp.int32, (1, LANES), 1)

    @pl.when(e == 0)
    def _():
        comb = comb_ref[0]
        sel = jnp.where(comb > 0.0, 1.0, 0.0)
        r = lax.broadcasted_iota(jnp.int32, (tm, tm), 0)
        c = lax.broadcasted_iota(jnp.int32, (tm, tm), 1)
        before = jnp.where(c < r, 1.0, 0.0).astype(BF16)
        rank = _dot(before, sel.astype(BF16))
        posc = jnp.where(comb > 0.0, rank, -1.0)
        posc_ref[...] = posc
        posr_ref[...] = posc.T
        cnt_ref[...] = jnp.sum(sel, axis=0, keepdims=True)
        o_ref[0] = jnp.zeros(o_ref.shape[1:], F32)
        ys_ref[...] = jnp.zeros_like(ys_ref)

    mine = lane == e
    pos_row = posr_ref[pl.ds(e, 1), :]
    count = jnp.sum(jnp.where(mine, cnt_ref[...], 0.0))

    def expert_rows(base, rows):
        slot_r = (base + lax.broadcasted_iota(jnp.int32, (rows, 1), 0)).astype(F32)
        gather = jnp.where(pos_row == slot_r, 1.0, 0.0).astype(BF16)
        xs = _dot(gather, h_ref[0]).astype(BF16)
        g = _dot(xs, wg_ref[...])
        u = _dot(xs, wu_ref[...])
        a = (g * _sigmoid(g) * u).astype(BF16)
        return _dot(a, wd_ref[...]).astype(BF16)

    def fill_slot(rows):
        ys_ref[pl.ds(pl.multiple_of(e * MOE_SLOT, MOE_SLOT), rows), :] = expert_rows(0, rows)

    bounds = (0,) + MOE_CLASSES
    for lo, rows in zip(bounds[:-1], bounds[1:]):
        in_class = (count > lo) & (count <= rows) if rows < MOE_SLOT else count > lo
        pl.when(in_class)(functools.partial(fill_slot, rows))

    def scatter_rows(base, rows):
        pos_col = jnp.sum(jnp.where(mine, posc_ref[...], 0.0), axis=1, keepdims=True)
        w_col = jnp.sum(jnp.where(mine, comb_ref[0], 0.0), axis=1, keepdims=True)
        slot_c = (base + lax.broadcasted_iota(jnp.int32, (1, rows), 1)).astype(F32)
        scatter = jnp.where(pos_col == slot_c, 1.0, 0.0).astype(BF16)
        o_ref[0] += w_col * _dot(scatter, expert_rows(base, rows))

    extra = (0,) + MOE_EXTRA
    for lo, rows in zip(extra[:-1], extra[1:]):
        over = count - MOE_SLOT
        in_class = (over > lo) & (over <= rows) if rows < MOE_SLOT else over > lo
        pl.when(in_class)(functools.partial(scatter_rows, MOE_SLOT, rows))
    for c in range(2, -(-tm // MOE_SLOT)):
        pl.when(count > c * MOE_SLOT)(functools.partial(scatter_rows, c * MOE_SLOT, MOE_SLOT))

    @pl.when(e == pl.num_programs(2) - 1)
    def _():
        slot_c = lax.broadcasted_iota(jnp.int32, (1, MOE_SLOT), 1).astype(F32)
        f = o_ref[0]
        for e0 in range(0, n_exp, 2):
            weights = jnp.concatenate(
                [jnp.where(posc_ref[:, ee:ee + 1] == slot_c, comb_ref[0, :, ee:ee + 1], 0.0).astype(BF16)
                 for ee in (e0, e0 + 1)], axis=1)
            f = f + _dot(weights, ys_ref[e0 * MOE_SLOT:(e0 + 2) * MOE_SLOT, :])
        gate2 = 1.0 + mod_ref[0, 5:6, :]
        o_ref[0] = _ln(DEEPNORM_ALPHA * x_ref[0] + gate2 * f) * lg_ref[...] + lb_ref[...]


def _moe(h, comb, w_up, w_down, x, mod, lg, lb):
    nb, seq, d = x.shape
    ne, ff, _ = w_down.shape
    tm = min(MOE_TOKENS, seq)
    row = lambda w: pl.BlockSpec((1, tm, w), lambda b, i, e: (b, i, 0))
    vec = pl.BlockSpec((1, d), lambda b, i, e: (0, 0))
    return pl.pallas_call(
        _moe_kernel,
        grid=(nb, seq // tm, ne),
        in_specs=[
            row(d), row(LANES),
            pl.BlockSpec((None, d, ff), lambda b, i, e: (e, 0, 0)),
            pl.BlockSpec((None, d, ff), lambda b, i, e: (e, 0, 1)),
            pl.BlockSpec((None, ff, d), lambda b, i, e: (e, 0, 0)),
            row(d),
            pl.BlockSpec((1, 6, d), lambda b, i, e: (b, 0, 0)),
            vec, vec,
        ],
        out_specs=row(d),
        out_shape=jax.ShapeDtypeStruct((nb, seq, d), F32),
        scratch_shapes=[pltpu.VMEM((tm, LANES), F32), pltpu.VMEM((LANES, tm), F32), pltpu.VMEM((1, LANES), F32),
                        pltpu.VMEM((ne * MOE_SLOT, d), BF16)],
        compiler_params=_params(("arbitrary",) * 3),
        name="swiglu_routed",
    )(h, comb, w_up, w_up, w_down, x, mod, lg, lb)


def _dft_tables(seq):
    n = 2 * seq
    tf = min(CONV_TF, seq)
    nk = seq // tf
    s = jnp.arange(n, dtype=jnp.int32)[None, :]

    def phase(mult):
        ang = ((mult * s) % n).astype(F32) * (2.0 * math.pi / n)
        return jnp.cos(ang), jnp.sin(ang)

    ca, sa = phase(jnp.arange(nk, dtype=jnp.int32)[:, None] * tf)
    cb, sb = phase(jnp.arange(tf, dtype=jnp.int32)[:, None])
    cos = ca[:, None, :] * cb[None] - sa[:, None, :] * sb[None]
    msin = -(sa[:, None, :] * cb[None] + ca[:, None, :] * sb[None])
    k = (jnp.arange(nk, dtype=jnp.int32)[:, None] * tf + jnp.arange(tf, dtype=jnp.int32)[None, :])[:, :, None]
    msin = jnp.where(k == 0, jnp.where(s % 2 == 0, 1.0, -1.0)[None], msin)
    fwd = jnp.concatenate([cos, msin], axis=1)
    scale = jnp.where(k == 0, 1.0 / n, 2.0 / n).astype(F32)
    inv = jnp.concatenate([cos[:, :, :seq] * scale, msin[:, :, :seq] * scale], axis=1).transpose(0, 2, 1)
    return fwd.astype(BF16), inv.astype(BF16)


def _rope_tables(seq):
    t = jnp.arange(seq, dtype=jnp.int32)
    inv = ROPE_THETA ** (-jnp.arange(0, ROPE_AXIS, 2, dtype=F32) / ROPE_AXIS)
    ang_r = (t // GRID_W).astype(F32)[:, None] * inv[None, :]
    ang_c = (t % GRID_W).astype(F32)[:, None] * inv[None, :]
    cos = jnp.concatenate([jnp.cos(ang_r)] * 2 + [jnp.cos(ang_c)] * 2, axis=1)
    sin = jnp.concatenate([-jnp.sin(ang_r), jnp.sin(ang_r), -jnp.sin(ang_c), jnp.sin(ang_c)], axis=1)
    return jnp.tile(cos, (1, GQ_HEADS)), jnp.tile(sin, (1, GQ_HEADS))


def _pad_to(a, rows, cols):
    return jnp.pad(a, ((0, rows - a.shape[0]), (0, cols - a.shape[1])))


def _prepare(seq, p):
    fwd, inv = _dft_tables(seq)
    cos, sin = _rope_tables(seq)
    bands = jnp.linspace(1e-4, N_BANDS - 1, N_BANDS, dtype=F32)
    bands_row = jnp.zeros((1, LANES), F32).at[0, 1:1 + N_BANDS].set(bands).at[0, 1 + N_BANDS:1 + 2 * N_BANDS].set(bands)
    deltas = jnp.abs(jnp.linspace(MIN_DECAY, MAX_DECAY, HY_WIDTH, dtype=F32))
    dec = jnp.tile(deltas, HY_ORDER)[None, :]
    da_scale = DA_HALF ** -0.5 * math.log2(math.e)
    fq, fk = _alibi_tables(seq)
    layers = []
    for l in range(DEPTH):
        w_in, b_in = p['w_in'][l], p['b_in'][l]
        a0, b0, c0, g0 = 0, HY_COLS, HY_COLS + DA_COLS, HY_COLS + DA_COLS + GQ_COLS
        w_cols = [w_in[:, g0:] * 0.5, w_in[:, a0:b0], w_in[:, b0:b0 + DA_WIDTH] * da_scale, w_in[:, b0 + DA_WIDTH:c0], w_in[:, c0:g0]]
        b_cols = [b_in[g0:] * 0.5, b_in[a0:b0], b_in[b0:b0 + DA_WIDTH] * da_scale, b_in[b0 + DA_WIDTH:c0], b_in[c0:g0]]
        ts = _filters_time(
            seq, bands_row,
            _pad_to(p['hy_f_w1'][l], LANES, LANES), _pad_to(p['hy_f_b1'][l][None, :], 1, LANES),
            _pad_to(p['hy_f_w2'][l], LANES, LANES), _pad_to(p['hy_f_b2'][l][None, :], 1, LANES),
            _pad_to(p['hy_f_w3'][l], LANES, 2 * HY_ORDER * HY_WIDTH), p['hy_f_b3'][l][None, :], dec)
        kf = _filter_dft(fwd.reshape(2 * seq, 2 * seq), ts).reshape(fwd.shape[0], -1, HY_ORDER * HY_WIDTH)
        lay = dict(
            w_in=jnp.concatenate(w_cols, axis=1).astype(BF16),
            b_in=jnp.concatenate(b_cols)[None, :],
            conv_w=p['hy_conv_w'][l], conv_b=p['hy_conv_b'][l][None, :],
            kf=kf, hy_bias=p['hy_bias'][l],
            lam=jnp.stack([p['da_lam_q1'][l], p['da_lam_k1'][l], p['da_lam_q2'][l], p['da_lam_k2'][l]]),
            lam_init=0.8 - 0.6 * math.exp(-0.3 * l),
            subln=jnp.tile(p['da_subln_g'][l], DA_HEADS)[None, :],
            gq=jnp.tile(p['gq_qnorm_g'][l], GQ_HEADS)[None, :],
            gk=jnp.tile(p['gq_knorm_g'][l], GQ_KV)[None, :],
            wa=(p['w_br_a'][l] * 0.5).astype(BF16), wb=(p['w_br_b'][l] * 0.5).astype(BF16),
            wc=(p['w_br_c'][l] * 0.5).astype(BF16),
            wo=p['w_out'][l].astype(BF16),
            ln1_g=p['ln1_g'][l][None, :], ln1_b=p['ln1_b'][l][None, :],
            ln2_g=p['ln2_g'][l][None, :], ln2_b=p['ln2_b'][l][None, :],
        )
        if l % 2 == 0:
            lay.update(w_up=p['ffn_w_up'][l // 2].astype(BF16), w_down=p['ffn_w_down'][l // 2].astype(BF16))
        else:
            lay.update(w_up=p['moe_w_up'][l // 2].astype(BF16), w_down=p['moe_w_down'][l // 2].astype(BF16),
                       router=(_pad_to(p['moe_w_router'][l // 2], D_MODEL, LANES),
                               _pad_to(p['moe_b_router'][l // 2][None, :], 1, LANES)))
        layers.append(lay)
    return dict(fwd=fwd[:, :, :seq], inv=inv, cos=cos, sin=sin, fq=fq, fk=fk, layers=layers)


def _trunk(x, c, p, prep):
    mods = _ada(c, p['w_ada'], p['b_ada'])
    nb = x.shape[0]
    for l, lay in enumerate(prep['layers']):
        mod = mods[l].reshape(nb, 6, D_MODEL)
        z = _inproj(x, mod, lay['w_in'], lay['b_in'], lay['conv_w'], lay['conv_b'])
        hy = COL_HY // HY_WIDTH
        u = _longconv(z, hy + 2, z, hy, prep['fwd'], prep['inv'], lay['kf'], 0, lay['hy_bias'][0:1])
        ya = _longconv(u, 0, z, hy + 1, prep['fwd'], prep['inv'], lay['kf'], 1, lay['hy_bias'][1:2])
        yb = _diff_attn(z, prep['fq'], prep['fk'], lay['lam'], lay['subln'], lay['lam_init'])
        yc = _gqa_attn(z, prep['cos'], prep['sin'], lay['gq'], lay['gk'])
        merged = _merge(ya, yb, yc, z, x, mod, lay['wa'], lay['wb'], lay['wc'], lay['wo'],
                        lay['ln1_g'], lay['ln1_b'], lay.get('router'))
        if l % 2 == 0:
            x, h = merged
            x = _ffn(h, lay['w_up'], lay['w_down'], x, mod, lay['ln2_g'], lay['ln2_b'])
        else:
            x, h, comb = merged
            x = _moe(h, comb, lay['w_up'], lay['w_down'], x, mod, lay['ln2_g'], lay['ln2_b'])
    return x


def kernel(x_prompt, x_sample, c_prompt, c_sample, w_ada, b_ada, w_in, b_in, hy_conv_w, hy_conv_b, hy_f_w1, hy_f_b1, hy_f_w2, hy_f_b2, hy_f_w3, hy_f_b3, hy_bias, da_lam_q1, da_lam_k1, da_lam_q2, da_lam_k2, da_subln_g, gq_qnorm_g, gq_knorm_g, w_br_a, w_br_b, w_br_c, w_out, ln1_g, ln1_b, ffn_w_up, ffn_w_down, moe_w_router, moe_b_router, moe_w_up, moe_w_down, ln2_g, ln2_b):
    p = dict(w_ada=w_ada, b_ada=b_ada, w_in=w_in, b_in=b_in,
             hy_conv_w=hy_conv_w, hy_conv_b=hy_conv_b, hy_f_w1=hy_f_w1, hy_f_b1=hy_f_b1,
             hy_f_w2=hy_f_w2, hy_f_b2=hy_f_b2, hy_f_w3=hy_f_w3, hy_f_b3=hy_f_b3, hy_bias=hy_bias,
             da_lam_q1=da_lam_q1, da_lam_k1=da_lam_k1, da_lam_q2=da_lam_q2, da_lam_k2=da_lam_k2,
             da_subln_g=da_subln_g, gq_qnorm_g=gq_qnorm_g, gq_knorm_g=gq_knorm_g,
             w_br_a=w_br_a, w_br_b=w_br_b, w_br_c=w_br_c, w_out=w_out, ln1_g=ln1_g, ln1_b=ln1_b,
             ffn_w_up=ffn_w_up, ffn_w_down=ffn_w_down, moe_w_router=moe_w_router,
             moe_b_router=moe_b_router, moe_w_up=moe_w_up, moe_w_down=moe_w_down,
             ln2_g=ln2_g, ln2_b=ln2_b)
    assert x_prompt.shape[1] == x_sample.shape[1]
    prep = _prepare(x_prompt.shape[1], p)
    return (_trunk(x_prompt, c_prompt, p, prep), _trunk(x_sample, c_sample, p, prep))
```
